```python
import math
import jax
import jax.numpy as jnp
from jax import lax
import numpy as np

D_MODEL = 2048
BATCH = 4
SEQ = 4096
DEPTH = 2

CONV_CH = D_MODEL // 2
CONV_WIDTH = 31
MLSTM_HEADS = 8
MLSTM_HEAD_DIM = (D_MODEL // 2) // MLSTM_HEADS
MLSTM_WIDTH = MLSTM_HEADS * MLSTM_HEAD_DIM
MLSTM_QK_CONV = 4
MLSTM_CHUNK = 128
ATTN_HEADS = 8
ATTN_HEAD_DIM = (D_MODEL // 2) // ATTN_HEADS
ATTN_WIDTH = ATTN_HEADS * ATTN_HEAD_DIM
DILATED_BRANCHES = ((128, 1), (512, 4), (2048, 16))
ATTN_BLOCK = 128
SSM_CH = D_MODEL // 2
SSM_GROUP = 16
SSM_GROUPS = SSM_CH // SSM_GROUP
SSM_STATE = 64
DT_MIN = 1e-3
DT_MAX = 1e-1
D_FF = 5632
N_EXPERTS = 8
TOP_K = 2
RMS_EPS = 1e-6
LN_EPS = 1e-5
MIX0_WIDTH = CONV_CH + MLSTM_WIDTH
MIX1_WIDTH = ATTN_WIDTH + SSM_CH
IN0_COLS = 2 * CONV_CH + 4 * MLSTM_WIDTH + 2 * MLSTM_HEADS
IN1_COLS = 3 * ATTN_WIDTH + SSM_CH

kernel_name = 'hybrid_conv_mlstm_dilatedattn_s5_moe'


def rmsnorm(x, g):
    xf = x.astype(jnp.float32)
    y = xf * lax.rsqrt(jnp.mean(xf * xf, axis=-1, keepdims=True) + RMS_EPS)
    return (y * g.astype(jnp.float32)).astype(x.dtype)


def causal_depthwise_conv(x, w, b):
    k_len, ch = w.shape
    y = lax.conv_general_dilated(x, w[:, None, :].astype(x.dtype), window_strides=(1,),
                                 padding=[(k_len - 1, 0)],
                                 dimension_numbers=('NWC', 'WIO', 'NWC'),
                                 feature_group_count=ch)
    return y + b.astype(x.dtype)


def swiglu(x, w1, w3, w2):
    return (jax.nn.silu(x @ w1) * (x @ w3)) @ w2


def conformer_conv(u, conv_w, conv_b, ln_g, ln_b):
    val, gate = jnp.split(u, 2, axis=-1)
    a = causal_depthwise_conv(val * jax.nn.sigmoid(gate), conv_w, conv_b)
    af = a.astype(jnp.float32)
    mu = jnp.mean(af, axis=-1, keepdims=True)
    var = jnp.mean(jnp.square(af - mu), axis=-1, keepdims=True)
    an = (af - mu) * lax.rsqrt(var + LN_EPS) * ln_g.astype(jnp.float32) + ln_b.astype(jnp.float32)
    return jax.nn.silu(an).astype(u.dtype)


def mlstm_chunkwise(q, k, v, i_pre, f_pre):
    bsz, seq, nh, dh = q.shape
    L = MLSTM_CHUNK
    nc = seq // L
    f32 = jnp.float32

    def chunks(t):
        return t.astype(f32).reshape(bsz, nc, L, nh, -1).transpose(1, 0, 3, 2, 4)

    def gate_chunks(t):
        return t.astype(f32).reshape(bsz, nc, L, nh).transpose(1, 0, 3, 2)

    qc, kc, vc = chunks(q), chunks(k) * (dh ** -0.5), chunks(v)
    ic = gate_chunks(i_pre)
    bc = jnp.cumsum(jax.nn.log_sigmoid(gate_chunks(f_pre)), axis=-1)
    causal = jnp.tril(jnp.ones((L, L), dtype=bool))

    def step(carry, xs):
        c_mat, n_vec, m = carry
        q_, k_, v_, i_, b_ = xs
        log_d = jnp.where(causal, b_[..., :, None] - b_[..., None, :] + i_[..., None, :], -jnp.inf)
        inter = b_ + m[..., None]
        m_t = jnp.maximum(inter, jnp.max(log_d, axis=-1))
        s = jnp.einsum('bhtd,bhsd->bhts', q_, k_) * jnp.exp(log_d - m_t[..., None])
        w_int = jnp.exp(inter - m_t)
        num = jnp.einsum('bhts,bhsd->bhtd', s, v_) + w_int[..., None] * jnp.einsum('bhtd,bhde->bhte', q_, c_mat)
        den = jnp.sum(s, axis=-1) + w_int * jnp.einsum('bhtd,bhd->bht', q_, n_vec)
        h = num / jnp.maximum(jnp.abs(den), jnp.exp(-m_t))[..., None]
        b_last = b_[..., -1]
        g = b_last[..., None] - b_ + i_
        m_new = jnp.maximum(b_last + m, jnp.max(g, axis=-1))
        w_k = jnp.exp(g - m_new[..., None])
        decay = jnp.exp(b_last + m - m_new)
        c_mat = decay[..., None, None] * c_mat + jnp.einsum('bhs,bhsd,bhse->bhde', w_k, k_, v_)
        n_vec = decay[..., None] * n_vec + jnp.einsum('bhs,bhsd->bhd', w_k, k_)
        return (c_mat, n_vec, m_new), h

    init = (jnp.zeros((bsz, nh, dh, dh), f32), jnp.zeros((bsz, nh, dh), f32), jnp.zeros((bsz, nh), f32))
    _, h = lax.scan(step, init, (qc, kc, vc, ic, bc))
    return h.transpose(1, 0, 3, 2, 4).reshape(bsz, seq, nh, dh)


def dilated_branch(q, k, v, window, dil):
    bsz, seq, nh, dh = q.shape
    T = ATTN_BLOCK
    w_sub = window // dil
    ls = seq // dil
    nb = -(-ls // T)
    pad = nb * T - ls

    def sub(t):
        t = t.reshape(bsz, ls, dil, nh, dh).transpose(0, 2, 3, 1, 4)
        t = jnp.pad(t, ((0, 0), (0, 0), (0, 0), (0, pad), (0, 0)))
        return t.reshape(bsz, dil, nh, nb, T, dh)

    def band(t):
        prev = jnp.pad(t, ((0, 0), (0, 0), (0, 0), (1, 0), (0, 0), (0, 0)))[:, :, :, :nb]
        return jnp.concatenate([prev, t], axis=4)

    qb = sub(q)
    kk, vv = band(sub(k)), band(sub(v))
    s = jnp.einsum('brhnqd,brhnkd->brhnqk', qb, kk)
    qi = jnp.arange(T)[:, None]
    ki = jnp.arange(2 * T)[None, :]
    dist = T + qi - ki
    kpos = jnp.arange(nb)[:, None, None] * T - T + ki[None]
    valid = (dist >= 0) & (dist <= w_sub) & (kpos >= 0)
    s = jnp.where(valid, s, -jnp.inf)
    m = jnp.max(s, axis=-1, keepdims=True)
    p = jnp.exp(s - m)
    l = jnp.sum(p, axis=-1, keepdims=True)
    o = jnp.einsum('brhnqk,brhnkd->brhnqd', p, vv) / l
    lse = (m + jnp.log(l))[..., 0]
    o = o.reshape(bsz, dil, nh, nb * T, dh)[:, :, :, :ls].transpose(0, 3, 1, 2, 4).reshape(bsz, seq, nh, dh)
    lse = lse.reshape(bsz, dil, nh, nb * T)[:, :, :, :ls].transpose(0, 3, 1, 2).reshape(bsz, seq, nh)
    return o, lse


def dilated_attention(q, k, v):
    dh = q.shape[-1]
    qf = q.astype(jnp.float32) * (dh ** -0.5)
    kf, vf = k.astype(jnp.float32), v.astype(jnp.float32)
    results = [dilated_branch(qf, kf, vf, w, r) for (w, r) in DILATED_BRANCHES]
    outs = jnp.stack([o for (o, _) in results], axis=0)
    lses = jnp.stack([s for (_, s) in results], axis=0)
    alpha = jax.nn.softmax(lses, axis=0)
    return jnp.einsum('gbsh,gbshd->bshd', alpha, outs)


def _linear_recurrence_op(e1, e2):
    a1, b1 = e1
    a2, b2 = e2
    return a1 * a2, a2 * b1 + b2


def s5_layer(u, lam_re, lam_im, log_dt, b_re, b_im, c_re, c_im, d_skip, w_glu, b_glu):
    f32 = jnp.float32
    bsz, seq, _ = u.shape
    uf = u.astype(f32).reshape(bsz, seq, SSM_GROUPS, SSM_GROUP)
    lam = lax.complex(lam_re.astype(f32), lam_im.astype(f32))
    dt = jnp.exp(log_dt.astype(f32))[:, None]
    lam_bar = jnp.exp(lam * dt)
    b_mat = lax.complex(b_re.astype(f32), b_im.astype(f32))
    b_bar = ((lam_bar - 1.0) / lam)[..., None] * b_mat
    bu = jnp.einsum('bsgc,gpc->bsgp', uf.astype(jnp.complex64), b_bar)
    a = jnp.broadcast_to(lam_bar[None, None], (1, seq) + lam_bar.shape)
    _, state = lax.associative_scan(_linear_recurrence_op, (a, bu), axis=1)
    c_mat = lax.complex(c_re.astype(f32), c_im.astype(f32))
    y = jnp.einsum('bsgp,gcp->bsgc', state, c_mat).real + d_skip.astype(f32).reshape(SSM_GROUPS, SSM_GROUP) * uf
    y = jax.nn.gelu(y.reshape(bsz, seq, SSM_CH))
    y = y * jax.nn.sigmoid(y @ w_glu.astype(f32) + b_glu.astype(f32))
    return y.astype(u.dtype)


def moe_swiglu(x, router, w1, w3, w2):
    bsz, seq, d = x.shape
    xt = x.reshape(-1, d)
    logits = xt.astype(jnp.float32) @ router.astype(jnp.float32)
    top_v, top_i = lax.top_k(logits, TOP_K)
    gates = jax.nn.softmax(top_v, axis=-1)
    combine = jnp.einsum('tk,tke->te', gates, jax.nn.one_hot(top_i, N_EXPERTS, dtype=jnp.float32)).astype(x.dtype)
    y = jnp.zeros_like(xt)
    for e in range(N_EXPERTS):
        y = y + combine[:, e:e + 1] * swiglu(xt, w1[e], w3[e], w2[e])
    return y.reshape(bsz, seq, d)


def layer_conv_mlstm(x, norm_mix, w_in, conv_w, conv_b, ln_g, ln_b, qk_conv_w, qk_conv_b, b_i, b_f, w_out):
    bsz, seq, _ = x.shape
    z = rmsnorm(x, norm_mix) @ w_in
    cuts = [2 * CONV_CH, 2 * CONV_CH + 2 * MLSTM_WIDTH, 2 * CONV_CH + 3 * MLSTM_WIDTH,
            2 * CONV_CH + 4 * MLSTM_WIDTH, 2 * CONV_CH + 4 * MLSTM_WIDTH + MLSTM_HEADS]
    u_a, z_qk, z_v, z_o, z_i, z_f = jnp.split(z, cuts, axis=-1)
    out_a = conformer_conv(u_a, conv_w, conv_b, ln_g, ln_b)
    qk = jax.nn.silu(causal_depthwise_conv(z_qk, qk_conv_w, qk_conv_b))
    q, k = jnp.split(qk, 2, axis=-1)
    shp = (bsz, seq, MLSTM_HEADS, MLSTM_HEAD_DIM)
    i_pre = z_i.astype(jnp.float32) + b_i.astype(jnp.float32)
    f_pre = z_f.astype(jnp.float32) + b_f.astype(jnp.float32)
    h = mlstm_chunkwise(q.reshape(shp), k.reshape(shp), z_v.reshape(shp), i_pre, f_pre)
    out_b = (jax.nn.sigmoid(z_o.astype(jnp.float32)) * h.reshape(bsz, seq, MLSTM_WIDTH)).astype(x.dtype)
    return x + jnp.concatenate([out_a, out_b], axis=-1) @ w_out


def layer_attn_ssm(x, norm_mix, w_in, lam_re, lam_im, log_dt, b_re, b_im, c_re, c_im, d_skip, w_glu, b_glu, w_out):
    bsz, seq, _ = x.shape
    z = rmsnorm(x, norm_mix) @ w_in
    q, k, v, u = jnp.split(z, [ATTN_WIDTH, 2 * ATTN_WIDTH, 3 * ATTN_WIDTH], axis=-1)
    shp = (bsz, seq, ATTN_HEADS, ATTN_HEAD_DIM)
    out_c = dilated_attention(q.reshape(shp), k.reshape(shp), v.reshape(shp)).reshape(bsz, seq, ATTN_WIDTH).astype(x.dtype)
    out_d = s5_layer(u, lam_re, lam_im, log_dt, b_re, b_im, c_re, c_im, d_skip, w_glu, b_glu)
    return x + jnp.concatenate([out_c, out_d], axis=-1) @ w_out


def setup_inputs(seed: int = 0) -> dict:
    key = jax.random.key(seed)
    ks = iter(jax.random.split(key, 64))
    f32 = jnp.float32
    D = D_MODEL
    ne, no = (DEPTH + 1) // 2, DEPTH // 2
    G, P, H = SSM_GROUPS, SSM_STATE, MLSTM_HEADS

    def nrm(shape, scale):
        return jax.random.normal(next(ks), shape, f32) * scale

    def gain(shape):
        return 1.0 + nrm(shape, 0.02)

    inp = {}
    inp['x'] = nrm((BATCH, SEQ, D), 1.0)
    inp['e_norm_mix'] = gain((ne, D))
    inp['e_w_in'] = nrm((ne, D, IN0_COLS), D ** -0.5)
    inp['e_conv_w'] = nrm((ne, CONV_WIDTH, CONV_CH), CONV_WIDTH ** -0.5)
    inp['e_conv_b'] = nrm((ne, CONV_CH), 0.02)
    inp['e_ln_g'] = gain((ne, CONV_CH))
    inp['e_ln_b'] = nrm((ne, CONV_CH), 0.02)
    inp['e_qk_conv_w'] = nrm((ne, MLSTM_QK_CONV, 2 * MLSTM_WIDTH), MLSTM_QK_CONV ** -0.5)
    inp['e_qk_conv_b'] = nrm((ne, 2 * MLSTM_WIDTH), 0.02)
    inp['e_b_i'] = nrm((ne, H), 0.1)
    inp['e_b_f'] = jnp.linspace(3.0, 6.0, H, dtype=f32) + nrm((ne, H), 0.1)
    inp['e_w_out'] = nrm((ne, MIX0_WIDTH, D), MIX0_WIDTH ** -0.5)
    inp['e_norm_ffn'] = gain((ne, D))
    inp['e_ffn_w1'] = nrm((ne, D, D_FF), D ** -0.5)
    inp['e_ffn_w3'] = nrm((ne, D, D_FF), D ** -0.5)
    inp['e_ffn_w2'] = nrm((ne, D_FF, D), D_FF ** -0.5)
    inp['o_norm_mix'] = gain((no, D))
    inp['o_w_in'] = nrm((no, D, IN1_COLS), D ** -0.5)
    inp['o_lam_re'] = -0.5 + nrm((no, G, P), 0.01)
    inp['o_lam_im'] = math.pi * jnp.arange(P, dtype=f32) + nrm((no, G, P), 0.01)
    inp['o_log_dt'] = jax.random.uniform(next(ks), (no, G), f32, math.log(DT_MIN), math.log(DT_MAX))
    inp['o_b_re'] = nrm((no, G, P, SSM_GROUP), (2 * SSM_GROUP) ** -0.5)
    inp['o_b_im'] = nrm((no, G, P, SSM_GROUP), (2 * SSM_GROUP) ** -0.5)
    inp['o_c_re'] = nrm((no, G, SSM_GROUP, P), (2 * P) ** -0.5)
    inp['o_c_im'] = nrm((no, G, SSM_GROUP, P), (2 * P) ** -0.5)
    inp['o_d_skip'] = nrm((no, SSM_CH), 1.0)
    inp['o_w_glu'] = nrm((no, SSM_CH, SSM_CH), SSM_CH ** -0.5)
    inp['o_b_glu'] = nrm((no, SSM_CH), 0.02)
    inp['o_w_out'] = nrm((no, MIX1_WIDTH, D), MIX1_WIDTH ** -0.5)
    inp['o_norm_ffn'] = gain((no, D))
    inp['o_router'] = nrm((no, D, N_EXPERTS), D ** -0.5)
    inp['o_exp_w1'] = nrm((no, N_EXPERTS, D, D_FF), D ** -0.5)
    inp['o_exp_w3'] = nrm((no, N_EXPERTS, D, D_FF), D ** -0.5)
    inp['o_exp_w2'] = nrm((no, N_EXPERTS, D_FF, D), D_FF ** -0.5)
    inp['final_norm'] = gain((D,))
    return inp


def reference(x, e_norm_mix, e_w_in, e_conv_w, e_conv_b, e_ln_g, e_ln_b, e_qk_conv_w, e_qk_conv_b,
              e_b_i, e_b_f, e_w_out, e_norm_ffn, e_ffn_w1, e_ffn_w3, e_ffn_w2,
              o_norm_mix, o_w_in, o_lam_re, o_lam_im, o_log_dt, o_b_re, o_b_im, o_c_re, o_c_im,
              o_d_skip, o_w_glu, o_b_glu, o_w_out, o_norm_ffn, o_router, o_exp_w1, o_exp_w3, o_exp_w2,
              final_norm):
    for layer in range(DEPTH):
        j = layer // 2
        if layer % 2 == 0:
            x = layer_conv_mlstm(x, e_norm_mix[j], e_w_in[j], e_conv_w[j], e_conv_b[j], e_ln_g[j], e_ln_b[j],
                                 e_qk_conv_w[j], e_qk_conv_b[j], e_b_i[j], e_b_f[j], e_w_out[j])
            x = x + swiglu(rmsnorm(x, e_norm_ffn[j]), e_ffn_w1[j], e_ffn_w3[j], e_ffn_w2[j])
        else:
            x = layer_attn_ssm(x, o_norm_mix[j], o_w_in[j], o_lam_re[j], o_lam_im[j], o_log_dt[j],
                               o_b_re[j], o_b_im[j], o_c_re[j], o_c_im[j], o_d_skip[j], o_w_glu[j],
                               o_b_glu[j], o_w_out[j])
            x = x + moe_swiglu(rmsnorm(x, o_norm_ffn[j]), o_router[j], o_exp_w1[j], o_exp_w3[j], o_exp_w2[j])
    return rmsnorm(x, final_norm)
```

```python
import functools
import math

import jax
import jax.numpy as jnp
from jax import lax
from jax.experimental import pallas as pl
from jax.experimental.pallas import tpu as pltpu

F32 = jnp.float32
BF16 = jnp.bfloat16

RMS_EPS = 1e-6
LN_EPS = 1e-5
CONV_WIDTH = 31
MLSTM_HEADS = 8
MLSTM_QK_CONV = 4
MLSTM_CHUNK = 128
ATTN_HEADS = 8
ATTN_BLOCK = 128
DILATIONS = (1, 4, 16)
SSM_GROUP = 16
SSM_STATE = 64
N_EXPERTS = 8
LANES = 128
SUBLANES = 8
VMEM_LIMIT = 56 * 1024 * 1024


def _cparams(*sem):
    return pltpu.CompilerParams(dimension_semantics=sem, vmem_limit_bytes=VMEM_LIMIT)


def _rms(x, g):
    return x * lax.rsqrt(jnp.mean(x * x, axis=-1, keepdims=True) + RMS_EPS) * g


def _norm_matmul_kernel(x_ref, g_ref, w_ref, o_ref, xn_ref):
    @pl.when(pl.program_id(1) == 0)
    def _():
        xn_ref[...] = _rms(x_ref[...], g_ref[...]).astype(BF16)

    o_ref[...] = jnp.dot(xn_ref[...], w_ref[...], preferred_element_type=F32).astype(o_ref.dtype)


def _norm_matmul_aux_kernel(x_ref, g_ref, w_ref, wa_ref, o_ref, oa_ref, xn_ref):
    @pl.when(pl.program_id(1) == 0)
    def _():
        xn = _rms(x_ref[...], g_ref[...]).astype(BF16)
        xn_ref[...] = xn
        oa_ref[...] = jnp.dot(xn, wa_ref[...], preferred_element_type=F32)

    o_ref[...] = jnp.dot(xn_ref[...], w_ref[...], preferred_element_type=F32).astype(o_ref.dtype)


def norm_matmul(x, g, w, w_aux=None, *, tm, tn):
    t, d = x.shape
    n = w.shape[1]
    grid = (t // tm, n // tn)
    x_spec = pl.BlockSpec((tm, d), lambda i, j: (i, 0))
    g_spec = pl.BlockSpec((1, d), lambda i, j: (0, 0))
    w_spec = pl.BlockSpec((d, tn), lambda i, j: (0, j))
    o_spec = pl.BlockSpec((tm, tn), lambda i, j: (i, j))
    scratch = [pltpu.VMEM((tm, d), BF16)]
    if w_aux is None:
        return pl.pallas_call(
            _norm_matmul_kernel, grid=grid, in_specs=[x_spec, g_spec, w_spec], out_specs=o_spec,
            out_shape=jax.ShapeDtypeStruct((t, n), BF16), scratch_shapes=scratch,
            compiler_params=_cparams("parallel", "arbitrary"), name="norm_matmul")(x, g, w)
    na = w_aux.shape[1]
    return pl.pallas_call(
        _norm_matmul_aux_kernel, grid=grid,
        in_specs=[x_spec, g_spec, w_spec, pl.BlockSpec((d, na), lambda i, j: (0, 0))],
        out_specs=[o_spec, pl.BlockSpec((tm, na), lambda i, j: (i, 0))],
        out_shape=[jax.ShapeDtypeStruct((t, n), BF16), jax.ShapeDtypeStruct((t, na), F32)],
        scratch_shapes=scratch, compiler_params=_cparams("parallel", "arbitrary"),
        name="norm_matmul_aux")(x, g, w, w_aux)


CONV_HALO = 32
CONV_ROWS = 32
CONV_LANES = 256


def _conv_kernel(u_ref, halo_ref, w_ref, cb_ref, lg_ref, lb_ref, o_ref, buf_ref, xs_ref, acc_ref, *, ts, ch):
    first = pl.program_id(1) == 0

    def glu(u):
        u = u.astype(F32)
        return u[:, :ch] * jax.nn.sigmoid(u[:, ch:])

    buf_ref[0:CONV_HALO, :] = jnp.where(first, 0.0, glu(halo_ref[...]))
    buf_ref[CONV_HALO:CONV_HALO + ts, :] = glu(u_ref[...])
    n_shift = ts + CONV_HALO - SUBLANES
    for b in range(1, SUBLANES):
        xs_ref[b, 0:n_shift, :] = buf_ref[b:b + n_shift, :]

    lead = CONV_HALO - (CONV_WIDTH - 1)

    def conv_step(r, carry):
        base = pl.multiple_of(r * CONV_ROWS, CONV_ROWS)
        for lc in range(ch // CONV_LANES):
            cols = slice(lc * CONV_LANES, (lc + 1) * CONV_LANES)
            acc = jnp.broadcast_to(cb_ref[:, cols], (CONV_ROWS, CONV_LANES))
            for k in range(CONV_WIDTH):
                a, b = divmod(lead + k, SUBLANES)
                rows = pl.ds(base + SUBLANES * a, CONV_ROWS)
                xk = buf_ref[rows, cols] if b == 0 else xs_ref[b, rows, cols]
                acc = acc + w_ref[k:k + 1, cols] * xk
            acc_ref[pl.ds(base, CONV_ROWS), cols] = acc
        return carry

    lax.fori_loop(0, ts // CONV_ROWS, conv_step, 0)

    def norm_step(r, carry):
        base = pl.multiple_of(r * CONV_ROWS, CONV_ROWS)
        a = acc_ref[pl.ds(base, CONV_ROWS), :]
        mu = jnp.mean(a, axis=-1, keepdims=True)
        d = a - mu
        var = jnp.mean(d * d, axis=-1, keepdims=True)
        an = d * lax.rsqrt(var + LN_EPS) * lg_ref[...] + lb_ref[...]
        o_ref[pl.ds(base, CONV_ROWS), :] = (an * jax.nn.sigmoid(an)).astype(o_ref.dtype)
        return carry

    lax.fori_loop(0, ts // CONV_ROWS, norm_step, 0)


def conformer_conv(z, conv_w, conv_b, ln_g, ln_b, *, batch, seq, ts):
    ch = conv_w.shape[1]
    nts = seq // ts
    halo_blocks = ts // CONV_HALO
    kern = functools.partial(_conv_kernel, ts=ts, ch=ch)
    vec = lambda: pl.BlockSpec((1, ch), lambda b, i: (0, 0))
    return pl.pallas_call(
        kern, grid=(batch, nts),
        in_specs=[
            pl.BlockSpec((ts, 2 * ch), lambda b, i: (b * nts + i, 0)),
            pl.BlockSpec((CONV_HALO, 2 * ch),
                         lambda b, i: (jnp.maximum((b * nts + i) * halo_blocks - 1, 0), 0)),
            pl.BlockSpec((CONV_WIDTH, ch), lambda b, i: (0, 0)),
            vec(), vec(), vec(),
        ],
        out_specs=pl.BlockSpec((ts, ch), lambda b, i: (b * nts + i, 0)),
        out_shape=jax.ShapeDtypeStruct((batch * seq, ch), BF16),
        scratch_shapes=[
            pltpu.VMEM((CONV_HALO + ts, ch), F32),
            pltpu.VMEM((SUBLANES, CONV_HALO + ts, ch), F32),
            pltpu.VMEM((ts, ch), F32),
        ],
        compiler_params=_cparams("parallel", "arbitrary"), name="conformer_conv",
    )(z, z, conv_w, conv_b, ln_g, ln_b)


QK_HALO = 16


def _mlstm_kernel(zqk_ref, halo_ref, zv_ref, zo_ref, g_ref, cw_ref, cb_ref, gb_ref, o_ref,
                  qb_ref, c_ref, n_ref, m_ref, *, nh, dh):
    L = MLSTM_CHUNK
    first = pl.program_id(1) == 0

    @pl.when(first)
    def _():
        c_ref[...] = jnp.zeros_like(c_ref)
        n_ref[...] = jnp.zeros_like(n_ref)
        m_ref[...] = jnp.zeros_like(m_ref)

    qb_ref[0:QK_HALO, :] = jnp.where(first, 0.0, halo_ref[...].astype(F32))
    qb_ref[QK_HALO:QK_HALO + L, :] = zqk_ref[...].astype(F32)
    lead = QK_HALO - (MLSTM_QK_CONV - 1)
    acc = jnp.broadcast_to(cb_ref[...], (L, 2 * nh * dh))
    for k in range(MLSTM_QK_CONV):
        acc = acc + cw_ref[k:k + 1, :] * qb_ref[lead + k:lead + k + L, :]
    qk = acc * jax.nn.sigmoid(acc)

    g = g_ref[...] + gb_ref[...]
    logf = jax.nn.log_sigmoid(g)
    row = lax.broadcasted_iota(jnp.int32, (L, L), 0)
    col = lax.broadcasted_iota(jnp.int32, (L, L), 1)
    causal = col <= row
    tri = causal.astype(F32)
    bcum = jnp.dot(tri, logf, preferred_element_type=F32, precision=lax.Precision.HIGHEST)
    g_t = g.T
    b_t = bcum.T
    scale = dh ** -0.5

    for h in range(nh):
        q = qk[:, h * dh:(h + 1) * dh].astype(BF16)
        kf = qk[:, (nh + h) * dh:(nh + h + 1) * dh] * scale
        k = kf.astype(BF16)
        v = zv_ref[:, h * dh:(h + 1) * dh]
        b_col = bcum[:, nh + h:nh + h + 1]
        b_row = b_t[nh + h:nh + h + 1, :]
        i_col = g[:, h:h + 1]
        i_row = g_t[h:h + 1, :]
        m_prev = m_ref[h, 0:1, 0:1]
        c_prev = c_ref[h]
        n_prev = n_ref[h, 0:1, :]

        log_d = jnp.where(causal, b_col - b_row + i_row, -jnp.inf)
        inter = b_col + m_prev
        m_t = jnp.maximum(inter, jnp.max(log_d, axis=-1, keepdims=True))
        s = lax.dot_general(q, k, (((1,), (1,)), ((), ())), preferred_element_type=F32)
        s = s * jnp.exp(log_d - m_t)
        w_int = jnp.exp(inter - m_t)
        num = (jnp.dot(s.astype(BF16), v, preferred_element_type=F32)
               + w_int * jnp.dot(q, c_prev.astype(BF16), preferred_element_type=F32))
        qn = jnp.sum(q.astype(F32) * n_prev, axis=-1, keepdims=True)
        den = jnp.sum(s, axis=-1, keepdims=True) + w_int * qn
        hval = num / jnp.maximum(jnp.abs(den), jnp.exp(-m_t))

        b_last = b_col[L - 1:L, :]
        gk = b_last - b_col + i_col
        m_new = jnp.maximum(b_last + m_prev, jnp.max(gk, axis=0, keepdims=True))
        w_k = jnp.exp(gk - m_new)
        decay = jnp.exp(b_last + m_prev - m_new)
        kw = kf * w_k
        c_ref[h] = decay * c_prev + lax.dot_general(
            kw.astype(BF16), v, (((0,), (0,)), ((), ())), preferred_element_type=F32)
        n_ref[h] = jnp.broadcast_to(decay * n_prev + jnp.sum(kw, axis=0, keepdims=True), (SUBLANES, dh))
        m_ref[h] = jnp.broadcast_to(m_new, (SUBLANES, LANES))

        gate_o = jax.nn.sigmoid(zo_ref[:, h * dh:(h + 1) * dh].astype(F32))
        o_ref[:, h * dh:(h + 1) * dh] = (gate_o * hval).astype(o_ref.dtype)


def mlstm(z, gates, qk_conv_w, qk_conv_b, gate_bias, *, batch, seq, col0):
    L = MLSTM_CHUNK
    nh = MLSTM_HEADS
    w = qk_conv_w.shape[1] // 2
    dh = w // nh
    nc = seq // L
    qk_blk = col0 // (2 * w)
    v_blk = (col0 + 2 * w) // w
    o_blk = v_blk + 1
    kern = functools.partial(_mlstm_kernel, nh=nh, dh=dh)
    return pl.pallas_call(
        kern, grid=(batch, nc),
        in_specs=[
            pl.BlockSpec((L, 2 * w), lambda b, c: (b * nc + c, qk_blk)),
            pl.BlockSpec((QK_HALO, 2 * w),
                         lambda b, c: (jnp.maximum((b * nc + c) * (L // QK_HALO) - 1, 0), qk_blk)),
            pl.BlockSpec((L, w), lambda b, c: (b * nc + c, v_blk)),
            pl.BlockSpec((L, w), lambda b, c: (b * nc + c, o_blk)),
            pl.BlockSpec((L, LANES), lambda b, c: (b * nc + c, 0)),
            pl.BlockSpec((MLSTM_QK_CONV, 2 * w), lambda b, c: (0, 0)),
            pl.BlockSpec((1, 2 * w), lambda b, c: (0, 0)),
            pl.BlockSpec((1, LANES), lambda b, c: (0, 0)),
        ],
        out_specs=pl.BlockSpec((L, w), lambda b, c: (b * nc + c, 0)),
        out_shape=jax.ShapeDtypeStruct((batch * seq, w), BF16),
        scratch_shapes=[
            pltpu.VMEM((QK_HALO + L, 2 * w), F32),
            pltpu.VMEM((nh, dh, dh), F32),
            pltpu.VMEM((nh, SUBLANES, dh), F32),
            pltpu.VMEM((nh, SUBLANES, LANES), F32),
        ],
        compiler_params=_cparams("parallel", "arbitrary"), name="mlstm",
    )(z, z, z, z, gates, qk_conv_w, qk_conv_b, gate_bias)


def _out_proj_kernel(a_ref, b_ref, wa_ref, wb_ref, r_ref, o_ref):
    o_ref[...] = (r_ref[...]
                  + jnp.dot(a_ref[...], wa_ref[...], preferred_element_type=F32)
                  + jnp.dot(b_ref[...], wb_ref[...], preferred_element_type=F32))


def out_proj(a, b, wa, wb, res, *, tm):
    t, d = res.shape
    ka, kb = a.shape[1], b.shape[1]
    return pl.pallas_call(
        _out_proj_kernel, grid=(t // tm,),
        in_specs=[
            pl.BlockSpec((tm, ka), lambda i: (i, 0)),
            pl.BlockSpec((tm, kb), lambda i: (i, 0)),
            pl.BlockSpec((ka, d), lambda i: (0, 0)),
            pl.BlockSpec((kb, d), lambda i: (0, 0)),
            pl.BlockSpec((tm, d), lambda i: (i, 0)),
        ],
        out_specs=pl.BlockSpec((tm, d), lambda i: (i, 0)),
        out_shape=jax.ShapeDtypeStruct((t, d), F32),
        compiler_params=_cparams("parallel"), name="out_proj",
    )(a, b, wa, wb, res)


def _ffn_kernel(x_ref, g_ref, w1_ref, w3_ref, w2_ref, o_ref, xn_ref):
    f = pl.program_id(1)

    @pl.when(f == 0)
    def _():
        x = x_ref[...]
        xn_ref[...] = _rms(x, g_ref[...]).astype(BF16)
        o_ref[...] = x

    xn = xn_ref[...]
    a = jnp.dot(xn, w1_ref[...], preferred_element_type=F32)
    b = jnp.dot(xn, w3_ref[...], preferred_element_type=F32)
    hid = (a * jax.nn.sigmoid(a) * b).astype(BF16)
    o_ref[...] += jnp.dot(hid, w2_ref[...], preferred_element_type=F32)


def ffn(x, g, w1, w3, w2, *, tm, tf):
    t, d = x.shape
    dff = w1.shape[1]
    return pl.pallas_call(
        _ffn_kernel, grid=(t // tm, dff // tf),
        in_specs=[
            pl.BlockSpec((tm, d), lambda i, f: (i, 0)),
            pl.BlockSpec((1, d), lambda i, f: (0, 0)),
            pl.BlockSpec((d, tf), lambda i, f: (0, f)),
            pl.BlockSpec((d, tf), lambda i, f: (0, f)),
            pl.BlockSpec((tf, d), lambda i, f: (f, 0)),
        ],
        out_specs=pl.BlockSpec((tm, d), lambda i, f: (i, 0)),
        out_shape=jax.ShapeDtypeStruct((t, d), F32),
        scratch_shapes=[pltpu.VMEM((tm, d), BF16)],
        compiler_params=_cparams("parallel", "arbitrary"), name="ffn",
    )(x, g, w1, w3, w2)


def _attn_kernel(q_ref, k_ref, v_ref, o_ref, qf_ref, kf_ref, vf_ref, qd_ref, kd_ref, vd_ref,
                 od_ref, ld_ref, on_ref, ln_ref, *, seq, dh):
    T = ATTN_BLOCK
    nblk = seq // T
    qf_ref[...] = q_ref[...].astype(F32) * (dh ** -0.5)
    kf_ref[...] = k_ref[...].astype(F32)
    vf_ref[...] = v_ref[...].astype(F32)
    kd_ref[0:T, :] = jnp.zeros((T, dh), BF16)
    vd_ref[0:T, :] = jnp.zeros((T, dh), BF16)

    qi = lax.broadcasted_iota(jnp.int32, (T, 2 * T), 0)
    ki = lax.broadcasted_iota(jnp.int32, (T, 2 * T), 1)
    dist = T + qi - ki
    band = (dist >= 0) & (dist <= T)

    for g, dil in enumerate(DILATIONS):
        ls = seq // dil
        nb = ls // T
        for r in range(dil):
            rows = pl.ds(r, ls, stride=dil) if dil > 1 else pl.ds(0, ls)
            qd_ref[r * ls:(r + 1) * ls, :] = qf_ref[rows, :].astype(BF16)
            kd_ref[T + r * ls:T + (r + 1) * ls, :] = kf_ref[rows, :].astype(BF16)
            vd_ref[T + r * ls:T + (r + 1) * ls, :] = vf_ref[rows, :].astype(BF16)

        def block(n, carry):
            base = pl.multiple_of(n * T, T)
            qb = qd_ref[pl.ds(base, T), :]
            kb = kd_ref[pl.ds(base, 2 * T), :]
            vb = vd_ref[pl.ds(base, 2 * T), :]
            s = lax.dot_general(qb, kb, (((1,), (1,)), ((), ())), preferred_element_type=F32)
            kmin = jnp.where(n % nb == 0, T, 0)
            s = jnp.where(band & (ki >= kmin), s, -jnp.inf)
            m = jnp.max(s, axis=-1, keepdims=True)
            p = jnp.exp(s - m)
            l = jnp.sum(p, axis=-1, keepdims=True)
            o = jnp.dot(p.astype(BF16), vb, preferred_element_type=F32) / l
            od_ref[pl.ds(base, T), :] = o
            ld_ref[pl.ds(base, T), :] = jnp.broadcast_to(m + jnp.log(l), (T, dh))
            return carry

        lax.fori_loop(0, nblk, block, 0)

        for r in range(dil):
            rows = pl.ds(r, ls, stride=dil) if dil > 1 else pl.ds(0, ls)
            on_ref[g, rows, :] = od_ref[r * ls:(r + 1) * ls, :]
            ln_ref[g, rows, :] = ld_ref[r * ls:(r + 1) * ls, :]

    def merge(n, carry):
        rows = pl.ds(pl.multiple_of(n * T, T), T)
        lses = [ln_ref[g, rows, :] for g in range(len(DILATIONS))]
        mx = functools.reduce(jnp.maximum, lses)
        ws = [jnp.exp(l - mx) for l in lses]
        tot = functools.reduce(lambda a, b: a + b, ws)
        acc = ws[0] * on_ref[0, rows, :]
        for g in range(1, len(DILATIONS)):
            acc = acc + ws[g] * on_ref[g, rows, :]
        o_ref[rows, :] = (acc / tot).astype(o_ref.dtype)
        return carry

    lax.fori_loop(0, nblk, merge, 0)


def dilated_attention(z, *, batch, seq, width):
    nh = ATTN_HEADS
    dh = width // nh
    ng = len(DILATIONS)
    kern = functools.partial(_attn_kernel, seq=seq, dh=dh)
    blk = lambda off: pl.BlockSpec((seq, dh), lambda b, h: (b, off + h))
    return pl.pallas_call(
        kern, grid=(batch, nh),
        in_specs=[blk(0), blk(nh), blk(2 * nh)],
        out_specs=pl.BlockSpec((seq, dh), lambda b, h: (b, h)),
        out_shape=jax.ShapeDtypeStruct((batch * seq, width), BF16),
        scratch_shapes=[
            pltpu.VMEM((seq, dh), F32), pltpu.VMEM((seq, dh), F32), pltpu.VMEM((seq, dh), F32),
            pltpu.VMEM((seq, dh), BF16),
            pltpu.VMEM((ATTN_BLOCK + seq, dh), BF16), pltpu.VMEM((ATTN_BLOCK + seq, dh), BF16),
            pltpu.VMEM((seq, dh), F32), pltpu.VMEM((seq, dh), F32),
            pltpu.VMEM((ng, seq, dh), F32), pltpu.VMEM((ng, seq, dh), F32),
        ],
        compiler_params=_cparams("parallel", "parallel"), name="dilated_attention",
    )(z, z, z)


S5_SEGS = SUBLANES
S5_SEG = 64
S5_PITCH = S5_SEG + SUBLANES
S5_TILE = S5_SEGS * S5_SEG
S5_ROWS = S5_SEGS * S5_PITCH
S5_CHUNK_IN = 256
S5_SLABS = S5_CHUNK_IN // SSM_GROUP * SSM_STATE // LANES


def _s5_kernel(u_ref, bre_ref, bim_ref, cre_ref, cim_ref, lre_ref, lim_ref, dsk_ref, wg_ref, bg_ref,
               o_ref, ug_ref, sre_ref, sim_ref, cyre_ref, cyim_ref, y_ref, *, nchunk):
    first = pl.program_id(1) == 0

    @pl.when(first)
    def _():
        cyre_ref[...] = jnp.zeros_like(cyre_ref)
        cyim_ref[...] = jnp.zeros_like(cyim_ref)
        ug_ref[...] = jnp.zeros_like(ug_ref)

    for i in range(S5_SEGS):
        ug_ref[i * S5_PITCH:i * S5_PITCH + S5_SEG, :] = u_ref[i * S5_SEG:(i + 1) * S5_SEG, :].astype(F32)

    seg_id = lax.broadcasted_iota(jnp.int32, (S5_SEGS, LANES), 0)
    for c in range(nchunk):
        ucols = slice(c * S5_CHUNK_IN, (c + 1) * S5_CHUNK_IN)
        ub = ug_ref[:, ucols].astype(BF16)
        bu_re = jnp.dot(ub, bre_ref[c], preferred_element_type=F32)
        bu_im = jnp.dot(ub, bim_ref[c], preferred_element_type=F32)
        for s in range(S5_SLABS):
            sre_ref[s] = bu_re[:, s * LANES:(s + 1) * LANES]
            sim_ref[s] = bu_im[:, s * LANES:(s + 1) * LANES]

        lam_re = [jnp.broadcast_to(lre_ref[c, :, s * LANES:(s + 1) * LANES], (S5_SEGS, LANES))
                  for s in range(S5_SLABS)]
        lam_im = [jnp.broadcast_to(lim_ref[c, :, s * LANES:(s + 1) * LANES], (S5_SEGS, LANES))
                  for s in range(S5_SLABS)]

        def step(j, st, store):
            rows = pl.ds(j, S5_SEGS, stride=S5_PITCH)
            new = []
            for s in range(S5_SLABS):
                pr, pi = st[2 * s], st[2 * s + 1]
                nr = lam_re[s] * pr - lam_im[s] * pi + sre_ref[s, rows, :]
                ni = lam_re[s] * pi + lam_im[s] * pr + sim_ref[s, rows, :]
                if store:
                    sre_ref[s, rows, :] = nr
                    sim_ref[s, rows, :] = ni
                new += [nr, ni]
            return tuple(new)

        zero = tuple(jnp.zeros((S5_SEGS, LANES), F32) for _ in range(2 * S5_SLABS))
        ends = lax.fori_loop(0, S5_SEG, lambda j, st: step(j, st, False), zero)

        pw_re, pw_im = [l[0:1] for l in lam_re], [l[0:1] for l in lam_im]
        for _ in range(int(math.log2(S5_SEG))):
            pw_re, pw_im = ([a * a - b * b for a, b in zip(pw_re, pw_im)],
                            [2.0 * a * b for a, b in zip(pw_re, pw_im)])
        init = []
        for s in range(S5_SLABS):
            cols = slice(s * LANES, (s + 1) * LANES)
            cr, ci = cyre_ref[c, 0:1, cols], cyim_ref[c, 0:1, cols]
            in_re = jnp.zeros((S5_SEGS, LANES), F32)
            in_im = jnp.zeros((S5_SEGS, LANES), F32)
            for i in range(S5_SEGS):
                in_re = jnp.where(seg_id == i, cr, in_re)
                in_im = jnp.where(seg_id == i, ci, in_im)
                er, ei = ends[2 * s][i:i + 1], ends[2 * s + 1][i:i + 1]
                cr, ci = (pw_re[s] * cr - pw_im[s] * ci + er, pw_re[s] * ci + pw_im[s] * cr + ei)
            cyre_ref[c, :, cols] = jnp.broadcast_to(cr, (SUBLANES, LANES))
            cyim_ref[c, :, cols] = jnp.broadcast_to(ci, (SUBLANES, LANES))
            init += [in_re, in_im]

        lax.fori_loop(0, S5_SEG, lambda j, st: step(j, st, True), tuple(init))

        st_re = jnp.concatenate([sre_ref[s] for s in range(S5_SLABS)], axis=1).astype(BF16)
        st_im = jnp.concatenate([sim_ref[s] for s in range(S5_SLABS)], axis=1).astype(BF16)
        y_ref[:, ucols] = (jnp.dot(st_re, cre_ref[c], preferred_element_type=F32)
                           + jnp.dot(st_im, cim_ref[c], preferred_element_type=F32))

    for i in range(S5_SEGS):
        rows = slice(i * S5_PITCH, i * S5_PITCH + S5_SEG)
        y = y_ref[rows, :] + dsk_ref[...] * ug_ref[rows, :]
        y = jax.nn.gelu(y)
        gate = jnp.dot(y.astype(BF16), wg_ref[...], preferred_element_type=F32) + bg_ref[...]
        o_ref[i * S5_SEG:(i + 1) * S5_SEG, :] = (y * jax.nn.sigmoid(gate)).astype(o_ref.dtype)


def s5(z, b_re, b_im, c_re, c_im, lam_re, lam_im, d_skip, w_glu, b_glu, *, batch, seq, col_blk):
    nchunk, _, nstate = b_re.shape
    ch = nchunk * S5_CHUNK_IN
    nt = seq // S5_TILE
    kern = functools.partial(_s5_kernel, nchunk=nchunk)
    full = lambda shape: pl.BlockSpec(shape, lambda b, i: (0,) * len(shape))
    return pl.pallas_call(
        kern, grid=(batch, nt),
        in_specs=[
            pl.BlockSpec((S5_TILE, ch), lambda b, i: (b * nt + i, col_blk)),
            full(b_re.shape), full(b_im.shape), full(c_re.shape), full(c_im.shape),
            full(lam_re.shape), full(lam_im.shape), full((1, ch)), full((ch, ch)), full((1, ch)),
        ],
        out_specs=pl.BlockSpec((S5_TILE, ch), lambda b, i: (b * nt + i, 0)),
        out_shape=jax.ShapeDtypeStruct((batch * seq, ch), BF16),
        scratch_shapes=[
            pltpu.VMEM((S5_ROWS, ch), F32),
            pltpu.VMEM((S5_SLABS, S5_ROWS, LANES), F32),
            pltpu.VMEM((S5_SLABS, S5_ROWS, LANES), F32),
            pltpu.VMEM((nchunk, SUBLANES, nstate), F32),
            pltpu.VMEM((nchunk, SUBLANES, nstate), F32),
            pltpu.VMEM((S5_ROWS, ch), F32),
        ],
        compiler_params=_cparams("parallel", "arbitrary"), name="s5",
    )(z, b_re, b_im, c_re, c_im, lam_re, lam_im, d_skip, w_glu, b_glu)


def s5_params(lam_re, lam_im, log_dt, b_re, b_im, c_re, c_im):
    ng, ns = lam_re.shape
    gpc = S5_CHUNK_IN // SSM_GROUP
    nchunk = ng // gpc
    dt = jnp.exp(log_dt)[:, None]
    mag = jnp.exp(lam_re * dt)
    lb_re, lb_im = mag * jnp.cos(lam_im * dt), mag * jnp.sin(lam_im * dt)
    den = lam_re * lam_re + lam_im * lam_im
    f_re = ((lb_re - 1.0) * lam_re + lb_im * lam_im) / den
    f_im = (lb_im * lam_re - (lb_re - 1.0) * lam_im) / den
    bb_re = f_re[..., None] * b_re - f_im[..., None] * b_im
    bb_im = f_re[..., None] * b_im + f_im[..., None] * b_re
    eye = jnp.eye(gpc, dtype=F32)

    def pack_b(m):
        m = m.reshape(nchunk, gpc, ns, SSM_GROUP)
        return jnp.einsum('cgpk,gh->cgkhp', m, eye).reshape(nchunk, gpc * SSM_GROUP, gpc * ns).astype(BF16)

    def pack_c(m):
        m = m.reshape(nchunk, gpc, SSM_GROUP, ns)
        return jnp.einsum('cgkp,gh->cgphk', m, eye).reshape(nchunk, gpc * ns, gpc * SSM_GROUP).astype(BF16)

    return (pack_b(bb_re), pack_b(bb_im), pack_c(c_re), pack_c(-c_im),
            lb_re.reshape(nchunk, 1, gpc * ns), lb_im.reshape(nchunk, 1, gpc * ns))


def _out_proj_route_kernel(a_ref, b_ref, wa_ref, wb_ref, r_ref, g_ref, wr_ref, x_ref, xn_ref, eid_ref, gate_ref):
    x = (r_ref[...]
         + jnp.dot(a_ref[...], wa_ref[...], preferred_element_type=F32)
         + jnp.dot(b_ref[...], wb_ref[...], preferred_element_type=F32))
    x_ref[...] = x
    xn = _rms(x, g_ref[...])
    xn_ref[...] = xn
    logits = jnp.dot(xn, wr_ref[...], preferred_element_type=F32, precision=lax.Precision.HIGHEST)
    lane = lax.broadcasted_iota(jnp.int32, logits.shape, 1)
    logits = jnp.where(lane < N_EXPERTS, logits, -jnp.inf)
    v1 = jnp.max(logits, axis=-1, keepdims=True)
    i1 = jnp.min(jnp.where(logits == v1, lane, LANES), axis=-1, keepdims=True)
    rest = jnp.where(lane == i1, -jnp.inf, logits)
    v2 = jnp.max(rest, axis=-1, keepdims=True)
    i2 = jnp.min(jnp.where(rest == v2, lane, LANES), axis=-1, keepdims=True)
    e2 = jnp.exp(v2 - v1)
    g1 = 1.0 / (1.0 + e2)
    g2 = e2 / (1.0 + e2)
    eid_ref[...] = jnp.where(lane == 0, i1, jnp.where(lane == 1, i2, 0))
    gate_ref[...] = jnp.where(lane == 0, g1, jnp.where(lane == 1, g2, 0.0))


def out_proj_route(a, b, wa, wb, res, g, w_router, *, tm):
    t, d = res.shape
    ka, kb = a.shape[1], b.shape[1]
    row = lambda n: pl.BlockSpec((tm, n), lambda i: (i, 0))
    full = lambda r, c: pl.BlockSpec((r, c), lambda i: (0, 0))
    return pl.pallas_call(
        _out_proj_route_kernel, grid=(t // tm,),
        in_specs=[row(ka), row(kb), full(ka, d), full(kb, d), row(d), full(1, d), full(d, LANES)],
        out_specs=[row(d), row(d), row(LANES), row(LANES)],
        out_shape=[jax.ShapeDtypeStruct((t, d), F32), jax.ShapeDtypeStruct((t, d), F32),
                   jax.ShapeDtypeStruct((t, LANES), jnp.int32), jax.ShapeDtypeStruct((t, LANES), F32)],
        compiler_params=_cparams("parallel"), name="out_proj_route",
    )(a, b, wa, wb, res, g, w_router)


def _row_copy(src_hbm, src_row, dst_ref, dst_row, sem):
    return pltpu.make_async_copy(src_hbm.at[pl.ds(src_row, 1), :], dst_ref.at[pl.ds(dst_row, 1), :], sem)


def _gather_kernel(src_ref, x_hbm, o_ref, buf_ref, sem, *, tg):
    base = pl.program_id(0) * tg

    def issue(r, carry):
        _row_copy(x_hbm, src_ref[base + r], buf_ref, r, sem).start()
        return carry

    lax.fori_loop(0, tg, issue, 0)

    def wait(r, carry):
        _row_copy(x_hbm, src_ref[base + r], buf_ref, r, sem).wait()
        return carry

    lax.fori_loop(0, tg, wait, 0)
    o_ref[...] = buf_ref[...].astype(o_ref.dtype)


def gather_rows(src, x, *, tg):
    ns = src.shape[0]
    d = x.shape[1]
    kern = functools.partial(_gather_kernel, tg=tg)
    return pl.pallas_call(
        kern,
        grid_spec=pltpu.PrefetchScalarGridSpec(
            num_scalar_prefetch=1, grid=(ns // tg,),
            in_specs=[pl.BlockSpec(memory_space=pl.ANY)],
            out_specs=pl.BlockSpec((tg, d), lambda i, src: (i, 0)),
            scratch_shapes=[pltpu.VMEM((tg, d), F32), pltpu.SemaphoreType.DMA(())]),
        out_shape=jax.ShapeDtypeStruct((ns, d), BF16),
        compiler_params=_cparams("arbitrary"), name="moe_gather",
    )(src, x)


def _expert_kernel(te_ref, tv_ref, x_ref, w1_ref, w3_ref, w2_ref, o_ref):
    i = pl.program_id(0)
    f = pl.program_id(1)

    @pl.when(f == 0)
    def _():
        o_ref[...] = jnp.zeros_like(o_ref)

    @pl.when(tv_ref[i] > 0)
    def _():
        x = x_ref[...]
        a = jnp.dot(x, w1_ref[0], preferred_element_type=F32)
        b = jnp.dot(x, w3_ref[0], preferred_element_type=F32)
        hid = (a * jax.nn.sigmoid(a) * b).astype(BF16)
        o_ref[...] += jnp.dot(hid, w2_ref[0], preferred_element_type=F32)


def expert_ffn(tile_e, tile_v, xs, w1, w3, w2, *, tm, tf):
    ns, d = xs.shape
    dff = w1.shape[2]
    nf = dff // tf
    fsel = lambda i, f, te, tv: jnp.where(tv[i] > 0, f, nf - 1)
    return pl.pallas_call(
        _expert_kernel,
        grid_spec=pltpu.PrefetchScalarGridSpec(
            num_scalar_prefetch=2, grid=(ns // tm, nf),
            in_specs=[
                pl.BlockSpec((tm, d), lambda i, f, te, tv: (i, 0)),
                pl.BlockSpec((1, d, tf), lambda i, f, te, tv: (te[i], 0, fsel(i, f, te, tv))),
                pl.BlockSpec((1, d, tf), lambda i, f, te, tv: (te[i], 0, fsel(i, f, te, tv))),
                pl.BlockSpec((1, tf, d), lambda i, f, te, tv: (te[i], fsel(i, f, te, tv), 0)),
            ],
            out_specs=pl.BlockSpec((tm, d), lambda i, f, te, tv: (i, 0))),
        out_shape=jax.ShapeDtypeStruct((ns, d), F32),
        compiler_params=_cparams("parallel", "arbitrary"), name="expert_ffn",
    )(tile_e, tile_v, xs, w1, w3, w2)


def _combine_kernel(s1_ref, s2_ref, x_ref, gate_ref, g_ref, y_hbm, o_ref, y1_ref, y2_ref, sem1, sem2, *, tc):
    base = pl.program_id(0) * tc

    def issue(r, carry):
        _row_copy(y_hbm, s1_ref[base + r], y1_ref, r, sem1).start()
        _row_copy(y_hbm, s2_ref[base + r], y2_ref, r, sem2).start()
        return carry

    lax.fori_loop(0, tc, issue, 0)

    def wait(r, carry):
        _row_copy(y_hbm, s1_ref[base + r], y1_ref, r, sem1).wait()
        _row_copy(y_hbm, s2_ref[base + r], y2_ref, r, sem2).wait()
        return carry

    lax.fori_loop(0, tc, wait, 0)
    gates = gate_ref[...]
    x = x_ref[...] + gates[:, 0:1] * y1_ref[...] + gates[:, 1:2] * y2_ref[...]
    o_ref[...] = _rms(x, g_ref[...])


def combine(slot1, slot2, x, gates, g, ys, *, tc):
    t, d = x.shape
    kern = functools.partial(_combine_kernel, tc=tc)
    return pl.pallas_call(
        kern,
        grid_spec=pltpu.PrefetchScalarGridSpec(
            num_scalar_prefetch=2, grid=(t // tc,),
            in_specs=[
                pl.BlockSpec((tc, d), lambda i, s1, s2: (i, 0)),
                pl.BlockSpec((tc, LANES), lambda i, s1, s2: (i, 0)),
                pl.BlockSpec((1, d), lambda i, s1, s2: (0, 0)),
                pl.BlockSpec(memory_space=pl.ANY),
            ],
            out_specs=pl.BlockSpec((tc, d), lambda i, s1, s2: (i, 0)),
            scratch_shapes=[pltpu.VMEM((tc, d), F32), pltpu.VMEM((tc, d), F32),
                            pltpu.SemaphoreType.DMA(()), pltpu.SemaphoreType.DMA(())]),
        out_shape=jax.ShapeDtypeStruct((t, d), F32),
        compiler_params=_cparams("arbitrary"), name="moe_combine",
    )(slot1, slot2, x, gates, g, ys)


def route_slots(eid, *, tm):
    t = eid.shape[0]
    na = 2 * t
    e_flat = eid.reshape(na)
    onehot = (e_flat[:, None] == jnp.arange(N_EXPERTS, dtype=jnp.int32)[None, :]).astype(jnp.int32)
    csum = jnp.cumsum(onehot, axis=0)
    rank = jnp.sum((csum - onehot) * onehot, axis=1)
    counts = csum[-1]
    padded = (counts + tm - 1) // tm * tm
    ends = jnp.cumsum(padded)
    slot = jnp.sum(onehot * (ends - padded)[None, :], axis=1) + rank
    ns = na + N_EXPERTS * tm
    src = jnp.zeros((ns,), jnp.int32).at[slot].set(jnp.arange(na, dtype=jnp.int32) // 2)
    starts = jnp.arange(ns // tm, dtype=jnp.int32) * tm
    tile_e = jnp.sum((starts[:, None] >= ends[None, :]).astype(jnp.int32), axis=1)
    tile_v = (tile_e < N_EXPERTS).astype(jnp.int32)
    last_e = jnp.max(jnp.where(tile_v > 0, tile_e, 0))
    tile_e = jnp.where(tile_v > 0, tile_e, last_e)
    return src, tile_e, tile_v, slot.reshape(t, 2)


def kernel(x, e_norm_mix, e_w_in, e_conv_w, e_conv_b, e_ln_g, e_ln_b, e_qk_conv_w, e_qk_conv_b, e_b_i, e_b_f, e_w_out, e_norm_ffn, e_ffn_w1, e_ffn_w3, e_ffn_w2, o_norm_mix, o_w_in, o_lam_re, o_lam_im, o_log_dt, o_b_re, o_b_im, o_c_re, o_c_im, o_d_skip, o_w_glu, o_b_glu, o_w_out, o_norm_ffn, o_router, o_exp_w1, o_exp_w3, o_exp_w2, final_norm):
    batch, seq, d = x.shape
    t = batch * seq
    xt = x.reshape(t, d)
    row = lambda v: v.reshape(1, -1)
    assert e_norm_mix.shape[0] == 1 and o_norm_mix.shape[0] == 1, "one even and one odd layer"

    ch = e_conv_w.shape[2]
    mw = e_qk_conv_w.shape[2] // 2
    nh = MLSTM_HEADS
    main_cols = 2 * ch + 4 * mw
    w_in = e_w_in[0]
    w_gates = jnp.pad(w_in[:, main_cols:], ((0, 0), (0, LANES - 2 * nh))).astype(BF16)
    z0, gates0 = norm_matmul(xt, row(e_norm_mix[0]), w_in[:, :main_cols].astype(BF16), w_gates, tm=1024, tn=1024)
    out_a = conformer_conv(z0, e_conv_w[0], row(e_conv_b[0]), row(e_ln_g[0]), row(e_ln_b[0]),
                           batch=batch, seq=seq, ts=256)
    gate_bias = jnp.pad(jnp.concatenate([e_b_i[0], e_b_f[0]]), (0, LANES - 2 * nh)).reshape(1, LANES)
    out_b = mlstm(z0, gates0, e_qk_conv_w[0], row(e_qk_conv_b[0]), gate_bias, batch=batch, seq=seq, col0=2 * ch)
    w_out = e_w_out[0].astype(BF16)
    x1 = out_proj(out_a, out_b, w_out[:ch], w_out[ch:], xt, tm=512)
    x2 = ffn(x1, row(e_norm_ffn[0]), e_ffn_w1[0].astype(BF16), e_ffn_w3[0].astype(BF16),
             e_ffn_w2[0].astype(BF16), tm=512, tf=512)

    aw = o_w_in.shape[2] - o_d_skip.shape[1]
    aw //= 3
    sch = o_d_skip.shape[1]
    z1 = norm_matmul(x2, row(o_norm_mix[0]), o_w_in[0].astype(BF16), tm=1024, tn=1024)
    out_c = dilated_attention(z1, batch=batch, seq=seq, width=aw)
    sp = s5_params(o_lam_re[0], o_lam_im[0], o_log_dt[0], o_b_re[0], o_b_im[0], o_c_re[0], o_c_im[0])
    out_d = s5(z1, *sp, row(o_d_skip[0]), o_w_glu[0].astype(BF16), row(o_b_glu[0]),
               batch=batch, seq=seq, col_blk=3 * aw // sch)
    w_out1 = o_w_out[0].astype(BF16)
    w_router = jnp.pad(o_router[0], ((0, 0), (0, LANES - N_EXPERTS)))
    x3, xn3, eid, gates = out_proj_route(out_c, out_d, w_out1[:aw], w_out1[aw:], x2, row(o_norm_ffn[0]),
                                         w_router, tm=256)
    tm_e = 512
    src, tile_e, tile_v, slots = route_slots(eid[:, :2], tm=tm_e)
    xs = gather_rows(src, xn3, tg=tm_e)
    ys = expert_ffn(tile_e, tile_v, xs, o_exp_w1[0].astype(BF16), o_exp_w3[0].astype(BF16),
                    o_exp_w2[0].astype(BF16), tm=tm_e, tf=512)
    out = combine(slots[:, 0], slots[:, 1], x3, gates, row(final_norm), ys, tc=256)
    return out.reshape(batch, seq, d)
```

```python
import functools
import math

import jax
import jax.numpy as jnp
from jax import lax
from jax.experimental import pallas as pl
from jax.experimental.pallas import tpu as pltpu

F32 = jnp.float32
BF16 = jnp.bfloat16

RMS_EPS = 1e-6
LN_EPS = 1e-5
CONV_WIDTH = 31
MLSTM_HEADS = 8
MLSTM_QK_CONV = 4
MLSTM_CHUNK = 128
ATTN_HEADS = 8
ATTN_BLOCK = 128
DILATIONS = (1, 4, 16)
SSM_GROUP = 16
SSM_STATE = 64
N_EXPERTS = 8
LANES = 128
SUBLANES = 8
VMEM_LIMIT = 56 * 1024 * 1024


def _cparams(*sem):
    return pltpu.CompilerParams(dimension_semantics=sem, vmem_limit_bytes=VMEM_LIMIT)


def _rms(x, g):
    return x * lax.rsqrt(jnp.mean(x * x, axis=-1, keepdims=True) + RMS_EPS) * g


def _norm_matmul_kernel(x_ref, g_ref, w_ref, o_ref, xn_ref):
    @pl.when(pl.program_id(1) == 0)
    def _():
        xn_ref[...] = _rms(x_ref[...], g_ref[...]).astype(BF16)

    o_ref[...] = jnp.dot(xn_ref[...], w_ref[...], preferred_element_type=F32).astype(o_ref.dtype)


def _norm_matmul_aux_kernel(x_ref, g_ref, w_ref, wa_ref, o_ref, oa_ref, xn_ref):
    @pl.when(pl.program_id(1) == 0)
    def _():
        xn = _rms(x_ref[...], g_ref[...]).astype(BF16)
        xn_ref[...] = xn
        oa_ref[...] = jnp.dot(xn, wa_ref[...], preferred_element_type=F32)

    o_ref[...] = jnp.dot(xn_ref[...], w_ref[...], preferred_element_type=F32).astype(o_ref.dtype)


def norm_matmul(x, g, w, w_aux=None, *, tm, tn):
    t, d = x.shape
    n = w.shape[1]
    grid = (t // tm, n // tn)
    x_spec = pl.BlockSpec((tm, d), lambda i, j: (i, 0))
    g_spec = pl.BlockSpec((1, d), lambda i, j: (0, 0))
    w_spec = pl.BlockSpec((d, tn), lambda i, j: (0, j))
    o_spec = pl.BlockSpec((tm, tn), lambda i, j: (i, j))
    scratch = [pltpu.VMEM((tm, d), BF16)]
    if w_aux is None:
        return pl.pallas_call(
            _norm_matmul_kernel, grid=grid, in_specs=[x_spec, g_spec, w_spec], out_specs=o_spec,
            out_shape=jax.ShapeDtypeStruct((t, n), BF16), scratch_shapes=scratch,
            compiler_params=_cparams("parallel", "arbitrary"), name="norm_matmul")(x, g, w)
    na = w_aux.shape[1]
    return pl.pallas_call(
        _norm_matmul_aux_kernel, grid=grid,
        in_specs=[x_spec, g_spec, w_spec, pl.BlockSpec((d, na), lambda i, j: (0, 0))],
        out_specs=[o_spec, pl.BlockSpec((tm, na), lambda i, j: (i, 0))],
        out_shape=[jax.ShapeDtypeStruct((t, n), BF16), jax.ShapeDtypeStruct((t, na), F32)],
        scratch_shapes=scratch, compiler_params=_cparams("parallel", "arbitrary"),
        name="norm_matmul_aux")(x, g, w, w_aux)


CONV_HALO = 32
CONV_ROWS = 32
CONV_LANES = 256


def _conv_kernel(u_ref, halo_ref, w_ref, cb_ref, lg_ref, lb_ref, o_ref, buf_ref, xs_ref, acc_ref, *, ts, ch):
    first = pl.program_id(1) == 0

    def glu(u):
        u = u.astype(F32)
        return u[:, :ch] * jax.nn.sigmoid(u[:, ch:])

    buf_ref[0:CONV_HALO, :] = jnp.where(first, 0.0, glu(halo_ref[...]))
    buf_ref[CONV_HALO:CONV_HALO + ts, :] = glu(u_ref[...])
    n_shift = ts + CONV_HALO - SUBLANES
    for b in range(1, SUBLANES):
        xs_ref[b, 0:n_shift, :] = buf_ref[b:b + n_shift, :]

    lead = CONV_HALO - (CONV_WIDTH - 1)

    def conv_step(r, carry):
        base = pl.multiple_of(r * CONV_ROWS, CONV_ROWS)
        for lc in range(ch // CONV_LANES):
            cols = slice(lc * CONV_LANES, (lc + 1) * CONV_LANES)
            acc = jnp.broadcast_to(cb_ref[:, cols], (CONV_ROWS, CONV_LANES))
            for k in range(CONV_WIDTH):
                a, b = divmod(lead + k, SUBLANES)
                rows = pl.ds(base + SUBLANES * a, CONV_ROWS)
                xk = buf_ref[rows, cols] if b == 0 else xs_ref[b, rows, cols]
                acc = acc + w_ref[k:k + 1, cols] * xk
            acc_ref[pl.ds(base, CONV_ROWS), cols] = acc
        return carry

    lax.fori_loop(0, ts // CONV_ROWS, conv_step, 0)

    def norm_step(r, carry):
        base = pl.multiple_of(r * CONV_ROWS, CONV_ROWS)
        a = acc_ref[pl.ds(base, CONV_ROWS), :]
        mu = jnp.mean(a, axis=-1, keepdims=True)
        d = a - mu
        var = jnp.mean(d * d, axis=-1, keepdims=True)
        an = d * lax.rsqrt(var + LN_EPS) * lg_ref[...] + lb_ref[...]
        o_ref[pl.ds(base, CONV_ROWS), :] = (an * jax.nn.sigmoid(an)).astype(o_ref.dtype)
        return carry

    lax.fori_loop(0, ts // CONV_ROWS, norm_step, 0, unroll=4)


def conformer_conv(z, conv_w, conv_b, ln_g, ln_b, *, batch, seq, ts):
    ch = conv_w.shape[1]
    nts = seq // ts
    halo_blocks = ts // CONV_HALO
    kern = functools.partial(_conv_kernel, ts=ts, ch=ch)
    vec = lambda: pl.BlockSpec((1, ch), lambda b, i: (0, 0))
    return pl.pallas_call(
        kern, grid=(batch, nts),
        in_specs=[
            pl.BlockSpec((ts, 2 * ch), lambda b, i: (b * nts + i, 0)),
            pl.BlockSpec((CONV_HALO, 2 * ch),
                         lambda b, i: (jnp.maximum((b * nts + i) * halo_blocks - 1, 0), 0)),
            pl.BlockSpec((CONV_WIDTH, ch), lambda b, i: (0, 0)),
            vec(), vec(), vec(),
        ],
        out_specs=pl.BlockSpec((ts, ch), lambda b, i: (b * nts + i, 0)),
        out_shape=jax.ShapeDtypeStruct((batch * seq, ch), BF16),
        scratch_shapes=[
            pltpu.VMEM((CONV_HALO + ts, ch), F32),
            pltpu.VMEM((SUBLANES, CONV_HALO + ts, ch), F32),
            pltpu.VMEM((ts, ch), F32),
        ],
        compiler_params=_cparams("parallel", "arbitrary"), name="conformer_conv",
    )(z, z, conv_w, conv_b, ln_g, ln_b)


QK_HALO = 16


def _mlstm_kernel(zqk_ref, halo_ref, zv_ref, zo_ref, g_ref, cw_ref, cb_ref, gb_ref, o_ref,
                  qb_ref, c_ref, n_ref, m_ref, *, nh, dh):
    L = MLSTM_CHUNK
    first = pl.program_id(1) == 0

    @pl.when(first)
    def _():
        c_ref[...] = jnp.zeros_like(c_ref)
        n_ref[...] = jnp.zeros_like(n_ref)
        m_ref[...] = jnp.zeros_like(m_ref)

    qb_ref[0:QK_HALO, :] = jnp.where(first, 0.0, halo_ref[...].astype(F32))
    qb_ref[QK_HALO:QK_HALO + L, :] = zqk_ref[...].astype(F32)
    lead = QK_HALO - (MLSTM_QK_CONV - 1)
    acc = jnp.broadcast_to(cb_ref[...], (L, 2 * nh * dh))
    for k in range(MLSTM_QK_CONV):
        acc = acc + cw_ref[k:k + 1, :] * qb_ref[lead + k:lead + k + L, :]
    qk = acc * jax.nn.sigmoid(acc)

    g = g_ref[...] + gb_ref[...]
    logf = jax.nn.log_sigmoid(g)
    row = lax.broadcasted_iota(jnp.int32, (L, L), 0)
    col = lax.broadcasted_iota(jnp.int32, (L, L), 1)
    causal = col <= row
    tri = causal.astype(F32)
    bcum = jnp.dot(tri, logf, preferred_element_type=F32, precision=lax.Precision.HIGHEST)
    g_t = g.T
    b_t = bcum.T
    scale = dh ** -0.5

    for h in range(nh):
        q = qk[:, h * dh:(h + 1) * dh].astype(BF16)
        kf = qk[:, (nh + h) * dh:(nh + h + 1) * dh] * scale
        k = kf.astype(BF16)
        v = zv_ref[:, h * dh:(h + 1) * dh]
        b_col = bcum[:, nh + h:nh + h + 1]
        b_row = b_t[nh + h:nh + h + 1, :]
        i_col = g[:, h:h + 1]
        i_row = g_t[h:h + 1, :]
        m_prev = m_ref[h, 0:1, 0:1]
        c_prev = c_ref[h]
        n_prev = n_ref[h, 0:1, :]

        log_d = jnp.where(causal, b_col - b_row + i_row, -jnp.inf)
        inter = b_col + m_prev
        m_t = jnp.maximum(inter, jnp.max(log_d, axis=-1, keepdims=True))
        s = lax.dot_general(q, k, (((1,), (1,)), ((), ())), preferred_element_type=F32)
        s = s * jnp.exp(log_d - m_t)
        w_int = jnp.exp(inter - m_t)
        num = (jnp.dot(s.astype(BF16), v, preferred_element_type=F32)
               + w_int * jnp.dot(q, c_prev.astype(BF16), preferred_element_type=F32))
        qn = jnp.sum(q.astype(F32) * n_prev, axis=-1, keepdims=True)
        den = jnp.sum(s, axis=-1, keepdims=True) + w_int * qn
        hval = num / jnp.maximum(jnp.abs(den), jnp.exp(-m_t))

        b_last = b_col[L - 1:L, :]
        gk = b_last - b_col + i_col
        m_new = jnp.maximum(b_last + m_prev, jnp.max(gk, axis=0, keepdims=True))
        w_k = jnp.exp(gk - m_new)
        decay = jnp.exp(b_last + m_prev - m_new)
        kw = kf * w_k
        c_ref[h] = decay * c_prev + lax.dot_general(
            kw.astype(BF16), v, (((0,), (0,)), ((), ())), preferred_element_type=F32)
        n_ref[h] = jnp.broadcast_to(decay * n_prev + jnp.sum(kw, axis=0, keepdims=True), (SUBLANES, dh))
        m_ref[h] = jnp.broadcast_to(m_new, (SUBLANES, LANES))

        gate_o = jax.nn.sigmoid(zo_ref[:, h * dh:(h + 1) * dh].astype(F32))
        o_ref[:, h * dh:(h + 1) * dh] = (gate_o * hval).astype(o_ref.dtype)


def mlstm(z, gates, qk_conv_w, qk_conv_b, gate_bias, *, batch, seq, col0):
    L = MLSTM_CHUNK
    nh = MLSTM_HEADS
    w = qk_conv_w.shape[1] // 2
    dh = w // nh
    nc = seq // L
    qk_blk = col0 // (2 * w)
    v_blk = (col0 + 2 * w) // w
    o_blk = v_blk + 1
    kern = functools.partial(_mlstm_kernel, nh=nh, dh=dh)
    return pl.pallas_call(
        kern, grid=(batch, nc),
        in_specs=[
            pl.BlockSpec((L, 2 * w), lambda b, c: (b * nc + c, qk_blk)),
            pl.BlockSpec((QK_HALO, 2 * w),
                         lambda b, c: (jnp.maximum((b * nc + c) * (L // QK_HALO) - 1, 0), qk_blk)),
            pl.BlockSpec((L, w), lambda b, c: (b * nc + c, v_blk)),
            pl.BlockSpec((L, w), lambda b, c: (b * nc + c, o_blk)),
            pl.BlockSpec((L, LANES), lambda b, c: (b * nc + c, 0)),
            pl.BlockSpec((MLSTM_QK_CONV, 2 * w), lambda b, c: (0, 0)),
            pl.BlockSpec((1, 2 * w), lambda b, c: (0, 0)),
            pl.BlockSpec((1, LANES), lambda b, c: (0, 0)),
        ],
        out_specs=pl.BlockSpec((L, w), lambda b, c: (b * nc + c, 0)),
        out_shape=jax.ShapeDtypeStruct((batch * seq, w), BF16),
        scratch_shapes=[
            pltpu.VMEM((QK_HALO + L, 2 * w), F32),
            pltpu.VMEM((nh, dh, dh), F32),
            pltpu.VMEM((nh, SUBLANES, dh), F32),
            pltpu.VMEM((nh, SUBLANES, LANES), F32),
        ],
        compiler_params=_cparams("parallel", "arbitrary"), name="mlstm",
    )(z, z, z, z, gates, qk_conv_w, qk_conv_b, gate_bias)


def _out_proj_kernel(a_ref, b_ref, wa_ref, wb_ref, r_ref, o_ref):
    o_ref[...] = (r_ref[...]
                  + jnp.dot(a_ref[...], wa_ref[...], preferred_element_type=F32)
                  + jnp.dot(b_ref[...], wb_ref[...], preferred_element_type=F32))


def out_proj(a, b, wa, wb, res, *, tm):
    t, d = res.shape
    ka, kb = a.shape[1], b.shape[1]
    return pl.pallas_call(
        _out_proj_kernel, grid=(t // tm,),
        in_specs=[
            pl.BlockSpec((tm, ka), lambda i: (i, 0)),
            pl.BlockSpec((tm, kb), lambda i: (i, 0)),
            pl.BlockSpec((ka, d), lambda i: (0, 0)),
            pl.BlockSpec((kb, d), lambda i: (0, 0)),
            pl.BlockSpec((tm, d), lambda i: (i, 0)),
        ],
        out_specs=pl.BlockSpec((tm, d), lambda i: (i, 0)),
        out_shape=jax.ShapeDtypeStruct((t, d), F32),
        compiler_params=_cparams("parallel"), name="out_proj",
    )(a, b, wa, wb, res)


def _ffn_kernel(x_ref, g_ref, w1_ref, w3_ref, w2_ref, o_ref, xn_ref):
    f = pl.program_id(1)

    @pl.when(f == 0)
    def _():
        x = x_ref[...]
        xn_ref[...] = _rms(x, g_ref[...]).astype(BF16)
        o_ref[...] = x

    xn = xn_ref[...]
    a = jnp.dot(xn, w1_ref[...], preferred_element_type=F32)
    b = jnp.dot(xn, w3_ref[...], preferred_element_type=F32)
    hid = (a * jax.nn.sigmoid(a) * b).astype(BF16)
    o_ref[...] += jnp.dot(hid, w2_ref[...], preferred_element_type=F32)


def ffn(x, g, w1, w3, w2, *, tm, tf):
    t, d = x.shape
    dff = w1.shape[1]
    return pl.pallas_call(
        _ffn_kernel, grid=(t // tm, dff // tf),
        in_specs=[
            pl.BlockSpec((tm, d), lambda i, f: (i, 0)),
            pl.BlockSpec((1, d), lambda i, f: (0, 0)),
            pl.BlockSpec((d, tf), lambda i, f: (0, f)),
            pl.BlockSpec((d, tf), lambda i, f: (0, f)),
            pl.BlockSpec((tf, d), lambda i, f: (f, 0)),
        ],
        out_specs=pl.BlockSpec((tm, d), lambda i, f: (i, 0)),
        out_shape=jax.ShapeDtypeStruct((t, d), F32),
        scratch_shapes=[pltpu.VMEM((tm, d), BF16)],
        compiler_params=_cparams("parallel", "arbitrary"), name="ffn",
    )(x, g, w1, w3, w2)


def _attn_kernel(q_ref, k_ref, v_ref, o_ref, qf_ref, kf_ref, vf_ref, qd_ref, kd_ref, vd_ref,
                 od_ref, ld_ref, on_ref, ln_ref, *, seq, dh):
    T = ATTN_BLOCK
    nblk = seq // T
    qf_ref[...] = q_ref[...].astype(F32) * (dh ** -0.5)
    kf_ref[...] = k_ref[...].astype(F32)
    vf_ref[...] = v_ref[...].astype(F32)
    kd_ref[0:T, :] = jnp.zeros((T, dh), BF16)
    vd_ref[0:T, :] = jnp.zeros((T, dh), BF16)

    qi = lax.broadcasted_iota(jnp.int32, (T, 2 * T), 0)
    ki = lax.broadcasted_iota(jnp.int32, (T, 2 * T), 1)
    dist = T + qi - ki
    band = (dist >= 0) & (dist <= T)

    for g, dil in enumerate(DILATIONS):
        ls = seq // dil
        nb = ls // T
        for r in range(dil):
            rows = pl.ds(r, ls, stride=dil) if dil > 1 else pl.ds(0, ls)
            qd_ref[r * ls:(r + 1) * ls, :] = qf_ref[rows, :].astype(BF16)
            kd_ref[T + r * ls:T + (r + 1) * ls, :] = kf_ref[rows, :].astype(BF16)
            vd_ref[T + r * ls:T + (r + 1) * ls, :] = vf_ref[rows, :].astype(BF16)

        def block(n, carry):
            base = pl.multiple_of(n * T, T)
            qb = qd_ref[pl.ds(base, T), :]
            kb = kd_ref[pl.ds(base, 2 * T), :]
            vb = vd_ref[pl.ds(base, 2 * T), :]
            s = lax.dot_general(qb, kb, (((1,), (1,)), ((), ())), preferred_element_type=F32)
            kmin = jnp.where(n % nb == 0, T, 0)
            s = jnp.where(band & (ki >= kmin), s, -jnp.inf)
            m = jnp.max(s, axis=-1, keepdims=True)
            p = jnp.exp(s - m)
            l = jnp.sum(p, axis=-1, keepdims=True)
            o = jnp.dot(p.astype(BF16), vb, preferred_element_type=F32) / l
            od_ref[pl.ds(base, T), :] = o
            ld_ref[pl.ds(base, T), :] = jnp.broadcast_to(m + jnp.log(l), (T, dh))
            return carry

        lax.fori_loop(0, nblk, block, 0, unroll=8)

        for r in range(dil):
            rows = pl.ds(r, ls, stride=dil) if dil > 1 else pl.ds(0, ls)
            on_ref[g, rows, :] = od_ref[r * ls:(r + 1) * ls, :]
            ln_ref[g, rows, :] = ld_ref[r * ls:(r + 1) * ls, :]

    def merge(n, carry):
        rows = pl.ds(pl.multiple_of(n * T, T), T)
        lses = [ln_ref[g, rows, :] for g in range(len(DILATIONS))]
        mx = functools.reduce(jnp.maximum, lses)
        ws = [jnp.exp(l - mx) for l in lses]
        tot = functools.reduce(lambda a, b: a + b, ws)
        acc = ws[0] * on_ref[0, rows, :]
        for g in range(1, len(DILATIONS)):
            acc = acc + ws[g] * on_ref[g, rows, :]
        o_ref[rows, :] = (acc / tot).astype(o_ref.dtype)
        return carry

    lax.fori_loop(0, nblk, merge, 0)


def dilated_attention(z, *, batch, seq, width):
    nh = ATTN_HEADS
    dh = width // nh
    ng = len(DILATIONS)
    kern = functools.partial(_attn_kernel, seq=seq, dh=dh)
    blk = lambda off: pl.BlockSpec((seq, dh), lambda b, h: (b, off + h))
    return pl.pallas_call(
        kern, grid=(batch, nh),
        in_specs=[blk(0), blk(nh), blk(2 * nh)],
        out_specs=pl.BlockSpec((seq, dh), lambda b, h: (b, h)),
        out_shape=jax.ShapeDtypeStruct((batch * seq, width), BF16),
        scratch_shapes=[
            pltpu.VMEM((seq, dh), F32), pltpu.VMEM((seq, dh), F32), pltpu.VMEM((seq, dh), F32),
            pltpu.VMEM((seq, dh), BF16),
            pltpu.VMEM((ATTN_BLOCK + seq, dh), BF16), pltpu.VMEM((ATTN_BLOCK + seq, dh), BF16),
            pltpu.VMEM((seq, dh), F32), pltpu.VMEM((seq, dh), F32),
            pltpu.VMEM((ng, seq, dh), F32), pltpu.VMEM((ng, seq, dh), F32),
        ],
        compiler_params=_cparams("parallel", "parallel"), name="dilated_attention",
    )(z, z, z)


S5_SEGS = SUBLANES
S5_SEG = 64
S5_PITCH = S5_SEG + 4
S5_TILE = S5_SEGS * S5_SEG
S5_ROWS = S5_SEGS * S5_PITCH
S5_CHUNK_IN = 256
S5_SLABS = S5_CHUNK_IN // SSM_GROUP * SSM_STATE // LANES


def _s5_kernel(u_ref, bre_ref, bim_ref, cre_ref, cim_ref, lre_ref, lim_ref, dsk_ref, wg_ref, bg_ref,
               o_ref, ug_ref, sre_ref, sim_ref, cyre_ref, cyim_ref, y_ref, *, nchunk):
    first = pl.program_id(1) == 0

    @pl.when(first)
    def _():
        cyre_ref[...] = jnp.zeros_like(cyre_ref)
        cyim_ref[...] = jnp.zeros_like(cyim_ref)
        ug_ref[...] = jnp.zeros_like(ug_ref)

    for i in range(S5_SEGS):
        ug_ref[i * S5_PITCH:i * S5_PITCH + S5_SEG, :] = u_ref[i * S5_SEG:(i + 1) * S5_SEG, :].astype(F32)

    seg_id = lax.broadcasted_iota(jnp.int32, (S5_SEGS, LANES), 0)
    for c in range(nchunk):
        ucols = slice(c * S5_CHUNK_IN, (c + 1) * S5_CHUNK_IN)
        ub = ug_ref[:, ucols].astype(BF16)
        bu_re = jnp.dot(ub, bre_ref[c], preferred_element_type=F32)
        bu_im = jnp.dot(ub, bim_ref[c], preferred_element_type=F32)
        for s in range(S5_SLABS):
            sre_ref[s] = bu_re[:, s * LANES:(s + 1) * LANES]
            sim_ref[s] = bu_im[:, s * LANES:(s + 1) * LANES]

        lam_re = [jnp.broadcast_to(lre_ref[c, :, s * LANES:(s + 1) * LANES], (S5_SEGS, LANES))
                  for s in range(S5_SLABS)]
        lam_im = [jnp.broadcast_to(lim_ref[c, :, s * LANES:(s + 1) * LANES], (S5_SEGS, LANES))
                  for s in range(S5_SLABS)]

        def step(j, st, store):
            rows = pl.ds(j, S5_SEGS, stride=S5_PITCH)
            new = []
            for s in range(S5_SLABS):
                pr, pi = st[2 * s], st[2 * s + 1]
                nr = lam_re[s] * pr - lam_im[s] * pi + sre_ref[s, rows, :]
                ni = lam_re[s] * pi + lam_im[s] * pr + sim_ref[s, rows, :]
                if store:
                    sre_ref[s, rows, :] = nr
                    sim_ref[s, rows, :] = ni
                new += [nr, ni]
            return tuple(new)

        zero = tuple(jnp.zeros((S5_SEGS, LANES), F32) for _ in range(2 * S5_SLABS))
        ends = lax.fori_loop(0, S5_SEG, lambda j, st: step(j, st, False), zero)

        pw_re, pw_im = [l[0:1] for l in lam_re], [l[0:1] for l in lam_im]
        for _ in range(int(math.log2(S5_SEG))):
            pw_re, pw_im = ([a * a - b * b for a, b in zip(pw_re, pw_im)],
                            [2.0 * a * b for a, b in zip(pw_re, pw_im)])
        init = []
        for s in range(S5_SLABS):
            cols = slice(s * LANES, (s + 1) * LANES)
            cr, ci = cyre_ref[c, 0:1, cols], cyim_ref[c, 0:1, cols]
            in_re = jnp.zeros((S5_SEGS, LANES), F32)
            in_im = jnp.zeros((S5_SEGS, LANES), F32)
            for i in range(S5_SEGS):
                in_re = jnp.where(seg_id == i, cr, in_re)
                in_im = jnp.where(seg_id == i, ci, in_im)
                er, ei = ends[2 * s][i:i + 1], ends[2 * s + 1][i:i + 1]
                cr, ci = (pw_re[s] * cr - pw_im[s] * ci + er, pw_re[s] * ci + pw_im[s] * cr + ei)
            cyre_ref[c, :, cols] = jnp.broadcast_to(cr, (SUBLANES, LANES))
            cyim_ref[c, :, cols] = jnp.broadcast_to(ci, (SUBLANES, LANES))
            init += [in_re, in_im]

        lax.fori_loop(0, S5_SEG, lambda j, st: step(j, st, True), tuple(init))

        st_re = jnp.concatenate([sre_ref[s] for s in range(S5_SLABS)], axis=1).astype(BF16)
        st_im = jnp.concatenate([sim_ref[s] for s in range(S5_SLABS)], axis=1).astype(BF16)
        y_ref[:, ucols] = (jnp.dot(st_re, cre_ref[c], preferred_element_type=F32)
                           + jnp.dot(st_im, cim_ref[c], preferred_element_type=F32))

    for i in range(S5_SEGS):
        rows = slice(i * S5_PITCH, i * S5_PITCH + S5_SEG)
        y = y_ref[rows, :] + dsk_ref[...] * ug_ref[rows, :]
        y = jax.nn.gelu(y)
        gate = jnp.dot(y.astype(BF16), wg_ref[...], preferred_element_type=F32) + bg_ref[...]
        o_ref[i * S5_SEG:(i + 1) * S5_SEG, :] = (y * jax.nn.sigmoid(gate)).astype(o_ref.dtype)


def s5(z, b_re, b_im, c_re, c_im, lam_re, lam_im, d_skip, w_glu, b_glu, *, batch, seq, col_blk):
    nchunk, _, nstate = b_re.shape
    ch = nchunk * S5_CHUNK_IN
    nt = seq // S5_TILE
    kern = functools.partial(_s5_kernel, nchunk=nchunk)
    full = lambda shape: pl.BlockSpec(shape, lambda b, i: (0,) * len(shape))
    return pl.pallas_call(
        kern, grid=(batch, nt),
        in_specs=[
            pl.BlockSpec((S5_TILE, ch), lambda b, i: (b * nt + i, col_blk)),
            full(b_re.shape), full(b_im.shape), full(c_re.shape), full(c_im.shape),
            full(lam_re.shape), full(lam_im.shape), full((1, ch)), full((ch, ch)), full((1, ch)),
        ],
        out_specs=pl.BlockSpec((S5_TILE, ch), lambda b, i: (b * nt + i, 0)),
        out_shape=jax.ShapeDtypeStruct((batch * seq, ch), BF16),
        scratch_shapes=[
            pltpu.VMEM((S5_ROWS, ch), F32),
            pltpu.VMEM((S5_SLABS, S5_ROWS, LANES), F32),
            pltpu.VMEM((S5_SLABS, S5_ROWS, LANES), F32),
            pltpu.VMEM((nchunk, SUBLANES, nstate), F32),
            pltpu.VMEM((nchunk, SUBLANES, nstate), F32),
            pltpu.VMEM((S5_ROWS, ch), F32),
        ],
        compiler_params=_cparams("parallel", "arbitrary"), name="s5",
    )(z, b_re, b_im, c_re, c_im, lam_re, lam_im, d_skip, w_glu, b_glu)


def s5_params(lam_re, lam_im, log_dt, b_re, b_im, c_re, c_im):
    ng, ns = lam_re.shape
    gpc = S5_CHUNK_IN // SSM_GROUP
    nchunk = ng // gpc
    dt = jnp.exp(log_dt)[:, None]
    mag = jnp.exp(lam_re * dt)
    lb_re, lb_im = mag * jnp.cos(lam_im * dt), mag * jnp.sin(lam_im * dt)
    den = lam_re * lam_re + lam_im * lam_im
    f_re = ((lb_re - 1.0) * lam_re + lb_im * lam_im) / den
    f_im = (lb_im * lam_re - (lb_re - 1.0) * lam_im) / den
    bb_re = f_re[..., None] * b_re - f_im[..., None] * b_im
    bb_im = f_re[..., None] * b_im + f_im[..., None] * b_re
    eye = jnp.eye(gpc, dtype=F32)

    def pack_b(m):
        m = m.reshape(nchunk, gpc, ns, SSM_GROUP)
        return jnp.einsum('cgpk,gh->cgkhp', m, eye).reshape(nchunk, gpc * SSM_GROUP, gpc * ns).astype(BF16)

    def pack_c(m):
        m = m.reshape(nchunk, gpc, SSM_GROUP, ns)
        return jnp.einsum('cgkp,gh->cgphk', m, eye).reshape(nchunk, gpc * ns, gpc * SSM_GROUP).astype(BF16)

    return (pack_b(bb_re), pack_b(bb_im), pack_c(c_re), pack_c(-c_im),
            lb_re.reshape(nchunk, 1, gpc * ns), lb_im.reshape(nchunk, 1, gpc * ns))


def _pack_bf16_pair(lo, hi):
    lo_bits = lax.bitcast_convert_type(lo.astype(BF16).astype(F32), jnp.uint32) >> 16
    hi_bits = lax.bitcast_convert_type(hi.astype(BF16).astype(F32), jnp.uint32) & jnp.uint32(0xFFFF0000)
    return lo_bits | hi_bits


def _unpack_bf16_pair(w):
    lo = lax.bitcast_convert_type(w << 16, F32)
    hi = lax.bitcast_convert_type(w & jnp.uint32(0xFFFF0000), F32)
    return lo, hi


def _split3(v):
    hi = v.astype(BF16)
    r1 = v - hi.astype(F32)
    mid = r1.astype(BF16)
    lo = (r1 - mid.astype(F32)).astype(BF16)
    return hi, mid, lo


def _out_proj_route_kernel(a_ref, b_ref, wa_ref, wb_ref, r_ref, g_ref, wr_ref, x_ref, xp_ref, eid_ref, gate_ref):
    x = (r_ref[...]
         + jnp.dot(a_ref[...], wa_ref[...], preferred_element_type=F32)
         + jnp.dot(b_ref[...], wb_ref[...], preferred_element_type=F32))
    x_ref[...] = x
    xn = _rms(x, g_ref[...])
    half = xn.shape[1] // 2
    xp_ref[...] = _pack_bf16_pair(xn[:, :half], xn[:, half:])
    xh, xm, xl = _split3(xn)
    ph = jnp.dot(xh, wr_ref[...], preferred_element_type=F32)
    pm = jnp.dot(xm, wr_ref[:, :2 * LANES], preferred_element_type=F32)
    pl_ = jnp.dot(xl, wr_ref[:, :LANES], preferred_element_type=F32)
    logits = (ph[:, :LANES] + (ph[:, LANES:2 * LANES] + pm[:, :LANES])
              + (ph[:, 2 * LANES:] + pm[:, LANES:] + pl_))
    lane = lax.broadcasted_iota(jnp.int32, logits.shape, 1)
    logits = jnp.where(lane < N_EXPERTS, logits, -jnp.inf)
    v1 = jnp.max(logits, axis=-1, keepdims=True)
    i1 = jnp.min(jnp.where(logits == v1, lane, LANES), axis=-1, keepdims=True)
    rest = jnp.where(lane == i1, -jnp.inf, logits)
    v2 = jnp.max(rest, axis=-1, keepdims=True)
    i2 = jnp.min(jnp.where(rest == v2, lane, LANES), axis=-1, keepdims=True)
    e2 = jnp.exp(v2 - v1)
    g1 = 1.0 / (1.0 + e2)
    g2 = e2 / (1.0 + e2)
    eid_ref[...] = jnp.where(lane == 0, i1, jnp.where(lane == 1, i2, 0))
    gate_ref[...] = jnp.where(lane == 0, g1, jnp.where(lane == 1, g2, 0.0))


def out_proj_route(a, b, wa, wb, res, g, w_router, *, tm):
    t, d = res.shape
    ka, kb = a.shape[1], b.shape[1]
    row = lambda n: pl.BlockSpec((tm, n), lambda i: (i, 0))
    full = lambda r, c: pl.BlockSpec((r, c), lambda i: (0, 0))
    return pl.pallas_call(
        _out_proj_route_kernel, grid=(t // tm,),
        in_specs=[row(ka), row(kb), full(ka, d), full(kb, d), row(d), full(1, d), full(d, 3 * LANES)],
        out_specs=[row(d), row(d // 2), row(LANES), row(LANES)],
        out_shape=[jax.ShapeDtypeStruct((t, d), F32), jax.ShapeDtypeStruct((t, d // 2), jnp.uint32),
                   jax.ShapeDtypeStruct((t, LANES), jnp.int32), jax.ShapeDtypeStruct((t, LANES), F32)],
        compiler_params=_cparams("parallel"), name="out_proj_route",
    )(a, b, wa, wb, res, g, w_router)


EXPERT_TM = 512
EXPERT_TF = 512
COMBINE_TC = 256


def _gather_rows_per_step(tm, nf):
    return -(-tm // (nf * SUBLANES)) * SUBLANES


def _expert_kernel(te_ref, tv_ref, src_ref, dst_ref, xp_hbm, w1_ref, w3_ref, w2_ref, ys_hbm,
                   gbuf, xb, acc, obuf, gsem, ssem, *, tm, nf, gr):
    i = pl.program_id(0)
    f = pl.program_id(1)
    ntiles = pl.num_programs(0)
    slot = i % 2
    other = 1 - slot
    gb = gr * nf
    half = xb.shape[1] // 2

    def gather(tile, r, s):
        return pltpu.make_async_copy(xp_hbm.at[pl.ds(src_ref[tile * gb + r], 1), :],
                                     gbuf.at[s, pl.ds(r, 1), :], gsem.at[s])

    def scatter(tile, r, s):
        return pltpu.make_async_copy(obuf.at[s, pl.ds(r, 1), :],
                                     ys_hbm.at[pl.ds(dst_ref[(tile + 1) * gb + r], 1), :], ssem.at[s])

    def for_rows(n, fn):
        def body(r, carry):
            fn(r)
            return carry
        lax.fori_loop(0, n, body, 0, unroll=8)

    @pl.when((i == 0) & (f == 0))
    def _():
        obuf[...] = jnp.zeros_like(obuf)
        for_rows(gb, lambda r: gather(0, r, 0).start())

    @pl.when(f == 0)
    def _():
        for_rows(gb, lambda r: gather(i, r, slot).wait())

        @pl.when(i >= 1)
        def _():
            for_rows(gb, lambda r: scatter(i - 2, r, slot).wait())

        lo, hi = _unpack_bf16_pair(gbuf[slot, 0:tm, :])
        xb[:, :half] = lo.astype(BF16)
        xb[:, half:] = hi.astype(BF16)
        acc[...] = jnp.zeros_like(acc)

    def start_row_dmas():
        for r in range(gr):
            gather(i + 1, f * gr + r, other).start()
            scatter(i - 1, f * gr + r, other).start()

    @pl.when(tv_ref[i] > 0)
    def _():
        start_row_dmas()
        x = xb[...]
        a = jnp.dot(x, w1_ref[0], preferred_element_type=F32)
        b = jnp.dot(x, w3_ref[0], preferred_element_type=F32)
        hid = (a * jax.nn.sigmoid(a) * b).astype(BF16)
        acc[...] += jnp.dot(hid, w2_ref[0], preferred_element_type=F32)

    @pl.when(tv_ref[i] == 0)
    def _():
        start_row_dmas()

    @pl.when(f == nf - 1)
    def _():
        y = acc[...]
        obuf[slot, 0:tm, :] = _pack_bf16_pair(y[:, :half], y[:, half:])

    @pl.when((i == ntiles - 1) & (f == nf - 1))
    def _():
        for_rows(gb, lambda r: gather(i + 1, r, other).wait())
        for_rows(gb, lambda r: scatter(i - 1, r, other).wait())
        for_rows(gb, lambda r: scatter(i, r, slot).start())
        for_rows(gb, lambda r: scatter(i, r, slot).wait())


def expert_ffn(tile_e, tile_v, src, dst, xp, w1, w3, w2, *, n_out_rows):
    tm, tf = EXPERT_TM, EXPERT_TF
    dh = xp.shape[1]
    d = 2 * dh
    dff = w1.shape[2]
    nf = dff // tf
    gr = _gather_rows_per_step(tm, nf)
    ntiles = tile_e.shape[0]
    fsel = lambda i, f, te, tv: jnp.where(tv[i] > 0, f, nf - 1)
    kern = functools.partial(_expert_kernel, tm=tm, nf=nf, gr=gr)
    return pl.pallas_call(
        kern,
        grid_spec=pltpu.PrefetchScalarGridSpec(
            num_scalar_prefetch=4, grid=(ntiles, nf),
            in_specs=[
                pl.BlockSpec(memory_space=pl.ANY),
                pl.BlockSpec((1, d, tf), lambda i, f, te, tv, s, t: (te[i], 0, fsel(i, f, te, tv))),
                pl.BlockSpec((1, d, tf), lambda i, f, te, tv, s, t: (te[i], 0, fsel(i, f, te, tv))),
                pl.BlockSpec((1, tf, d), lambda i, f, te, tv, s, t: (te[i], fsel(i, f, te, tv), 0)),
            ],
            out_specs=pl.BlockSpec(memory_space=pl.ANY),
            scratch_shapes=[
                pltpu.VMEM((2, gr * nf, dh), jnp.uint32),
                pltpu.VMEM((tm, d), BF16),
                pltpu.VMEM((tm, d), F32),
                pltpu.VMEM((2, gr * nf, dh), jnp.uint32),
                pltpu.SemaphoreType.DMA((2,)),
                pltpu.SemaphoreType.DMA((2,)),
            ]),
        out_shape=jax.ShapeDtypeStruct((n_out_rows, dh), jnp.uint32),
        compiler_params=_cparams("arbitrary", "arbitrary"), name="expert_ffn",
    )(tile_e, tile_v, src, dst, xp, w1, w3, w2)


def _combine_kernel(x_ref, gate_ref, g_ref, y1_ref, y2_ref, o_ref):
    gates = gate_ref[...]
    half = x_ref.shape[1] // 2
    lo1, hi1 = _unpack_bf16_pair(y1_ref[...])
    lo2, hi2 = _unpack_bf16_pair(y2_ref[...])
    g1, g2 = gates[:, 0:1], gates[:, 1:2]
    x = x_ref[...]
    x = jnp.concatenate([x[:, :half] + g1 * lo1 + g2 * lo2, x[:, half:] + g1 * hi1 + g2 * hi2], axis=1)
    o_ref[...] = _rms(x, g_ref[...])


def combine(x, gates, g, ys):
    t, d = x.shape
    tc = COMBINE_TC
    return pl.pallas_call(
        _combine_kernel, grid=(t // tc,),
        in_specs=[
            pl.BlockSpec((tc, d), lambda i: (i, 0)),
            pl.BlockSpec((tc, LANES), lambda i: (i, 0)),
            pl.BlockSpec((1, d), lambda i: (0, 0)),
            pl.BlockSpec((tc, d // 2), lambda i: (i, 0)),
            pl.BlockSpec((tc, d // 2), lambda i: (t // tc + i, 0)),
        ],
        out_specs=pl.BlockSpec((tc, d), lambda i: (i, 0)),
        out_shape=jax.ShapeDtypeStruct((t, d), F32),
        compiler_params=_cparams("parallel"), name="moe_combine",
    )(x, gates, g, ys, ys)


def route_slots(eid, *, dff):
    tm = EXPERT_TM
    nf = dff // EXPERT_TF
    gb = _gather_rows_per_step(tm, nf) * nf
    t = eid.shape[0]
    na = 2 * t
    e_flat = eid.reshape(na)
    onehot = (e_flat[:, None] == jnp.arange(N_EXPERTS, dtype=jnp.int32)[None, :]).astype(jnp.int32)
    csum = jnp.cumsum(onehot, axis=0)
    rank = jnp.sum((csum - onehot) * onehot, axis=1)
    counts = csum[-1]
    padded = (counts + tm - 1) // tm * tm
    ends = jnp.cumsum(padded)
    slot = jnp.sum(onehot * (ends - padded)[None, :], axis=1) + rank
    ns = na + N_EXPERTS * tm
    ntiles = ns // tm
    asg = jnp.full((ns,), -1, jnp.int32).at[slot].set(jnp.arange(na, dtype=jnp.int32))
    is_pad = asg < 0
    tok = jnp.where(is_pad, 0, asg // 2)
    pad_rank = jnp.cumsum(is_pad.astype(jnp.int32)) - 1
    dst = jnp.where(is_pad, na + gb + pad_rank, (asg % 2) * t + asg // 2).reshape(ntiles, tm)
    over = (na + gb + (ns - na) + jnp.arange(ntiles * (gb - tm), dtype=jnp.int32)).reshape(ntiles, gb - tm)
    dst = jnp.concatenate([na + jnp.arange(gb, dtype=jnp.int32),
                           jnp.concatenate([dst, over], axis=1).reshape(-1)])
    src = jnp.pad(tok.reshape(ntiles, tm), ((0, 1), (0, gb - tm))).reshape(-1)
    starts = jnp.arange(ntiles, dtype=jnp.int32) * tm
    tile_e = jnp.sum((starts[:, None] >= ends[None, :]).astype(jnp.int32), axis=1)
    tile_v = (tile_e < N_EXPERTS).astype(jnp.int32)
    last_e = jnp.max(jnp.where(tile_v > 0, tile_e, 0))
    tile_e = jnp.where(tile_v > 0, tile_e, last_e)
    n_out_rows = na + gb + (ns - na) + ntiles * (gb - tm)
    n_out_rows = -(-n_out_rows // COMBINE_TC) * COMBINE_TC
    return tile_e, tile_v, src, dst, n_out_rows


def kernel(x, e_norm_mix, e_w_in, e_conv_w, e_conv_b, e_ln_g, e_ln_b, e_qk_conv_w, e_qk_conv_b, e_b_i, e_b_f, e_w_out, e_norm_ffn, e_ffn_w1, e_ffn_w3, e_ffn_w2, o_norm_mix, o_w_in, o_lam_re, o_lam_im, o_log_dt, o_b_re, o_b_im, o_c_re, o_c_im, o_d_skip, o_w_glu, o_b_glu, o_w_out, o_norm_ffn, o_router, o_exp_w1, o_exp_w3, o_exp_w2, final_norm):
    batch, seq, d = x.shape
    t = batch * seq
    xt = x.reshape(t, d)
    row = lambda v: v.reshape(1, -1)
    assert e_norm_mix.shape[0] == 1 and o_norm_mix.shape[0] == 1, "one even and one odd layer"

    ch = e_conv_w.shape[2]
    mw = e_qk_conv_w.shape[2] // 2
    nh = MLSTM_HEADS
    main_cols = 2 * ch + 4 * mw
    w_in = e_w_in[0]
    w_gates = jnp.pad(w_in[:, main_cols:], ((0, 0), (0, LANES - 2 * nh))).astype(BF16)
    z0, gates0 = norm_matmul(xt, row(e_norm_mix[0]), w_in[:, :main_cols].astype(BF16), w_gates, tm=1024, tn=1024)
    out_a = conformer_conv(z0, e_conv_w[0], row(e_conv_b[0]), row(e_ln_g[0]), row(e_ln_b[0]),
                           batch=batch, seq=seq, ts=256)
    gate_bias = jnp.pad(jnp.concatenate([e_b_i[0], e_b_f[0]]), (0, LANES - 2 * nh)).reshape(1, LANES)
    out_b = mlstm(z0, gates0, e_qk_conv_w[0], row(e_qk_conv_b[0]), gate_bias, batch=batch, seq=seq, col0=2 * ch)
    w_out = e_w_out[0].astype(BF16)
    x1 = out_proj(out_a, out_b, w_out[:ch], w_out[ch:], xt, tm=512)
    x2 = ffn(x1, row(e_norm_ffn[0]), e_ffn_w1[0].astype(BF16), e_ffn_w3[0].astype(BF16),
             e_ffn_w2[0].astype(BF16), tm=512, tf=512)

    aw = o_w_in.shape[2] - o_d_skip.shape[1]
    aw //= 3
    sch = o_d_skip.shape[1]
    z1 = norm_matmul(x2, row(o_norm_mix[0]), o_w_in[0].astype(BF16), tm=1024, tn=1024)
    out_c = dilated_attention(z1, batch=batch, seq=seq, width=aw)
    sp = s5_params(o_lam_re[0], o_lam_im[0], o_log_dt[0], o_b_re[0], o_b_im[0], o_c_re[0], o_c_im[0])
    out_d = s5(z1, *sp, row(o_d_skip[0]), o_w_glu[0].astype(BF16), row(o_b_glu[0]),
               batch=batch, seq=seq, col_blk=3 * aw // sch)
    w_out1 = o_w_out[0].astype(BF16)
    w_router = jnp.concatenate(
        [jnp.pad(p, ((0, 0), (0, LANES - N_EXPERTS))) for p in _split3(o_router[0])], axis=1)
    x3, xp3, eid, gates = out_proj_route(out_c, out_d, w_out1[:aw], w_out1[aw:], x2, row(o_norm_ffn[0]),
                                         w_router, tm=256)
    tile_e, tile_v, src, dst, n_out_rows = route_slots(eid[:, :2], dff=o_exp_w1.shape[3])
    ys = expert_ffn(tile_e, tile_v, src, dst, xp3, o_exp_w1[0].astype(BF16), o_exp_w3[0].astype(BF16),
                    o_exp_w2[0].astype(BF16), n_out_rows=n_out_rows)
    out = combine(x3, gates, row(final_norm), ys)
    return out.reshape(batch, seq, d)
```

```python
import functools
import math

import jax
import jax.numpy as jnp
from jax import lax
from jax.experimental import pallas as pl
from jax.experimental.pallas import tpu as pltpu

F32 = jnp.float32
BF16 = jnp.bfloat16

RMS_EPS = 1e-6
LN_EPS = 1e-5
CONV_WIDTH = 31
MLSTM_HEADS = 8
MLSTM_QK_CONV = 4
MLSTM_CHUNK = 128
ATTN_HEADS = 8
ATTN_BLOCK = 128
DILATIONS = (1, 4, 16)
SSM_GROUP = 16
SSM_STATE = 64
N_EXPERTS = 8
LANES = 128
SUBLANES = 8
VMEM_LIMIT = 56 * 1024 * 1024


def _cparams(*sem):
    return pltpu.CompilerParams(dimension_semantics=sem, vmem_limit_bytes=VMEM_LIMIT)


def _rms(x, g):
    return x * lax.rsqrt(jnp.mean(x * x, axis=-1, keepdims=True) + RMS_EPS) * g


def _norm_matmul_kernel(x_ref, g_ref, w_ref, o_ref, xn_ref):
    @pl.when(pl.program_id(1) == 0)
    def _():
        xn_ref[...] = _rms(x_ref[...], g_ref[...]).astype(BF16)

    o_ref[...] = jnp.dot(xn_ref[...], w_ref[...], preferred_element_type=F32).astype(o_ref.dtype)


def _norm_matmul_aux_kernel(x_ref, g_ref, w_ref, wa_ref, o_ref, oa_ref, xn_ref):
    @pl.when(pl.program_id(1) == 0)
    def _():
        xn = _rms(x_ref[...], g_ref[...]).astype(BF16)
        xn_ref[...] = xn
        oa_ref[...] = jnp.dot(xn, wa_ref[...], preferred_element_type=F32)

    o_ref[...] = jnp.dot(xn_ref[...], w_ref[...], preferred_element_type=F32).astype(o_ref.dtype)


def norm_matmul(x, g, w, w_aux=None, *, tm, tn):
    t, d = x.shape
    n = w.shape[1]
    grid = (t // tm, n // tn)
    x_spec = pl.BlockSpec((tm, d), lambda i, j: (i, 0))
    g_spec = pl.BlockSpec((1, d), lambda i, j: (0, 0))
    w_spec = pl.BlockSpec((d, tn), lambda i, j: (0, j))
    o_spec = pl.BlockSpec((tm, tn), lambda i, j: (i, j))
    scratch = [pltpu.VMEM((tm, d), BF16)]
    if w_aux is None:
        return pl.pallas_call(
            _norm_matmul_kernel, grid=grid, in_specs=[x_spec, g_spec, w_spec], out_specs=o_spec,
            out_shape=jax.ShapeDtypeStruct((t, n), BF16), scratch_shapes=scratch,
            compiler_params=_cparams("parallel", "arbitrary"), name="norm_matmul")(x, g, w)
    na = w_aux.shape[1]
    return pl.pallas_call(
        _norm_matmul_aux_kernel, grid=grid,
        in_specs=[x_spec, g_spec, w_spec, pl.BlockSpec((d, na), lambda i, j: (0, 0))],
        out_specs=[o_spec, pl.BlockSpec((tm, na), lambda i, j: (i, 0))],
        out_shape=[jax.ShapeDtypeStruct((t, n), BF16), jax.ShapeDtypeStruct((t, na), F32)],
        scratch_shapes=scratch, compiler_params=_cparams("parallel", "arbitrary"),
        name="norm_matmul_aux")(x, g, w, w_aux)


CONV_HALO = 32
CONV_ROWS = 32
CONV_LANES = 256


def _conv_kernel(u_ref, halo_ref, w_ref, cb_ref, lg_ref, lb_ref, o_ref, buf_ref, xs_ref, acc_ref, *, ts, ch):
    first = pl.program_id(1) == 0

    def glu(u):
        u = u.astype(F32)
        return u[:, :ch] * jax.nn.sigmoid(u[:, ch:])

    buf_ref[0:CONV_HALO, :] = jnp.where(first, 0.0, glu(halo_ref[...]))
    buf_ref[CONV_HALO:CONV_HALO + ts, :] = glu(u_ref[...])
    n_shift = ts + CONV_HALO - SUBLANES
    for b in range(1, SUBLANES):
        xs_ref[b, 0:n_shift, :] = buf_ref[b:b + n_shift, :]

    lead = CONV_HALO - (CONV_WIDTH - 1)

    def conv_step(r, carry):
        base = pl.multiple_of(r * CONV_ROWS, CONV_ROWS)
        for lc in range(ch // CONV_LANES):
            cols = slice(lc * CONV_LANES, (lc + 1) * CONV_LANES)
            acc = jnp.broadcast_to(cb_ref[:, cols], (CONV_ROWS, CONV_LANES))
            for k in range(CONV_WIDTH):
                a, b = divmod(lead + k, SUBLANES)
                rows = pl.ds(base + SUBLANES * a, CONV_ROWS)
                xk = buf_ref[rows, cols] if b == 0 else xs_ref[b, rows, cols]
                acc = acc + w_ref[k:k + 1, cols] * xk
            acc_ref[pl.ds(base, CONV_ROWS), cols] = acc
        return carry

    lax.fori_loop(0, ts // CONV_ROWS, conv_step, 0)

    def norm_step(r, carry):
        base = pl.multiple_of(r * CONV_ROWS, CONV_ROWS)
        a = acc_ref[pl.ds(base, CONV_ROWS), :]
        mu = jnp.mean(a, axis=-1, keepdims=True)
        d = a - mu
        var = jnp.mean(d * d, axis=-1, keepdims=True)
        an = d * lax.rsqrt(var + LN_EPS) * lg_ref[...] + lb_ref[...]
        o_ref[pl.ds(base, CONV_ROWS), :] = (an * jax.nn.sigmoid(an)).astype(o_ref.dtype)
        return carry

    lax.fori_loop(0, ts // CONV_ROWS, norm_step, 0, unroll=4)


def conformer_conv(z, conv_w, conv_b, ln_g, ln_b, *, batch, seq, ts):
    ch = conv_w.shape[1]
    nts = seq // ts
    halo_blocks = ts // CONV_HALO
    kern = functools.partial(_conv_kernel, ts=ts, ch=ch)
    vec = lambda: pl.BlockSpec((1, ch), lambda b, i: (0, 0))
    return pl.pallas_call(
        kern, grid=(batch, nts),
        in_specs=[
            pl.BlockSpec((ts, 2 * ch), lambda b, i: (b * nts + i, 0)),
            pl.BlockSpec((CONV_HALO, 2 * ch),
                         lambda b, i: (jnp.maximum((b * nts + i) * halo_blocks - 1, 0), 0)),
            pl.BlockSpec((CONV_WIDTH, ch), lambda b, i: (0, 0)),
            vec(), vec(), vec(),
        ],
        out_specs=pl.BlockSpec((ts, ch), lambda b, i: (b * nts + i, 0)),
        out_shape=jax.ShapeDtypeStruct((batch * seq, ch), BF16),
        scratch_shapes=[
            pltpu.VMEM((CONV_HALO + ts, ch), F32),
            pltpu.VMEM((SUBLANES, CONV_HALO + ts, ch), F32),
            pltpu.VMEM((ts, ch), F32),
        ],
        compiler_params=_cparams("parallel", "arbitrary"), name="conformer_conv",
    )(z, z, conv_w, conv_b, ln_g, ln_b)


QK_HALO = 16


def _mlstm_kernel(zqk_ref, halo_ref, zv_ref, zo_ref, g_ref, cw_ref, cb_ref, gb_ref, o_ref,
                  qb_ref, c_ref, n_ref, m_ref, *, nh, dh):
    L = MLSTM_CHUNK
    first = pl.program_id(1) == 0

    @pl.when(first)
    def _():
        c_ref[...] = jnp.zeros_like(c_ref)
        n_ref[...] = jnp.zeros_like(n_ref)
        m_ref[...] = jnp.zeros_like(m_ref)

    qb_ref[0:QK_HALO, :] = jnp.where(first, 0.0, halo_ref[...].astype(F32))
    qb_ref[QK_HALO:QK_HALO + L, :] = zqk_ref[...].astype(F32)
    lead = QK_HALO - (MLSTM_QK_CONV - 1)
    acc = jnp.broadcast_to(cb_ref[...], (L, 2 * nh * dh))
    for k in range(MLSTM_QK_CONV):
        acc = acc + cw_ref[k:k + 1, :] * qb_ref[lead + k:lead + k + L, :]
    qk = acc * jax.nn.sigmoid(acc)
    row = lax.broadcasted_iota(jnp.int32, (L, L), 0)
    col = lax.broadcasted_iota(jnp.int32, (L, L), 1)

    g = g_ref[...] + gb_ref[...]
    logf = jax.nn.log_sigmoid(g)
    causal = col <= row
    tri = causal.astype(F32)
    bcum = jnp.dot(tri, logf, preferred_element_type=F32, precision=lax.Precision.HIGHEST)
    g_t = g.T
    b_t = bcum.T
    scale = dh ** -0.5

    for h in range(nh):
        q = qk[:, h * dh:(h + 1) * dh].astype(BF16)
        kf = qk[:, (nh + h) * dh:(nh + h + 1) * dh] * scale
        k = kf.astype(BF16)
        v = zv_ref[:, h * dh:(h + 1) * dh]
        b_col = bcum[:, nh + h:nh + h + 1]
        b_row = b_t[nh + h:nh + h + 1, :]
        i_col = g[:, h:h + 1]
        i_row = g_t[h:h + 1, :]
        m_prev = m_ref[h, 0:1, 0:1]
        c_prev = c_ref[h]
        n_prev = n_ref[h, 0:1, :]

        log_d = jnp.where(causal, b_col - b_row + i_row, -jnp.inf)
        inter = b_col + m_prev
        m_t = jnp.maximum(inter, jnp.max(log_d, axis=-1, keepdims=True))
        s = lax.dot_general(q, k, (((1,), (1,)), ((), ())), preferred_element_type=F32)
        s = s * jnp.exp(log_d - m_t)
        w_int = jnp.exp(inter - m_t)
        num = (jnp.dot(s.astype(BF16), v, preferred_element_type=F32)
               + w_int * jnp.dot(q, c_prev.astype(BF16), preferred_element_type=F32))
        qn = jnp.sum(q.astype(F32) * n_prev, axis=-1, keepdims=True)
        den = jnp.sum(s, axis=-1, keepdims=True) + w_int * qn
        hval = num / jnp.maximum(jnp.abs(den), jnp.exp(-m_t))

        b_last = b_col[L - 1:L, :]
        gk = b_last - b_col + i_col
        m_new = jnp.maximum(b_last + m_prev, jnp.max(gk, axis=0, keepdims=True))
        w_k = jnp.exp(gk - m_new)
        decay = jnp.exp(b_last + m_prev - m_new)
        kw = kf * w_k
        c_ref[h] = decay * c_prev + lax.dot_general(
            kw.astype(BF16), v, (((0,), (0,)), ((), ())), preferred_element_type=F32)
        n_ref[h] = jnp.broadcast_to(decay * n_prev + jnp.sum(kw, axis=0, keepdims=True), (SUBLANES, dh))
        m_ref[h] = jnp.broadcast_to(m_new, (SUBLANES, LANES))

        gate_o = jax.nn.sigmoid(zo_ref[:, h * dh:(h + 1) * dh].astype(F32))
        o_ref[:, h * dh:(h + 1) * dh] = (gate_o * hval).astype(o_ref.dtype)


def mlstm(z, gates, qk_conv_w, qk_conv_b, gate_bias, *, batch, seq, col0):
    L = MLSTM_CHUNK
    nh = MLSTM_HEADS
    w = qk_conv_w.shape[1] // 2
    dh = w // nh
    nc = seq // L
    qk_blk = col0 // (2 * w)
    v_blk = (col0 + 2 * w) // w
    o_blk = v_blk + 1
    kern = functools.partial(_mlstm_kernel, nh=nh, dh=dh)
    return pl.pallas_call(
        kern, grid=(batch, nc),
        in_specs=[
            pl.BlockSpec((L, 2 * w), lambda b, c: (b * nc + c, qk_blk)),
            pl.BlockSpec((QK_HALO, 2 * w),
                         lambda b, c: (jnp.maximum((b * nc + c) * (L // QK_HALO) - 1, 0), qk_blk)),
            pl.BlockSpec((L, w), lambda b, c: (b * nc + c, v_blk)),
            pl.BlockSpec((L, w), lambda b, c: (b * nc + c, o_blk)),
            pl.BlockSpec((L, LANES), lambda b, c: (b * nc + c, 0)),
            pl.BlockSpec((MLSTM_QK_CONV, 2 * w), lambda b, c: (0, 0)),
            pl.BlockSpec((1, 2 * w), lambda b, c: (0, 0)),
            pl.BlockSpec((1, LANES), lambda b, c: (0, 0)),
        ],
        out_specs=pl.BlockSpec((L, w), lambda b, c: (b * nc + c, 0)),
        out_shape=jax.ShapeDtypeStruct((batch * seq, w), BF16),
        scratch_shapes=[
            pltpu.VMEM((QK_HALO + L, 2 * w), F32),
            pltpu.VMEM((nh, dh, dh), F32),
            pltpu.VMEM((nh, SUBLANES, dh), F32),
            pltpu.VMEM((nh, SUBLANES, LANES), F32),
        ],
        compiler_params=_cparams("parallel", "arbitrary"), name="mlstm",
    )(z, z, z, z, gates, qk_conv_w, qk_conv_b, gate_bias)


def _out_proj_kernel(a_ref, b_ref, wa_ref, wb_ref, r_ref, o_ref):
    o_ref[...] = (r_ref[...]
                  + jnp.dot(a_ref[...], wa_ref[...], preferred_element_type=F32)
                  + jnp.dot(b_ref[...], wb_ref[...], preferred_element_type=F32))


def out_proj(a, b, wa, wb, res, *, tm):
    t, d = res.shape
    ka, kb = a.shape[1], b.shape[1]
    return pl.pallas_call(
        _out_proj_kernel, grid=(t // tm,),
        in_specs=[
            pl.BlockSpec((tm, ka), lambda i: (i, 0)),
            pl.BlockSpec((tm, kb), lambda i: (i, 0)),
            pl.BlockSpec((ka, d), lambda i: (0, 0)),
            pl.BlockSpec((kb, d), lambda i: (0, 0)),
            pl.BlockSpec((tm, d), lambda i: (i, 0)),
        ],
        out_specs=pl.BlockSpec((tm, d), lambda i: (i, 0)),
        out_shape=jax.ShapeDtypeStruct((t, d), F32),
        compiler_params=_cparams("parallel"), name="out_proj",
    )(a, b, wa, wb, res)


def _ffn_kernel(x_ref, g_ref, w1_ref, w3_ref, w2_ref, o_ref, xn_ref):
    f = pl.program_id(1)

    @pl.when(f == 0)
    def _():
        x = x_ref[...]
        xn_ref[...] = _rms(x, g_ref[...]).astype(BF16)
        o_ref[...] = x

    xn = xn_ref[...]
    a = jnp.dot(xn, w1_ref[...], preferred_element_type=F32)
    b = jnp.dot(xn, w3_ref[...], preferred_element_type=F32)
    hid = (a * jax.nn.sigmoid(a) * b).astype(BF16)
    o_ref[...] += jnp.dot(hid, w2_ref[...], preferred_element_type=F32)


def ffn(x, g, w1, w3, w2, *, tm, tf):
    t, d = x.shape
    dff = w1.shape[1]
    return pl.pallas_call(
        _ffn_kernel, grid=(t // tm, dff // tf),
        in_specs=[
            pl.BlockSpec((tm, d), lambda i, f: (i, 0)),
            pl.BlockSpec((1, d), lambda i, f: (0, 0)),
            pl.BlockSpec((d, tf), lambda i, f: (0, f)),
            pl.BlockSpec((d, tf), lambda i, f: (0, f)),
            pl.BlockSpec((tf, d), lambda i, f: (f, 0)),
        ],
        out_specs=pl.BlockSpec((tm, d), lambda i, f: (i, 0)),
        out_shape=jax.ShapeDtypeStruct((t, d), F32),
        scratch_shapes=[pltpu.VMEM((tm, d), BF16)],
        compiler_params=_cparams("parallel", "arbitrary"), name="ffn",
    )(x, g, w1, w3, w2)


def _attn_kernel(q_ref, k_ref, v_ref, o_ref, qf_ref, kf_ref, vf_ref, qd_ref, kd_ref, vd_ref,
                 od_ref, ld_ref, on_ref, ln_ref, *, seq, dh):
    T = ATTN_BLOCK
    nblk = seq // T
    qf_ref[...] = q_ref[...].astype(F32) * (dh ** -0.5)
    kf_ref[...] = k_ref[...].astype(F32)
    vf_ref[...] = v_ref[...].astype(F32)
    kd_ref[0:T, :] = jnp.zeros((T, dh), BF16)
    vd_ref[0:T, :] = jnp.zeros((T, dh), BF16)

    qi = lax.broadcasted_iota(jnp.int32, (T, 2 * T), 0)
    ki = lax.broadcasted_iota(jnp.int32, (T, 2 * T), 1)
    dist = T + qi - ki
    band = (dist >= 0) & (dist <= T)

    for g, dil in enumerate(DILATIONS):
        ls = seq // dil
        nb = ls // T
        for r in range(dil):
            rows = pl.ds(r, ls, stride=dil) if dil > 1 else pl.ds(0, ls)
            qd_ref[r * ls:(r + 1) * ls, :] = qf_ref[rows, :].astype(BF16)
            kd_ref[T + r * ls:T + (r + 1) * ls, :] = kf_ref[rows, :].astype(BF16)
            vd_ref[T + r * ls:T + (r + 1) * ls, :] = vf_ref[rows, :].astype(BF16)

        def block(n, carry):
            base = pl.multiple_of(n * T, T)
            qb = qd_ref[pl.ds(base, T), :]
            kb = kd_ref[pl.ds(base, 2 * T), :]
            vb = vd_ref[pl.ds(base, 2 * T), :]
            s = lax.dot_general(qb, kb, (((1,), (1,)), ((), ())), preferred_element_type=F32)
            kmin = jnp.where(n % nb == 0, T, 0)
            s = jnp.where(band & (ki >= kmin), s, -jnp.inf)
            m = jnp.max(s, axis=-1, keepdims=True)
            p = jnp.exp(s - m)
            l = jnp.sum(p, axis=-1, keepdims=True)
            o = jnp.dot(p.astype(BF16), vb, preferred_element_type=F32) / l
            od_ref[pl.ds(base, T), :] = o
            ld_ref[pl.ds(base, T), :] = jnp.broadcast_to(m + jnp.log(l), (T, dh))
            return carry

        lax.fori_loop(0, nblk, block, 0, unroll=8)

        for r in range(dil):
            rows = pl.ds(r, ls, stride=dil) if dil > 1 else pl.ds(0, ls)
            on_ref[g, rows, :] = od_ref[r * ls:(r + 1) * ls, :]
            ln_ref[g, rows, :] = ld_ref[r * ls:(r + 1) * ls, :]

    def merge(n, carry):
        rows = pl.ds(pl.multiple_of(n * T, T), T)
        lses = [ln_ref[g, rows, :] for g in range(len(DILATIONS))]
        mx = functools.reduce(jnp.maximum, lses)
        ws = [jnp.exp(l - mx) for l in lses]
        tot = functools.reduce(lambda a, b: a + b, ws)
        acc = ws[0] * on_ref[0, rows, :]
        for g in range(1, len(DILATIONS)):
            acc = acc + ws[g] * on_ref[g, rows, :]
        o_ref[rows, :] = (acc / tot).astype(o_ref.dtype)
        return carry

    lax.fori_loop(0, nblk, merge, 0)


def dilated_attention(z, *, batch, seq, width):
    nh = ATTN_HEADS
    dh = width // nh
    ng = len(DILATIONS)
    kern = functools.partial(_attn_kernel, seq=seq, dh=dh)
    blk = lambda off: pl.BlockSpec((seq, dh), lambda b, h: (b, off + h))
    return pl.pallas_call(
        kern, grid=(batch, nh),
        in_specs=[blk(0), blk(nh), blk(2 * nh)],
        out_specs=pl.BlockSpec((seq, dh), lambda b, h: (b, h)),
        out_shape=jax.ShapeDtypeStruct((batch * seq, width), BF16),
        scratch_shapes=[
            pltpu.VMEM((seq, dh), F32), pltpu.VMEM((seq, dh), F32), pltpu.VMEM((seq, dh), F32),
            pltpu.VMEM((seq, dh), BF16),
            pltpu.VMEM((ATTN_BLOCK + seq, dh), BF16), pltpu.VMEM((ATTN_BLOCK + seq, dh), BF16),
            pltpu.VMEM((seq, dh), F32), pltpu.VMEM((seq, dh), F32),
            pltpu.VMEM((ng, seq, dh), F32), pltpu.VMEM((ng, seq, dh), F32),
        ],
        compiler_params=_cparams("parallel", "parallel"), name="dilated_attention",
    )(z, z, z)


S5_SEGS = SUBLANES
S5_SEG = 64
S5_PITCH = S5_SEG + 4
S5_TILE = S5_SEGS * S5_SEG
S5_ROWS = S5_SEGS * S5_PITCH
S5_CHUNK_IN = 256
S5_SLABS = S5_CHUNK_IN // SSM_GROUP * SSM_STATE // LANES


def _s5_kernel(u_ref, bre_ref, bim_ref, cre_ref, cim_ref, lre_ref, lim_ref, dsk_ref, wg_ref, bg_ref,
               o_ref, ug_ref, sre_ref, sim_ref, cyre_ref, cyim_ref, y_ref, *, nchunk):
    first = pl.program_id(1) == 0

    @pl.when(first)
    def _():
        cyre_ref[...] = jnp.zeros_like(cyre_ref)
        cyim_ref[...] = jnp.zeros_like(cyim_ref)
        ug_ref[...] = jnp.zeros_like(ug_ref)

    for i in range(S5_SEGS):
        ug_ref[i * S5_PITCH:i * S5_PITCH + S5_SEG, :] = u_ref[i * S5_SEG:(i + 1) * S5_SEG, :].astype(F32)

    seg_id = lax.broadcasted_iota(jnp.int32, (S5_SEGS, LANES), 0)
    for c in range(nchunk):
        ucols = slice(c * S5_CHUNK_IN, (c + 1) * S5_CHUNK_IN)
        ub = ug_ref[:, ucols].astype(BF16)
        bu_re = jnp.dot(ub, bre_ref[c], preferred_element_type=F32)
        bu_im = jnp.dot(ub, bim_ref[c], preferred_element_type=F32)
        for s in range(S5_SLABS):
            sre_ref[s] = bu_re[:, s * LANES:(s + 1) * LANES]
            sim_ref[s] = bu_im[:, s * LANES:(s + 1) * LANES]

        lam_re = [jnp.broadcast_to(lre_ref[c, :, s * LANES:(s + 1) * LANES], (S5_SEGS, LANES))
                  for s in range(S5_SLABS)]
        lam_im = [jnp.broadcast_to(lim_ref[c, :, s * LANES:(s + 1) * LANES], (S5_SEGS, LANES))
                  for s in range(S5_SLABS)]

        def step(j, st, store):
            rows = pl.ds(j, S5_SEGS, stride=S5_PITCH)
            new = []
            for s in range(S5_SLABS):
                pr, pi = st[2 * s], st[2 * s + 1]
                nr = lam_re[s] * pr - lam_im[s] * pi + sre_ref[s, rows, :]
                ni = lam_re[s] * pi + lam_im[s] * pr + sim_ref[s, rows, :]
                if store:
                    sre_ref[s, rows, :] = nr
                    sim_ref[s, rows, :] = ni
                new += [nr, ni]
            return tuple(new)

        zero = tuple(jnp.zeros((S5_SEGS, LANES), F32) for _ in range(2 * S5_SLABS))
        ends = lax.fori_loop(0, S5_SEG, lambda j, st: step(j, st, False), zero)

        pw_re, pw_im = [l[0:1] for l in lam_re], [l[0:1] for l in lam_im]
        for _ in range(int(math.log2(S5_SEG))):
            pw_re, pw_im = ([a * a - b * b for a, b in zip(pw_re, pw_im)],
                            [2.0 * a * b for a, b in zip(pw_re, pw_im)])
        init = []
        for s in range(S5_SLABS):
            cols = slice(s * LANES, (s + 1) * LANES)
            cr, ci = cyre_ref[c, 0:1, cols], cyim_ref[c, 0:1, cols]
            in_re = jnp.zeros((S5_SEGS, LANES), F32)
            in_im = jnp.zeros((S5_SEGS, LANES), F32)
            for i in range(S5_SEGS):
                in_re = jnp.where(seg_id == i, cr, in_re)
                in_im = jnp.where(seg_id == i, ci, in_im)
                er, ei = ends[2 * s][i:i + 1], ends[2 * s + 1][i:i + 1]
                cr, ci = (pw_re[s] * cr - pw_im[s] * ci + er, pw_re[s] * ci + pw_im[s] * cr + ei)
            cyre_ref[c, :, cols] = jnp.broadcast_to(cr, (SUBLANES, LANES))
            cyim_ref[c, :, cols] = jnp.broadcast_to(ci, (SUBLANES, LANES))
            init += [in_re, in_im]

        lax.fori_loop(0, S5_SEG, lambda j, st: step(j, st, True), tuple(init))

        st_re = jnp.concatenate([sre_ref[s] for s in range(S5_SLABS)], axis=1).astype(BF16)
        st_im = jnp.concatenate([sim_ref[s] for s in range(S5_SLABS)], axis=1).astype(BF16)
        y_ref[:, ucols] = (jnp.dot(st_re, cre_ref[c], preferred_element_type=F32)
                           + jnp.dot(st_im, cim_ref[c], preferred_element_type=F32))

    for i in range(S5_SEGS):
        rows = slice(i * S5_PITCH, i * S5_PITCH + S5_SEG)
        y = y_ref[rows, :] + dsk_ref[...] * ug_ref[rows, :]
        y = jax.nn.gelu(y)
        gate = jnp.dot(y.astype(BF16), wg_ref[...], preferred_element_type=F32) + bg_ref[...]
        o_ref[i * S5_SEG:(i + 1) * S5_SEG, :] = (y * jax.nn.sigmoid(gate)).astype(o_ref.dtype)


def s5(z, b_re, b_im, c_re, c_im, lam_re, lam_im, d_skip, w_glu, b_glu, *, batch, seq, col_blk):
    nchunk, _, nstate = b_re.shape
    ch = nchunk * S5_CHUNK_IN
    nt = seq // S5_TILE
    kern = functools.partial(_s5_kernel, nchunk=nchunk)
    full = lambda shape: pl.BlockSpec(shape, lambda b, i: (0,) * len(shape))
    return pl.pallas_call(
        kern, grid=(batch, nt),
        in_specs=[
            pl.BlockSpec((S5_TILE, ch), lambda b, i: (b * nt + i, col_blk)),
            full(b_re.shape), full(b_im.shape), full(c_re.shape), full(c_im.shape),
            full(lam_re.shape), full(lam_im.shape), full((1, ch)), full((ch, ch)), full((1, ch)),
        ],
        out_specs=pl.BlockSpec((S5_TILE, ch), lambda b, i: (b * nt + i, 0)),
        out_shape=jax.ShapeDtypeStruct((batch * seq, ch), BF16),
        scratch_shapes=[
            pltpu.VMEM((S5_ROWS, ch), F32),
            pltpu.VMEM((S5_SLABS, S5_ROWS, LANES), F32),
            pltpu.VMEM((S5_SLABS, S5_ROWS, LANES), F32),
            pltpu.VMEM((nchunk, SUBLANES, nstate), F32),
            pltpu.VMEM((nchunk, SUBLANES, nstate), F32),
            pltpu.VMEM((S5_ROWS, ch), F32),
        ],
        compiler_params=_cparams("parallel", "arbitrary"), name="s5",
    )(z, b_re, b_im, c_re, c_im, lam_re, lam_im, d_skip, w_glu, b_glu)


def s5_params(lam_re, lam_im, log_dt, b_re, b_im, c_re, c_im):
    ng, ns = lam_re.shape
    gpc = S5_CHUNK_IN // SSM_GROUP
    nchunk = ng // gpc
    dt = jnp.exp(log_dt)[:, None]
    mag = jnp.exp(lam_re * dt)
    lb_re, lb_im = mag * jnp.cos(lam_im * dt), mag * jnp.sin(lam_im * dt)
    den = lam_re * lam_re + lam_im * lam_im
    f_re = ((lb_re - 1.0) * lam_re + lb_im * lam_im) / den
    f_im = (lb_im * lam_re - (lb_re - 1.0) * lam_im) / den
    bb_re = f_re[..., None] * b_re - f_im[..., None] * b_im
    bb_im = f_re[..., None] * b_im + f_im[..., None] * b_re
    eye = jnp.eye(gpc, dtype=F32)

    def pack_b(m):
        m = m.reshape(nchunk, gpc, ns, SSM_GROUP)
        return jnp.einsum('cgpk,gh->cgkhp', m, eye).reshape(nchunk, gpc * SSM_GROUP, gpc * ns).astype(BF16)

    def pack_c(m):
        m = m.reshape(nchunk, gpc, SSM_GROUP, ns)
        return jnp.einsum('cgkp,gh->cgphk', m, eye).reshape(nchunk, gpc * ns, gpc * SSM_GROUP).astype(BF16)

    return (pack_b(bb_re), pack_b(bb_im), pack_c(c_re), pack_c(-c_im),
            lb_re.reshape(nchunk, 1, gpc * ns), lb_im.reshape(nchunk, 1, gpc * ns))


def _pack_bf16_pair(lo, hi):
    lo_bits = lax.bitcast_convert_type(lo.astype(BF16).astype(F32), jnp.uint32) >> 16
    hi_bits = lax.bitcast_convert_type(hi.astype(BF16).astype(F32), jnp.uint32) & jnp.uint32(0xFFFF0000)
    return lo_bits | hi_bits


def _unpack_bf16_pair(w):
    lo = lax.bitcast_convert_type(w << 16, F32)
    hi = lax.bitcast_convert_type(w & jnp.uint32(0xFFFF0000), F32)
    return lo, hi


def _split3(v):
    hi = v.astype(BF16)
    r1 = v - hi.astype(F32)
    mid = r1.astype(BF16)
    lo = (r1 - mid.astype(F32)).astype(BF16)
    return hi, mid, lo


def _out_proj_route_kernel(a_ref, b_ref, wa_ref, wb_ref, r_ref, g_ref, wr_ref, x_ref, xp_ref, eid_ref, gate_ref):
    x = (r_ref[...]
         + jnp.dot(a_ref[...], wa_ref[...], preferred_element_type=F32)
         + jnp.dot(b_ref[...], wb_ref[...], preferred_element_type=F32))
    x_ref[...] = x
    xn = _rms(x, g_ref[...])
    half = xn.shape[1] // 2
    xp_ref[...] = _pack_bf16_pair(xn[:, :half], xn[:, half:])
    xh, xm, xl = _split3(xn)
    ph = jnp.dot(xh, wr_ref[...], preferred_element_type=F32)
    pm = jnp.dot(xm, wr_ref[:, :2 * LANES], preferred_element_type=F32)
    pl_ = jnp.dot(xl, wr_ref[:, :LANES], preferred_element_type=F32)
    logits = (ph[:, :LANES] + (ph[:, LANES:2 * LANES] + pm[:, :LANES])
              + (ph[:, 2 * LANES:] + pm[:, LANES:] + pl_))
    lane = lax.broadcasted_iota(jnp.int32, logits.shape, 1)
    logits = jnp.where(lane < N_EXPERTS, logits, -jnp.inf)
    v1 = jnp.max(logits, axis=-1, keepdims=True)
    i1 = jnp.min(jnp.where(logits == v1, lane, LANES), axis=-1, keepdims=True)
    rest = jnp.where(lane == i1, -jnp.inf, logits)
    v2 = jnp.max(rest, axis=-1, keepdims=True)
    i2 = jnp.min(jnp.where(rest == v2, lane, LANES), axis=-1, keepdims=True)
    e2 = jnp.exp(v2 - v1)
    g1 = 1.0 / (1.0 + e2)
    g2 = e2 / (1.0 + e2)
    eid_ref[...] = jnp.where(lane == 0, i1, jnp.where(lane == 1, i2, 0))
    gate_ref[...] = jnp.where(lane == 0, g1, jnp.where(lane == 1, g2, 0.0))


def out_proj_route(a, b, wa, wb, res, g, w_router, *, tm):
    t, d = res.shape
    ka, kb = a.shape[1], b.shape[1]
    row = lambda n: pl.BlockSpec((tm, n), lambda i: (i, 0))
    full = lambda r, c: pl.BlockSpec((r, c), lambda i: (0, 0))
    return pl.pallas_call(
        _out_proj_route_kernel, grid=(t // tm,),
        in_specs=[row(ka), row(kb), full(ka, d), full(kb, d), row(d), full(1, d), full(d, 3 * LANES)],
        out_specs=[row(d), row(d // 2), row(LANES), row(LANES)],
        out_shape=[jax.ShapeDtypeStruct((t, d), F32), jax.ShapeDtypeStruct((t, d // 2), jnp.uint32),
                   jax.ShapeDtypeStruct((t, LANES), jnp.int32), jax.ShapeDtypeStruct((t, LANES), F32)],
        compiler_params=_cparams("parallel"), name="out_proj_route",
    )(a, b, wa, wb, res, g, w_router)


EXPERT_TM = 1024
EXPERT_TF = 256
COMBINE_TC = 256


def _gather_rows_per_step(tm, nf):
    return -(-tm // (nf * SUBLANES)) * SUBLANES


def _expert_kernel(te_ref, tv_ref, src_ref, dst_ref, xp_hbm, w1_ref, w3_ref, w2_ref, ys_hbm,
                   gbuf, xb, acc, obuf, gsem, ssem, *, tm, nf, gr):
    i = pl.program_id(0)
    f = pl.program_id(1)
    ntiles = pl.num_programs(0)
    slot = i % 2
    other = 1 - slot
    gb = gr * nf
    half = xb.shape[1] // 2

    def gather(tile, r, s):
        return pltpu.make_async_copy(xp_hbm.at[pl.ds(src_ref[tile * gb + r], 1), :],
                                     gbuf.at[s, pl.ds(r, 1), :], gsem.at[s])

    def scatter(tile, r, s):
        return pltpu.make_async_copy(obuf.at[s, pl.ds(r, 1), :],
                                     ys_hbm.at[pl.ds(dst_ref[(tile + 1) * gb + r], 1), :], ssem.at[s])

    def for_rows(n, fn):
        def body(r, carry):
            fn(r)
            return carry
        lax.fori_loop(0, n, body, 0, unroll=8)

    @pl.when((i == 0) & (f == 0))
    def _():
        obuf[...] = jnp.zeros_like(obuf)
        for_rows(gb, lambda r: gather(0, r, 0).start())

    @pl.when(f == 0)
    def _():
        for_rows(gb, lambda r: gather(i, r, slot).wait())

        @pl.when(i >= 1)
        def _():
            for_rows(gb, lambda r: scatter(i - 2, r, slot).wait())

        lo, hi = _unpack_bf16_pair(gbuf[slot, 0:tm, :])
        xb[:, :half] = lo.astype(BF16)
        xb[:, half:] = hi.astype(BF16)
        acc[...] = jnp.zeros_like(acc)

    def start_row_dmas():
        for r in range(gr):
            gather(i + 1, f * gr + r, other).start()
            scatter(i - 1, f * gr + r, other).start()

    def swiglu_rows(rows):
        x = xb[0:rows, :]
        a = jnp.dot(x, w1_ref[0].astype(BF16), preferred_element_type=F32)
        b = jnp.dot(x, w3_ref[0].astype(BF16), preferred_element_type=F32)
        hid = (a * jax.nn.sigmoid(a) * b).astype(BF16)
        acc[0:rows, :] += jnp.dot(hid, w2_ref[0].astype(BF16), preferred_element_type=F32)

    used = tv_ref[i]

    @pl.when(used > tm // 2)
    def _():
        start_row_dmas()
        swiglu_rows(tm)

    @pl.when((used > 0) & (used <= tm // 2))
    def _():
        start_row_dmas()
        swiglu_rows(tm // 2)

    @pl.when(used == 0)
    def _():
        start_row_dmas()

    @pl.when(f == nf - 1)
    def _():
        y = acc[...]
        obuf[slot, 0:tm, :] = _pack_bf16_pair(y[:, :half], y[:, half:])

    @pl.when((i == ntiles - 1) & (f == nf - 1))
    def _():
        for_rows(gb, lambda r: gather(i + 1, r, other).wait())
        for_rows(gb, lambda r: scatter(i - 1, r, other).wait())
        for_rows(gb, lambda r: scatter(i, r, slot).start())
        for_rows(gb, lambda r: scatter(i, r, slot).wait())


def expert_ffn(tile_e, tile_v, src, dst, xp, w1, w3, w2, *, n_out_rows):
    tm, tf = EXPERT_TM, EXPERT_TF
    dh = xp.shape[1]
    d = 2 * dh
    dff = w1.shape[2]
    nf = dff // tf
    gr = _gather_rows_per_step(tm, nf)
    ntiles = tile_e.shape[0]
    fsel = lambda i, f, te, tv: jnp.where(tv[i] > 0, f, nf - 1)
    kern = functools.partial(_expert_kernel, tm=tm, nf=nf, gr=gr)
    return pl.pallas_call(
        kern,
        grid_spec=pltpu.PrefetchScalarGridSpec(
            num_scalar_prefetch=4, grid=(ntiles, nf),
            in_specs=[
                pl.BlockSpec(memory_space=pl.ANY),
                pl.BlockSpec((1, d, tf), lambda i, f, te, tv, s, t: (te[i], 0, fsel(i, f, te, tv))),
                pl.BlockSpec((1, d, tf), lambda i, f, te, tv, s, t: (te[i], 0, fsel(i, f, te, tv))),
                pl.BlockSpec((1, tf, d), lambda i, f, te, tv, s, t: (te[i], fsel(i, f, te, tv), 0)),
            ],
            out_specs=pl.BlockSpec(memory_space=pl.ANY),
            scratch_shapes=[
                pltpu.VMEM((2, gr * nf, dh), jnp.uint32),
                pltpu.VMEM((tm, d), BF16),
                pltpu.VMEM((tm, d), F32),
                pltpu.VMEM((2, gr * nf, dh), jnp.uint32),
                pltpu.SemaphoreType.DMA((2,)),
                pltpu.SemaphoreType.DMA((2,)),
            ]),
        out_shape=jax.ShapeDtypeStruct((n_out_rows, dh), jnp.uint32),
        compiler_params=_cparams("arbitrary", "arbitrary"), name="expert_ffn",
    )(tile_e, tile_v, src, dst, xp, w1, w3, w2)


def _combine_kernel(x_ref, gate_ref, g_ref, y1_ref, y2_ref, o_ref):
    gates = gate_ref[...]
    half = x_ref.shape[1] // 2
    lo1, hi1 = _unpack_bf16_pair(y1_ref[...])
    lo2, hi2 = _unpack_bf16_pair(y2_ref[...])
    g1, g2 = gates[:, 0:1], gates[:, 1:2]
    x = x_ref[...]
    x = jnp.concatenate([x[:, :half] + g1 * lo1 + g2 * lo2, x[:, half:] + g1 * hi1 + g2 * hi2], axis=1)
    o_ref[...] = _rms(x, g_ref[...])


def combine(x, gates, g, ys):
    t, d = x.shape
    tc = COMBINE_TC
    return pl.pallas_call(
        _combine_kernel, grid=(t // tc,),
        in_specs=[
            pl.BlockSpec((tc, d), lambda i: (i, 0)),
            pl.BlockSpec((tc, LANES), lambda i: (i, 0)),
            pl.BlockSpec((1, d), lambda i: (0, 0)),
            pl.BlockSpec((tc, d // 2), lambda i: (i, 0)),
            pl.BlockSpec((tc, d // 2), lambda i: (t // tc + i, 0)),
        ],
        out_specs=pl.BlockSpec((tc, d), lambda i: (i, 0)),
        out_shape=jax.ShapeDtypeStruct((t, d), F32),
        compiler_params=_cparams("parallel"), name="moe_combine",
    )(x, gates, g, ys, ys)


def route_slots(eid, *, dff):
    tm = EXPERT_TM
    nf = dff // EXPERT_TF
    gb = _gather_rows_per_step(tm, nf) * nf
    t = eid.shape[0]
    na = 2 * t
    e_flat = eid.reshape(na)
    onehot = (e_flat[:, None] == jnp.arange(N_EXPERTS, dtype=jnp.int32)[None, :]).astype(jnp.int32)
    csum = jnp.cumsum(onehot, axis=0)
    rank = jnp.sum((csum - onehot) * onehot, axis=1)
    counts = csum[-1]
    padded = (counts + tm - 1) // tm * tm
    ends = jnp.cumsum(padded)
    slot = jnp.sum(onehot * (ends - padded)[None, :], axis=1) + rank
    ns = na + N_EXPERTS * tm
    ntiles = ns // tm
    asg = jnp.full((ns,), -1, jnp.int32).at[slot].set(jnp.arange(na, dtype=jnp.int32))
    is_pad = asg < 0
    tok = jnp.where(is_pad, 0, asg // 2)
    pad_rank = jnp.cumsum(is_pad.astype(jnp.int32)) - 1
    dst = jnp.where(is_pad, na + gb + pad_rank, (asg % 2) * t + asg // 2).reshape(ntiles, tm)
    over = (na + gb + (ns - na) + jnp.arange(ntiles * (gb - tm), dtype=jnp.int32)).reshape(ntiles, gb - tm)
    dst = jnp.concatenate([na + jnp.arange(gb, dtype=jnp.int32),
                           jnp.concatenate([dst, over], axis=1).reshape(-1)])
    src = jnp.pad(tok.reshape(ntiles, tm), ((0, 1), (0, gb - tm))).reshape(-1)
    starts = jnp.arange(ntiles, dtype=jnp.int32) * tm
    tile_e = jnp.sum((starts[:, None] >= ends[None, :]).astype(jnp.int32), axis=1)
    used_end = jnp.concatenate([ends - padded + counts, jnp.zeros((1,), jnp.int32)])
    tile_v = jnp.clip(used_end[tile_e] - starts, 0, tm)
    last_e = jnp.max(jnp.where(tile_v > 0, tile_e, 0))
    tile_e = jnp.where(tile_v > 0, tile_e, last_e)
    n_out_rows = na + gb + (ns - na) + ntiles * (gb - tm)
    n_out_rows = -(-n_out_rows // COMBINE_TC) * COMBINE_TC
    return tile_e, tile_v, src, dst, n_out_rows


def kernel(x, e_norm_mix, e_w_in, e_conv_w, e_conv_b, e_ln_g, e_ln_b, e_qk_conv_w, e_qk_conv_b, e_b_i, e_b_f, e_w_out, e_norm_ffn, e_ffn_w1, e_ffn_w3, e_ffn_w2, o_norm_mix, o_w_in, o_lam_re, o_lam_im, o_log_dt, o_b_re, o_b_im, o_c_re, o_c_im, o_d_skip, o_w_glu, o_b_glu, o_w_out, o_norm_ffn, o_router, o_exp_w1, o_exp_w3, o_exp_w2, final_norm):
    batch, seq, d = x.shape
    t = batch * seq
    xt = x.reshape(t, d)
    row = lambda v: v.reshape(1, -1)
    assert e_norm_mix.shape[0] == 1 and o_norm_mix.shape[0] == 1, "one even and one odd layer"

    ch = e_conv_w.shape[2]
    mw = e_qk_conv_w.shape[2] // 2
    nh = MLSTM_HEADS
    main_cols = 2 * ch + 4 * mw
    w_in = e_w_in[0]
    w_gates = jnp.pad(w_in[:, main_cols:], ((0, 0), (0, LANES - 2 * nh))).astype(BF16)
    z0, gates0 = norm_matmul(xt, row(e_norm_mix[0]), w_in[:, :main_cols].astype(BF16), w_gates, tm=1024, tn=1024)
    out_a = conformer_conv(z0, e_conv_w[0], row(e_conv_b[0]), row(e_ln_g[0]), row(e_ln_b[0]),
                           batch=batch, seq=seq, ts=256)
    gate_bias = jnp.pad(jnp.concatenate([e_b_i[0], e_b_f[0]]), (0, LANES - 2 * nh)).reshape(1, LANES)
    out_b = mlstm(z0, gates0, e_qk_conv_w[0], row(e_qk_conv_b[0]), gate_bias, batch=batch, seq=seq, col0=2 * ch)
    w_out = e_w_out[0].astype(BF16)
    x1 = out_proj(out_a, out_b, w_out[:ch], w_out[ch:], xt, tm=512)
    x2 = ffn(x1, row(e_norm_ffn[0]), e_ffn_w1[0].astype(BF16), e_ffn_w3[0].astype(BF16),
             e_ffn_w2[0].astype(BF16), tm=512, tf=512)

    aw = o_w_in.shape[2] - o_d_skip.shape[1]
    aw //= 3
    sch = o_d_skip.shape[1]
    z1 = norm_matmul(x2, row(o_norm_mix[0]), o_w_in[0].astype(BF16), tm=1024, tn=1024)
    out_c = dilated_attention(z1, batch=batch, seq=seq, width=aw)
    sp = s5_params(o_lam_re[0], o_lam_im[0], o_log_dt[0], o_b_re[0], o_b_im[0], o_c_re[0], o_c_im[0])
    out_d = s5(z1, *sp, row(o_d_skip[0]), o_w_glu[0].astype(BF16), row(o_b_glu[0]),
               batch=batch, seq=seq, col_blk=3 * aw // sch)
    w_out1 = o_w_out[0].astype(BF16)
    w_router = jnp.concatenate(
        [jnp.pad(p, ((0, 0), (0, LANES - N_EXPERTS))) for p in _split3(o_router[0])], axis=1)
    x3, xp3, eid, gates = out_proj_route(out_c, out_d, w_out1[:aw], w_out1[aw:], x2, row(o_norm_ffn[0]),
                                         w_router, tm=512)
    tile_e, tile_v, src, dst, n_out_rows = route_slots(eid[:, :2], dff=o_exp_w1.shape[3])
    ys = expert_ffn(tile_e, tile_v, src, dst, xp3, o_exp_w1[0], o_exp_w3[0], o_exp_w2[0], n_out_rows=n_out_rows)
    out = combine(x3, gates, row(final_norm), ys)
    return out.reshape(batch, seq, d)
```

```python
import functools
import math

import jax
import jax.numpy as jnp
from jax import lax
from jax.experimental import pallas as pl
from jax.experimental.pallas import tpu as pltpu

F32 = jnp.float32
BF16 = jnp.bfloat16

RMS_EPS = 1e-6
LN_EPS = 1e-5
CONV_WIDTH = 31
MLSTM_HEADS = 8
MLSTM_QK_CONV = 4
MLSTM_CHUNK = 128
ATTN_HEADS = 8
ATTN_BLOCK = 128
DILATIONS = (1, 4, 16)
SSM_GROUP = 16
SSM_STATE = 64
N_EXPERTS = 8
LANES = 128
SUBLANES = 8
VMEM_LIMIT = 56 * 1024 * 1024


def _cparams(*sem):
    return pltpu.CompilerParams(dimension_semantics=sem, vmem_limit_bytes=VMEM_LIMIT)


def _rms(x, g):
    return x * lax.rsqrt(jnp.mean(x * x, axis=-1, keepdims=True) + RMS_EPS) * g


def _norm_matmul_kernel(x_ref, g_ref, w_ref, o_ref, xn_ref):
    @pl.when(pl.program_id(1) == 0)
    def _():
        xn_ref[...] = _rms(x_ref[...], g_ref[...]).astype(BF16)

    o_ref[...] = jnp.dot(xn_ref[...], w_ref[...], preferred_element_type=F32).astype(o_ref.dtype)


def _norm_matmul_aux_kernel(x_ref, g_ref, w_ref, wa_ref, o_ref, oa_ref, xn_ref):
    @pl.when(pl.program_id(1) == 0)
    def _():
        xn = _rms(x_ref[...], g_ref[...]).astype(BF16)
        xn_ref[...] = xn
        oa_ref[...] = jnp.dot(xn, wa_ref[...], preferred_element_type=F32)

    o_ref[...] = jnp.dot(xn_ref[...], w_ref[...], preferred_element_type=F32).astype(o_ref.dtype)


def norm_matmul(x, g, w, w_aux=None, *, tm, tn):
    t, d = x.shape
    n = w.shape[1]
    grid = (t // tm, n // tn)
    x_spec = pl.BlockSpec((tm, d), lambda i, j: (i, 0))
    g_spec = pl.BlockSpec((1, d), lambda i, j: (0, 0))
    w_spec = pl.BlockSpec((d, tn), lambda i, j: (0, j))
    o_spec = pl.BlockSpec((tm, tn), lambda i, j: (i, j))
    scratch = [pltpu.VMEM((tm, d), BF16)]
    if w_aux is None:
        return pl.pallas_call(
            _norm_matmul_kernel, grid=grid, in_specs=[x_spec, g_spec, w_spec], out_specs=o_spec,
            out_shape=jax.ShapeDtypeStruct((t, n), BF16), scratch_shapes=scratch,
            compiler_params=_cparams("parallel", "arbitrary"), name="norm_matmul")(x, g, w)
    na = w_aux.shape[1]
    return pl.pallas_call(
        _norm_matmul_aux_kernel, grid=grid,
        in_specs=[x_spec, g_spec, w_spec, pl.BlockSpec((d, na), lambda i, j: (0, 0))],
        out_specs=[o_spec, pl.BlockSpec((tm, na), lambda i, j: (i, 0))],
        out_shape=[jax.ShapeDtypeStruct((t, n), BF16), jax.ShapeDtypeStruct((t, na), F32)],
        scratch_shapes=scratch, compiler_params=_cparams("parallel", "arbitrary"),
        name="norm_matmul_aux")(x, g, w, w_aux)


CONV_HALO = 32
CONV_ROWS = 64
NORM_ROWS = 32


def _conv_kernel(u_ref, halo_ref, w_ref, cb_ref, lg_ref, lb_ref, o_ref, buf_ref, xs_ref, acc_ref, *, ts, ch):
    first = pl.program_id(1) == 0

    def glu(u):
        u = u.astype(F32)
        return u[:, :ch] * jax.nn.sigmoid(u[:, ch:])

    buf_ref[0:CONV_HALO, :] = jnp.where(first, 0.0, glu(halo_ref[...]))
    buf_ref[CONV_HALO:CONV_HALO + ts, :] = glu(u_ref[...])
    n_shift = ts + CONV_HALO - SUBLANES
    for b in range(1, SUBLANES):
        xs_ref[b, 0:n_shift, :] = buf_ref[b:b + n_shift, :]

    lead = CONV_HALO - (CONV_WIDTH - 1)

    nsub = CONV_ROWS // SUBLANES
    for lc in range(ch // LANES):
        cols = slice(lc * LANES, (lc + 1) * LANES)
        taps = [jnp.broadcast_to(w_ref[k:k + 1, cols], (SUBLANES, LANES)) for k in range(CONV_WIDTH)]
        bias = jnp.broadcast_to(cb_ref[:, cols], (SUBLANES, LANES))

        def conv_step(r, carry, cols=cols, taps=taps, bias=bias):
            base = pl.multiple_of(r * CONV_ROWS, CONV_ROWS)
            acc = [bias] * nsub
            for k in range(CONV_WIDTH):
                a, b = divmod(lead + k, SUBLANES)
                for j in range(nsub):
                    rows = pl.ds(base + SUBLANES * (a + j), SUBLANES)
                    xk = buf_ref[rows, cols] if b == 0 else xs_ref[b, rows, cols]
                    acc[j] = acc[j] + taps[k] * xk
            for j in range(nsub):
                acc_ref[pl.ds(base + SUBLANES * j, SUBLANES), cols] = acc[j]
            return carry

        lax.fori_loop(0, ts // CONV_ROWS, conv_step, 0)

    def norm_step(r, carry):
        base = pl.multiple_of(r * NORM_ROWS, NORM_ROWS)
        a = acc_ref[pl.ds(base, NORM_ROWS), :]
        mu = jnp.mean(a, axis=-1, keepdims=True)
        d = a - mu
        var = jnp.mean(d * d, axis=-1, keepdims=True)
        an = d * lax.rsqrt(var + LN_EPS) * lg_ref[...] + lb_ref[...]
        o_ref[pl.ds(base, NORM_ROWS), :] = (an * jax.nn.sigmoid(an)).astype(o_ref.dtype)
        return carry

    lax.fori_loop(0, ts // NORM_ROWS, norm_step, 0, unroll=4)


def conformer_conv(z, conv_w, conv_b, ln_g, ln_b, *, batch, seq, ts):
    ch = conv_w.shape[1]
    nts = seq // ts
    halo_blocks = ts // CONV_HALO
    kern = functools.partial(_conv_kernel, ts=ts, ch=ch)
    vec = lambda: pl.BlockSpec((1, ch), lambda b, i: (0, 0))
    return pl.pallas_call(
        kern, grid=(batch, nts),
        in_specs=[
            pl.BlockSpec((ts, 2 * ch), lambda b, i: (b * nts + i, 0)),
            pl.BlockSpec((CONV_HALO, 2 * ch),
                         lambda b, i: (jnp.maximum((b * nts + i) * halo_blocks - 1, 0), 0)),
            pl.BlockSpec((CONV_WIDTH, ch), lambda b, i: (0, 0)),
            vec(), vec(), vec(),
        ],
        out_specs=pl.BlockSpec((ts, ch), lambda b, i: (b * nts + i, 0)),
        out_shape=jax.ShapeDtypeStruct((batch * seq, ch), BF16),
        scratch_shapes=[
            pltpu.VMEM((CONV_HALO + ts, ch), F32),
            pltpu.VMEM((SUBLANES, CONV_HALO + ts, ch), F32),
            pltpu.VMEM((ts, ch), F32),
        ],
        compiler_params=_cparams("parallel", "arbitrary"), name="conformer_conv",
    )(z, z, conv_w, conv_b, ln_g, ln_b)


QK_HALO = 16


def _mlstm_kernel(zqk_ref, halo_ref, zv_ref, zo_ref, g_ref, cw_ref, cb_ref, gb_ref, o_ref,
                  qb_ref, c_ref, n_ref, m_ref, *, nh, dh):
    L = MLSTM_CHUNK
    first = pl.program_id(1) == 0

    @pl.when(first)
    def _():
        c_ref[...] = jnp.zeros_like(c_ref)
        n_ref[...] = jnp.zeros_like(n_ref)
        m_ref[...] = jnp.zeros_like(m_ref)

    qb_ref[0:QK_HALO, :] = jnp.where(first, 0.0, halo_ref[...].astype(F32))
    qb_ref[QK_HALO:QK_HALO + L, :] = zqk_ref[...].astype(F32)
    lead = QK_HALO - (MLSTM_QK_CONV - 1)
    acc = jnp.broadcast_to(cb_ref[...], (L, 2 * nh * dh))
    for k in range(MLSTM_QK_CONV):
        acc = acc + cw_ref[k:k + 1, :] * qb_ref[lead + k:lead + k + L, :]
    qk = acc * jax.nn.sigmoid(acc)
    row = lax.broadcasted_iota(jnp.int32, (L, L), 0)
    col = lax.broadcasted_iota(jnp.int32, (L, L), 1)

    g = g_ref[...] + gb_ref[...]
    logf = jax.nn.log_sigmoid(g)
    causal = col <= row
    tri = causal.astype(F32)
    bcum = jnp.dot(tri, logf, preferred_element_type=F32, precision=lax.Precision.HIGHEST)
    g_t = g.T
    b_t = bcum.T
    scale = dh ** -0.5

    for h in range(nh):
        q = qk[:, h * dh:(h + 1) * dh].astype(BF16)
        kf = qk[:, (nh + h) * dh:(nh + h + 1) * dh] * scale
        k = kf.astype(BF16)
        v = zv_ref[:, h * dh:(h + 1) * dh]
        b_col = bcum[:, nh + h:nh + h + 1]
        b_row = b_t[nh + h:nh + h + 1, :]
        i_col = g[:, h:h + 1]
        i_row = g_t[h:h + 1, :]
        m_prev = m_ref[h, 0:1, 0:1]
        c_prev = c_ref[h]
        n_prev = n_ref[h, 0:1, :]

        log_d = jnp.where(causal, b_col - b_row + i_row, -jnp.inf)
        inter = b_col + m_prev
        m_t = jnp.maximum(inter, jnp.max(log_d, axis=-1, keepdims=True))
        s = lax.dot_general(q, k, (((1,), (1,)), ((), ())), preferred_element_type=F32)
        s = s * jnp.exp(log_d - m_t)
        w_int = jnp.exp(inter - m_t)
        num = (jnp.dot(s.astype(BF16), v, preferred_element_type=F32)
               + w_int * jnp.dot(q, c_prev.astype(BF16), preferred_element_type=F32))
        qn = jnp.sum(q.astype(F32) * n_prev, axis=-1, keepdims=True)
        den = jnp.sum(s, axis=-1, keepdims=True) + w_int * qn
        hval = num / jnp.maximum(jnp.abs(den), jnp.exp(-m_t))

        b_last = b_col[L - 1:L, :]
        gk = b_last - b_col + i_col
        m_new = jnp.maximum(b_last + m_prev, jnp.max(gk, axis=0, keepdims=True))
        w_k = jnp.exp(gk - m_new)
        decay = jnp.exp(b_last + m_prev - m_new)
        kw = kf * w_k
        c_ref[h] = decay * c_prev + lax.dot_general(
            kw.astype(BF16), v, (((0,), (0,)), ((), ())), preferred_element_type=F32)
        n_ref[h] = jnp.broadcast_to(decay * n_prev + jnp.sum(kw, axis=0, keepdims=True), (SUBLANES, dh))
        m_ref[h] = jnp.broadcast_to(m_new, (SUBLANES, LANES))

        gate_o = jax.nn.sigmoid(zo_ref[:, h * dh:(h + 1) * dh].astype(F32))
        o_ref[:, h * dh:(h + 1) * dh] = (gate_o * hval).astype(o_ref.dtype)


def mlstm(z, gates, qk_conv_w, qk_conv_b, gate_bias, *, batch, seq, col0):
    L = MLSTM_CHUNK
    nh = MLSTM_HEADS
    w = qk_conv_w.shape[1] // 2
    dh = w // nh
    nc = seq // L
    qk_blk = col0 // (2 * w)
    v_blk = (col0 + 2 * w) // w
    o_blk = v_blk + 1
    kern = functools.partial(_mlstm_kernel, nh=nh, dh=dh)
    return pl.pallas_call(
        kern, grid=(batch, nc),
        in_specs=[
            pl.BlockSpec((L, 2 * w), lambda b, c: (b * nc + c, qk_blk)),
            pl.BlockSpec((QK_HALO, 2 * w),
                         lambda b, c: (jnp.maximum((b * nc + c) * (L // QK_HALO) - 1, 0), qk_blk)),
            pl.BlockSpec((L, w), lambda b, c: (b * nc + c, v_blk)),
            pl.BlockSpec((L, w), lambda b, c: (b * nc + c, o_blk)),
            pl.BlockSpec((L, LANES), lambda b, c: (b * nc + c, 0)),
            pl.BlockSpec((MLSTM_QK_CONV, 2 * w), lambda b, c: (0, 0)),
            pl.BlockSpec((1, 2 * w), lambda b, c: (0, 0)),
            pl.BlockSpec((1, LANES), lambda b, c: (0, 0)),
        ],
        out_specs=pl.BlockSpec((L, w), lambda b, c: (b * nc + c, 0)),
        out_shape=jax.ShapeDtypeStruct((batch * seq, w), BF16),
        scratch_shapes=[
            pltpu.VMEM((QK_HALO + L, 2 * w), F32),
            pltpu.VMEM((nh, dh, dh), F32),
            pltpu.VMEM((nh, SUBLANES, dh), F32),
            pltpu.VMEM((nh, SUBLANES, LANES), F32),
        ],
        compiler_params=_cparams("parallel", "arbitrary"), name="mlstm",
    )(z, z, z, z, gates, qk_conv_w, qk_conv_b, gate_bias)


def _out_proj_kernel(a_ref, b_ref, wa_ref, wb_ref, r_ref, o_ref):
    o_ref[...] = (r_ref[...]
                  + jnp.dot(a_ref[...], wa_ref[...], preferred_element_type=F32)
                  + jnp.dot(b_ref[...], wb_ref[...], preferred_element_type=F32))


def out_proj(a, b, wa, wb, res, *, tm):
    t, d = res.shape
    ka, kb = a.shape[1], b.shape[1]
    return pl.pallas_call(
        _out_proj_kernel, grid=(t // tm,),
        in_specs=[
            pl.BlockSpec((tm, ka), lambda i: (i, 0)),
            pl.BlockSpec((tm, kb), lambda i: (i, 0)),
            pl.BlockSpec((ka, d), lambda i: (0, 0)),
            pl.BlockSpec((kb, d), lambda i: (0, 0)),
            pl.BlockSpec((tm, d), lambda i: (i, 0)),
        ],
        out_specs=pl.BlockSpec((tm, d), lambda i: (i, 0)),
        out_shape=jax.ShapeDtypeStruct((t, d), F32),
        compiler_params=_cparams("parallel"), name="out_proj",
    )(a, b, wa, wb, res)


def _ffn_kernel(x_ref, g_ref, w1_ref, w3_ref, w2_ref, o_ref, xn_ref):
    f = pl.program_id(1)

    @pl.when(f == 0)
    def _():
        x = x_ref[...]
        xn_ref[...] = _rms(x, g_ref[...]).astype(BF16)
        o_ref[...] = x

    xn = xn_ref[...]
    a = jnp.dot(xn, w1_ref[...], preferred_element_type=F32)
    b = jnp.dot(xn, w3_ref[...], preferred_element_type=F32)
    hid = (a * jax.nn.sigmoid(a) * b).astype(BF16)
    o_ref[...] += jnp.dot(hid, w2_ref[...], preferred_element_type=F32)


def ffn(x, g, w1, w3, w2, *, tm, tf):
    t, d = x.shape
    dff = w1.shape[1]
    return pl.pallas_call(
        _ffn_kernel, grid=(t // tm, dff // tf),
        in_specs=[
            pl.BlockSpec((tm, d), lambda i, f: (i, 0)),
            pl.BlockSpec((1, d), lambda i, f: (0, 0)),
            pl.BlockSpec((d, tf), lambda i, f: (0, f)),
            pl.BlockSpec((d, tf), lambda i, f: (0, f)),
            pl.BlockSpec((tf, d), lambda i, f: (f, 0)),
        ],
        out_specs=pl.BlockSpec((tm, d), lambda i, f: (i, 0)),
        out_shape=jax.ShapeDtypeStruct((t, d), F32),
        scratch_shapes=[pltpu.VMEM((tm, d), BF16)],
        compiler_params=_cparams("parallel", "arbitrary"), name="ffn",
    )(x, g, w1, w3, w2)


def _attn_kernel(q_ref, k_ref, v_ref, o_ref, qf_ref, kf_ref, vf_ref, qd_ref, kd_ref, vd_ref,
                 od_ref, ld_ref, on_ref, ln_ref, *, seq, dh):
    T = ATTN_BLOCK
    nblk = seq // T
    qf_ref[...] = q_ref[...].astype(F32) * (dh ** -0.5)
    kf_ref[...] = k_ref[...].astype(F32)
    vf_ref[...] = v_ref[...].astype(F32)
    kd_ref[0:T, :] = jnp.zeros((T, dh), BF16)
    vd_ref[0:T, :] = jnp.zeros((T, dh), BF16)

    qi = lax.broadcasted_iota(jnp.int32, (T, 2 * T), 0)
    ki = lax.broadcasted_iota(jnp.int32, (T, 2 * T), 1)
    dist = T + qi - ki
    band = (dist >= 0) & (dist <= T)

    for g, dil in enumerate(DILATIONS):
        ls = seq // dil
        nb = ls // T
        for r in range(dil):
            rows = pl.ds(r, ls, stride=dil) if dil > 1 else pl.ds(0, ls)
            qd_ref[r * ls:(r + 1) * ls, :] = qf_ref[rows, :].astype(BF16)
            kd_ref[T + r * ls:T + (r + 1) * ls, :] = kf_ref[rows, :].astype(BF16)
            vd_ref[T + r * ls:T + (r + 1) * ls, :] = vf_ref[rows, :].astype(BF16)

        def block(n, carry):
            base = pl.multiple_of(n * T, T)
            qb = qd_ref[pl.ds(base, T), :]
            kb = kd_ref[pl.ds(base, 2 * T), :]
            vb = vd_ref[pl.ds(base, 2 * T), :]
            s = lax.dot_general(qb, kb, (((1,), (1,)), ((), ())), preferred_element_type=F32)
            kmin = jnp.where(n % nb == 0, T, 0)
            s = jnp.where(band & (ki >= kmin), s, -jnp.inf)
            m = jnp.max(s, axis=-1, keepdims=True)
            p = jnp.exp(s - m)
            l = jnp.sum(p, axis=-1, keepdims=True)
            o = jnp.dot(p.astype(BF16), vb, preferred_element_type=F32) / l
            od_ref[pl.ds(base, T), :] = o
            ld_ref[pl.ds(base, T), :] = jnp.broadcast_to(m + jnp.log(l), (T, dh))
            return carry

        lax.fori_loop(0, nblk, block, 0, unroll=8)

        for r in range(dil):
            rows = pl.ds(r, ls, stride=dil) if dil > 1 else pl.ds(0, ls)
            on_ref[g, rows, :] = od_ref[r * ls:(r + 1) * ls, :]
            ln_ref[g, rows, :] = ld_ref[r * ls:(r + 1) * ls, :]

    def merge(n, carry):
        rows = pl.ds(pl.multiple_of(n * T, T), T)
        lses = [ln_ref[g, rows, :] for g in range(len(DILATIONS))]
        mx = functools.reduce(jnp.maximum, lses)
        ws = [jnp.exp(l - mx) for l in lses]
        tot = functools.reduce(lambda a, b: a + b, ws)
        acc = ws[0] * on_ref[0, rows, :]
        for g in range(1, len(DILATIONS)):
            acc = acc + ws[g] * on_ref[g, rows, :]
        o_ref[rows, :] = (acc / tot).astype(o_ref.dtype)
        return carry

    lax.fori_loop(0, nblk, merge, 0)


def dilated_attention(z, *, batch, seq, width):
    nh = ATTN_HEADS
    dh = width // nh
    ng = len(DILATIONS)
    kern = functools.partial(_attn_kernel, seq=seq, dh=dh)
    blk = lambda off: pl.BlockSpec((seq, dh), lambda b, h: (b, off + h))
    return pl.pallas_call(
        kern, grid=(batch, nh),
        in_specs=[blk(0), blk(nh), blk(2 * nh)],
        out_specs=pl.BlockSpec((seq, dh), lambda b, h: (b, h)),
        out_shape=jax.ShapeDtypeStruct((batch * seq, width), BF16),
        scratch_shapes=[
            pltpu.VMEM((seq, dh), F32), pltpu.VMEM((seq, dh), F32), pltpu.VMEM((seq, dh), F32),
            pltpu.VMEM((seq, dh), BF16),
            pltpu.VMEM((ATTN_BLOCK + seq, dh), BF16), pltpu.VMEM((ATTN_BLOCK + seq, dh), BF16),
            pltpu.VMEM((seq, dh), F32), pltpu.VMEM((seq, dh), F32),
            pltpu.VMEM((ng, seq, dh), F32), pltpu.VMEM((ng, seq, dh), F32),
        ],
        compiler_params=_cparams("parallel", "parallel"), name="dilated_attention",
    )(z, z, z)


S5_SEGS = SUBLANES
S5_SEG = 64
S5_PITCH = S5_SEG + 4
S5_TILE = S5_SEGS * S5_SEG
S5_ROWS = S5_SEGS * S5_PITCH
S5_UNROLL = S5_SEG
S5_CHUNK_IN = 256
S5_SLABS = S5_CHUNK_IN // SSM_GROUP * SSM_STATE // LANES


def _s5_kernel(u_ref, bre_ref, bim_ref, cre_ref, cim_ref, lre_ref, lim_ref, dsk_ref, wg_ref, bg_ref,
               o_ref, ug_ref, sre_ref, sim_ref, cyre_ref, cyim_ref, y_ref, *, nchunk):
    first = pl.program_id(1) == 0

    @pl.when(first)
    def _():
        cyre_ref[...] = jnp.zeros_like(cyre_ref)
        cyim_ref[...] = jnp.zeros_like(cyim_ref)
        ug_ref[...] = jnp.zeros_like(ug_ref)

    for i in range(S5_SEGS):
        ug_ref[i * S5_PITCH:i * S5_PITCH + S5_SEG, :] = u_ref[i * S5_SEG:(i + 1) * S5_SEG, :].astype(F32)

    seg_id = lax.broadcasted_iota(jnp.int32, (S5_SEGS, LANES), 0)
    for c in range(nchunk):
        par = (c % 2) * S5_SLABS
        ucols = slice(c * S5_CHUNK_IN, (c + 1) * S5_CHUNK_IN)
        ub = ug_ref[:, ucols].astype(BF16)
        bu_re = jnp.dot(ub, bre_ref[c], preferred_element_type=F32)
        bu_im = jnp.dot(ub, bim_ref[c], preferred_element_type=F32)
        for s in range(S5_SLABS):
            sre_ref[par + s] = bu_re[:, s * LANES:(s + 1) * LANES]
            sim_ref[par + s] = bu_im[:, s * LANES:(s + 1) * LANES]

        lam_re = [jnp.broadcast_to(lre_ref[c, :, s * LANES:(s + 1) * LANES], (S5_SEGS, LANES))
                  for s in range(S5_SLABS)]
        lam_im = [jnp.broadcast_to(lim_ref[c, :, s * LANES:(s + 1) * LANES], (S5_SEGS, LANES))
                  for s in range(S5_SLABS)]

        def step(j, st, store):
            rows = pl.ds(j, S5_SEGS, stride=S5_PITCH)
            new = []
            for s in range(S5_SLABS):
                pr, pi = st[2 * s], st[2 * s + 1]
                nr = lam_re[s] * pr - lam_im[s] * pi + sre_ref[par + s, rows, :]
                ni = lam_re[s] * pi + lam_im[s] * pr + sim_ref[par + s, rows, :]
                if store:
                    sre_ref[par + s, rows, :] = nr
                    sim_ref[par + s, rows, :] = ni
                new += [nr, ni]
            return tuple(new)

        zero = tuple(jnp.zeros((S5_SEGS, LANES), F32) for _ in range(2 * S5_SLABS))
        ends = lax.fori_loop(0, S5_SEG, lambda j, st: step(j, st, False), zero, unroll=S5_UNROLL)

        pw_re, pw_im = [l[0:1] for l in lam_re], [l[0:1] for l in lam_im]
        for _ in range(int(math.log2(S5_SEG))):
            pw_re, pw_im = ([a * a - b * b for a, b in zip(pw_re, pw_im)],
                            [2.0 * a * b for a, b in zip(pw_re, pw_im)])
        init = []
        for s in range(S5_SLABS):
            cols = slice(s * LANES, (s + 1) * LANES)
            cr, ci = cyre_ref[c, 0:1, cols], cyim_ref[c, 0:1, cols]
            in_re = jnp.zeros((S5_SEGS, LANES), F32)
            in_im = jnp.zeros((S5_SEGS, LANES), F32)
            for i in range(S5_SEGS):
                in_re = jnp.where(seg_id == i, cr, in_re)
                in_im = jnp.where(seg_id == i, ci, in_im)
                er, ei = ends[2 * s][i:i + 1], ends[2 * s + 1][i:i + 1]
                cr, ci = (pw_re[s] * cr - pw_im[s] * ci + er, pw_re[s] * ci + pw_im[s] * cr + ei)
            cyre_ref[c, :, cols] = jnp.broadcast_to(cr, (SUBLANES, LANES))
            cyim_ref[c, :, cols] = jnp.broadcast_to(ci, (SUBLANES, LANES))
            init += [in_re, in_im]

        lax.fori_loop(0, S5_SEG, lambda j, st: step(j, st, True), tuple(init), unroll=S5_UNROLL)

        st_re = jnp.concatenate([sre_ref[par + s] for s in range(S5_SLABS)], axis=1).astype(BF16)
        st_im = jnp.concatenate([sim_ref[par + s] for s in range(S5_SLABS)], axis=1).astype(BF16)
        y_ref[:, ucols] = (jnp.dot(st_re, cre_ref[c], preferred_element_type=F32)
                           + jnp.dot(st_im, cim_ref[c], preferred_element_type=F32))

    for i in range(S5_SEGS):
        rows = slice(i * S5_PITCH, i * S5_PITCH + S5_SEG)
        y = y_ref[rows, :] + dsk_ref[...] * ug_ref[rows, :]
        y = jax.nn.gelu(y)
        gate = jnp.dot(y.astype(BF16), wg_ref[...], preferred_element_type=F32) + bg_ref[...]
        o_ref[i * S5_SEG:(i + 1) * S5_SEG, :] = (y * jax.nn.sigmoid(gate)).astype(o_ref.dtype)


def s5(z, b_re, b_im, c_re, c_im, lam_re, lam_im, d_skip, w_glu, b_glu, *, batch, seq, col_blk):
    nchunk, _, nstate = b_re.shape
    ch = nchunk * S5_CHUNK_IN
    nt = seq // S5_TILE
    kern = functools.partial(_s5_kernel, nchunk=nchunk)
    full = lambda shape: pl.BlockSpec(shape, lambda b, i: (0,) * len(shape))
    return pl.pallas_call(
        kern, grid=(batch, nt),
        in_specs=[
            pl.BlockSpec((S5_TILE, ch), lambda b, i: (b * nt + i, col_blk)),
            full(b_re.shape), full(b_im.shape), full(c_re.shape), full(c_im.shape),
            full(lam_re.shape), full(lam_im.shape), full((1, ch)), full((ch, ch)), full((1, ch)),
        ],
        out_specs=pl.BlockSpec((S5_TILE, ch), lambda b, i: (b * nt + i, 0)),
        out_shape=jax.ShapeDtypeStruct((batch * seq, ch), BF16),
        scratch_shapes=[
            pltpu.VMEM((S5_ROWS, ch), F32),
            pltpu.VMEM((2 * S5_SLABS, S5_ROWS, LANES), F32),
            pltpu.VMEM((2 * S5_SLABS, S5_ROWS, LANES), F32),
            pltpu.VMEM((nchunk, SUBLANES, nstate), F32),
            pltpu.VMEM((nchunk, SUBLANES, nstate), F32),
            pltpu.VMEM((S5_ROWS, ch), F32),
        ],
        compiler_params=_cparams("parallel", "arbitrary"), name="s5",
    )(z, b_re, b_im, c_re, c_im, lam_re, lam_im, d_skip, w_glu, b_glu)


def s5_params(lam_re, lam_im, log_dt, b_re, b_im, c_re, c_im):
    ng, ns = lam_re.shape
    gpc = S5_CHUNK_IN // SSM_GROUP
    nchunk = ng // gpc
    dt = jnp.exp(log_dt)[:, None]
    mag = jnp.exp(lam_re * dt)
    lb_re, lb_im = mag * jnp.cos(lam_im * dt), mag * jnp.sin(lam_im * dt)
    den = lam_re * lam_re + lam_im * lam_im
    f_re = ((lb_re - 1.0) * lam_re + lb_im * lam_im) / den
    f_im = (lb_im * lam_re - (lb_re - 1.0) * lam_im) / den
    bb_re = f_re[..., None] * b_re - f_im[..., None] * b_im
    bb_im = f_re[..., None] * b_im + f_im[..., None] * b_re
    eye = jnp.eye(gpc, dtype=F32)

    def pack_b(m):
        m = m.reshape(nchunk, gpc, ns, SSM_GROUP)
        return jnp.einsum('cgpk,gh->cgkhp', m, eye).reshape(nchunk, gpc * SSM_GROUP, gpc * ns).astype(BF16)

    def pack_c(m):
        m = m.reshape(nchunk, gpc, SSM_GROUP, ns)
        return jnp.einsum('cgkp,gh->cgphk', m, eye).reshape(nchunk, gpc * ns, gpc * SSM_GROUP).astype(BF16)

    return (pack_b(bb_re), pack_b(bb_im), pack_c(c_re), pack_c(-c_im),
            lb_re.reshape(nchunk, 1, gpc * ns), lb_im.reshape(nchunk, 1, gpc * ns))


def _pack_bf16_pair(lo, hi):
    lo_bits = lax.bitcast_convert_type(lo.astype(BF16).astype(F32), jnp.uint32) >> 16
    hi_bits = lax.bitcast_convert_type(hi.astype(BF16).astype(F32), jnp.uint32) & jnp.uint32(0xFFFF0000)
    return lo_bits | hi_bits


def _unpack_bf16_pair(w):
    lo = lax.bitcast_convert_type(w << 16, F32)
    hi = lax.bitcast_convert_type(w & jnp.uint32(0xFFFF0000), F32)
    return lo, hi


def _split3(v):
    hi = v.astype(BF16)
    r1 = v - hi.astype(F32)
    mid = r1.astype(BF16)
    lo = (r1 - mid.astype(F32)).astype(BF16)
    return hi, mid, lo


def _out_proj_route_kernel(a_ref, b_ref, wa_ref, wb_ref, r_ref, g_ref, wr_ref, x_ref, xp_ref, eid_ref, gate_ref):
    x = (r_ref[...]
         + jnp.dot(a_ref[...], wa_ref[...], preferred_element_type=F32)
         + jnp.dot(b_ref[...], wb_ref[...], preferred_element_type=F32))
    x_ref[...] = x
    xn = _rms(x, g_ref[...])
    half = xn.shape[1] // 2
    xp_ref[...] = _pack_bf16_pair(xn[:, :half], xn[:, half:])
    xh, xm, xl = _split3(xn)
    ph = jnp.dot(xh, wr_ref[...], preferred_element_type=F32)
    pm = jnp.dot(xm, wr_ref[:, :2 * LANES], preferred_element_type=F32)
    pl_ = jnp.dot(xl, wr_ref[:, :LANES], preferred_element_type=F32)
    logits = (ph[:, :LANES] + (ph[:, LANES:2 * LANES] + pm[:, :LANES])
              + (ph[:, 2 * LANES:] + pm[:, LANES:] + pl_))
    lane = lax.broadcasted_iota(jnp.int32, logits.shape, 1)
    logits = jnp.where(lane < N_EXPERTS, logits, -jnp.inf)
    v1 = jnp.max(logits, axis=-1, keepdims=True)
    i1 = jnp.min(jnp.where(logits == v1, lane, LANES), axis=-1, keepdims=True)
    rest = jnp.where(lane == i1, -jnp.inf, logits)
    v2 = jnp.max(rest, axis=-1, keepdims=True)
    i2 = jnp.min(jnp.where(rest == v2, lane, LANES), axis=-1, keepdims=True)
    e2 = jnp.exp(v2 - v1)
    g1 = 1.0 / (1.0 + e2)
    g2 = e2 / (1.0 + e2)
    eid_ref[...] = jnp.where(lane == 0, i1, jnp.where(lane == 1, i2, 0))
    gate_ref[...] = jnp.where(lane == 0, g1, jnp.where(lane == 1, g2, 0.0))


def out_proj_route(a, b, wa, wb, res, g, w_router, *, tm):
    t, d = res.shape
    ka, kb = a.shape[1], b.shape[1]
    row = lambda n: pl.BlockSpec((tm, n), lambda i: (i, 0))
    full = lambda r, c: pl.BlockSpec((r, c), lambda i: (0, 0))
    return pl.pallas_call(
        _out_proj_route_kernel, grid=(t // tm,),
        in_specs=[row(ka), row(kb), full(ka, d), full(kb, d), row(d), full(1, d), full(d, 3 * LANES)],
        out_specs=[row(d), row(d // 2), row(LANES), row(LANES)],
        out_shape=[jax.ShapeDtypeStruct((t, d), F32), jax.ShapeDtypeStruct((t, d // 2), jnp.uint32),
                   jax.ShapeDtypeStruct((t, LANES), jnp.int32), jax.ShapeDtypeStruct((t, LANES), F32)],
        compiler_params=_cparams("parallel"), name="out_proj_route",
    )(a, b, wa, wb, res, g, w_router)


EXPERT_TM = 1024
EXPERT_TF = 512
COMBINE_TC = 256


def _gather_rows_per_step(tm, nf):
    return -(-tm // ((nf - 1) * SUBLANES)) * SUBLANES


def _expert_kernel(te_ref, tv_ref, src_ref, dst_ref, xp_hbm, w1_ref, w3_ref, w2_ref, ys_hbm,
                   gbuf, xb, acc, obuf, gsem, ssem, *, tm, nf, gr):
    i = pl.program_id(0)
    f = pl.program_id(1)
    ntiles = pl.num_programs(0)
    gb = gr * (nf - 1)
    half = xb.shape[1] // 2

    def gather(tile, r):
        return pltpu.make_async_copy(xp_hbm.at[pl.ds(src_ref[tile * gb + r], 1), :],
                                     gbuf.at[pl.ds(r, 1), :], gsem)

    def scatter(tile, r):
        return pltpu.make_async_copy(obuf.at[pl.ds(r, 1), :],
                                     ys_hbm.at[pl.ds(dst_ref[(tile + 1) * gb + r], 1), :], ssem)

    def for_rows(n, fn):
        def body(r, carry):
            fn(r)
            return carry
        lax.fori_loop(0, n, body, 0, unroll=8)

    @pl.when((i == 0) & (f == 0))
    def _():
        obuf[...] = jnp.zeros_like(obuf)
        for_rows(gb, lambda r: gather(0, r).start())

    @pl.when(f == 0)
    def _():
        for_rows(gb, lambda r: gather(i, r).wait())
        lo, hi = _unpack_bf16_pair(gbuf[0:tm, :])
        xb[:, :half] = lo.astype(BF16)
        xb[:, half:] = hi.astype(BF16)
        acc[...] = jnp.zeros_like(acc)

    def swiglu_rows(rows):
        x = xb[0:rows, :]
        a = jnp.dot(x, w1_ref[0].astype(BF16), preferred_element_type=F32)
        b = jnp.dot(x, w3_ref[0].astype(BF16), preferred_element_type=F32)
        hid = (a * jax.nn.sigmoid(a) * b).astype(BF16)
        acc[0:rows, :] += jnp.dot(hid, w2_ref[0].astype(BF16), preferred_element_type=F32)

    def step(rows, moving):
        if moving:
            for r in range(gr):
                gather(i + 1, f * gr + r).start()
                scatter(i - 1, f * gr + r).start()
        if rows:
            swiglu_rows(rows)

    used = tv_ref[i]
    is_last = f == nf - 1
    for cond, rows in ((used > tm // 2, tm), ((used > 0) & (used <= tm // 2), tm // 2), (used == 0, 0)):
        pl.when(cond & jnp.logical_not(is_last))(functools.partial(step, rows, True))
        pl.when(cond & is_last)(functools.partial(step, rows, False))

    @pl.when(is_last)
    def _():
        for_rows(gb, lambda r: scatter(i - 1, r).wait())
        y = acc[...]
        obuf[0:tm, :] = _pack_bf16_pair(y[:, :half], y[:, half:])

    @pl.when((i == ntiles - 1) & is_last)
    def _():
        for_rows(gb, lambda r: gather(i + 1, r).wait())
        for_rows(gb, lambda r: scatter(i, r).start())
        for_rows(gb, lambda r: scatter(i, r).wait())


def expert_ffn(tile_e, tile_v, src, dst, xp, w1, w3, w2, *, n_out_rows):
    tm, tf = EXPERT_TM, EXPERT_TF
    dh = xp.shape[1]
    d = 2 * dh
    dff = w1.shape[2]
    nf = dff // tf
    gr = _gather_rows_per_step(tm, nf)
    ntiles = tile_e.shape[0]
    fsel = lambda i, f, te, tv: jnp.where(tv[i] > 0, f, nf - 1)
    kern = functools.partial(_expert_kernel, tm=tm, nf=nf, gr=gr)
    return pl.pallas_call(
        kern,
        grid_spec=pltpu.PrefetchScalarGridSpec(
            num_scalar_prefetch=4, grid=(ntiles, nf),
            in_specs=[
                pl.BlockSpec(memory_space=pl.ANY),
                pl.BlockSpec((1, d, tf), lambda i, f, te, tv, s, t: (te[i], 0, fsel(i, f, te, tv))),
                pl.BlockSpec((1, d, tf), lambda i, f, te, tv, s, t: (te[i], 0, fsel(i, f, te, tv))),
                pl.BlockSpec((1, tf, d), lambda i, f, te, tv, s, t: (te[i], fsel(i, f, te, tv), 0)),
            ],
            out_specs=pl.BlockSpec(memory_space=pl.ANY),
            scratch_shapes=[
                pltpu.VMEM((gr * (nf - 1), dh), jnp.uint32),
                pltpu.VMEM((tm, d), BF16),
                pltpu.VMEM((tm, d), F32),
                pltpu.VMEM((gr * (nf - 1), dh), jnp.uint32),
                pltpu.SemaphoreType.DMA(()),
                pltpu.SemaphoreType.DMA(()),
            ]),
        out_shape=jax.ShapeDtypeStruct((n_out_rows, dh), jnp.uint32),
        compiler_params=_cparams("arbitrary", "arbitrary"), name="expert_ffn",
    )(tile_e, tile_v, src, dst, xp, w1, w3, w2)


def _combine_kernel(x_ref, gate_ref, g_ref, y1_ref, y2_ref, o_ref):
    gates = gate_ref[...]
    half = x_ref.shape[1] // 2
    lo1, hi1 = _unpack_bf16_pair(y1_ref[...])
    lo2, hi2 = _unpack_bf16_pair(y2_ref[...])
    g1, g2 = gates[:, 0:1], gates[:, 1:2]
    x = x_ref[...]
    x = jnp.concatenate([x[:, :half] + g1 * lo1 + g2 * lo2, x[:, half:] + g1 * hi1 + g2 * hi2], axis=1)
    o_ref[...] = _rms(x, g_ref[...])


def combine(x, gates, g, ys):
    t, d = x.shape
    tc = COMBINE_TC
    return pl.pallas_call(
        _combine_kernel, grid=(t // tc,),
        in_specs=[
            pl.BlockSpec((tc, d), lambda i: (i, 0)),
            pl.BlockSpec((tc, LANES), lambda i: (i, 0)),
            pl.BlockSpec((1, d), lambda i: (0, 0)),
            pl.BlockSpec((tc, d // 2), lambda i: (i, 0)),
            pl.BlockSpec((tc, d // 2), lambda i: (t // tc + i, 0)),
        ],
        out_specs=pl.BlockSpec((tc, d), lambda i: (i, 0)),
        out_shape=jax.ShapeDtypeStruct((t, d), F32),
        compiler_params=_cparams("parallel"), name="moe_combine",
    )(x, gates, g, ys, ys)


def route_slots(eid, *, dff):
    tm = EXPERT_TM
    nf = dff // EXPERT_TF
    gb = _gather_rows_per_step(tm, nf) * (nf - 1)
    t = eid.shape[0]
    na = 2 * t
    e_flat = eid.reshape(na)
    onehot = (e_flat[:, None] == jnp.arange(N_EXPERTS, dtype=jnp.int32)[None, :]).astype(jnp.int32)
    csum = jnp.cumsum(onehot, axis=0)
    rank = jnp.sum((csum - onehot) * onehot, axis=1)
    counts = csum[-1]
    padded = (counts + tm - 1) // tm * tm
    ends = jnp.cumsum(padded)
    slot = jnp.sum(onehot * (ends - padded)[None, :], axis=1) + rank
    ns = na + N_EXPERTS * tm
    ntiles = ns // tm
    asg = jnp.full((ns,), -1, jnp.int32).at[slot].set(jnp.arange(na, dtype=jnp.int32))
    is_pad = asg < 0
    tok = jnp.where(is_pad, 0, asg // 2)
    pad_rank = jnp.cumsum(is_pad.astype(jnp.int32)) - 1
    dst = jnp.where(is_pad, na + gb + pad_rank, (asg % 2) * t + asg // 2).reshape(ntiles, tm)
    over = (na + gb + (ns - na) + jnp.arange(ntiles * (gb - tm), dtype=jnp.int32)).reshape(ntiles, gb - tm)
    dst = jnp.concatenate([na + jnp.arange(gb, dtype=jnp.int32),
                           jnp.concatenate([dst, over], axis=1).reshape(-1)])
    src = jnp.pad(tok.reshape(ntiles, tm), ((0, 1), (0, gb - tm))).reshape(-1)
    starts = jnp.arange(ntiles, dtype=jnp.int32) * tm
    tile_e = jnp.sum((starts[:, None] >= ends[None, :]).astype(jnp.int32), axis=1)
    used_end = jnp.concatenate([ends - padded + counts, jnp.zeros((1,), jnp.int32)])
    tile_v = jnp.clip(used_end[tile_e] - starts, 0, tm)
    last_e = jnp.max(jnp.where(tile_v > 0, tile_e, 0))
    tile_e = jnp.where(tile_v > 0, tile_e, last_e)
    n_out_rows = na + gb + (ns - na) + ntiles * (gb - tm)
    n_out_rows = -(-n_out_rows // COMBINE_TC) * COMBINE_TC
    return tile_e, tile_v, src, dst, n_out_rows


def kernel(x, e_norm_mix, e_w_in, e_conv_w, e_conv_b, e_ln_g, e_ln_b, e_qk_conv_w, e_qk_conv_b, e_b_i, e_b_f, e_w_out, e_norm_ffn, e_ffn_w1, e_ffn_w3, e_ffn_w2, o_norm_mix, o_w_in, o_lam_re, o_lam_im, o_log_dt, o_b_re, o_b_im, o_c_re, o_c_im, o_d_skip, o_w_glu, o_b_glu, o_w_out, o_norm_ffn, o_router, o_exp_w1, o_exp_w3, o_exp_w2, final_norm):
    batch, seq, d = x.shape
    t = batch * seq
    xt = x.reshape(t, d)
    row = lambda v: v.reshape(1, -1)
    assert e_norm_mix.shape[0] == 1 and o_norm_mix.shape[0] == 1, "one even and one odd layer"

    ch = e_conv_w.shape[2]
    mw = e_qk_conv_w.shape[2] // 2
    nh = MLSTM_HEADS
    main_cols = 2 * ch + 4 * mw
    w_in = e_w_in[0]
    w_gates = jnp.pad(w_in[:, main_cols:], ((0, 0), (0, LANES - 2 * nh))).astype(BF16)
    z0, gates0 = norm_matmul(xt, row(e_norm_mix[0]), w_in[:, :main_cols].astype(BF16), w_gates, tm=1024, tn=1024)
    out_a = conformer_conv(z0, e_conv_w[0], row(e_conv_b[0]), row(e_ln_g[0]), row(e_ln_b[0]),
                           batch=batch, seq=seq, ts=256)
    gate_bias = jnp.pad(jnp.concatenate([e_b_i[0], e_b_f[0]]), (0, LANES - 2 * nh)).reshape(1, LANES)
    out_b = mlstm(z0, gates0, e_qk_conv_w[0], row(e_qk_conv_b[0]), gate_bias, batch=batch, seq=seq, col0=2 * ch)
    w_out = e_w_out[0].astype(BF16)
    x1 = out_proj(out_a, out_b, w_out[:ch], w_out[ch:], xt, tm=512)
    x2 = ffn(x1, row(e_norm_ffn[0]), e_ffn_w1[0].astype(BF16), e_ffn_w3[0].astype(BF16),
             e_ffn_w2[0].astype(BF16), tm=512, tf=512)

    aw = o_w_in.shape[2] - o_d_skip.shape[1]
    aw //= 3
    sch = o_d_skip.shape[1]
    z1 = norm_matmul(x2, row(o_norm_mix[0]), o_w_in[0].astype(BF16), tm=1024, tn=1024)
    out_c = dilated_attention(z1, batch=batch, seq=seq, width=aw)
    sp = s5_params(o_lam_re[0], o_lam_im[0], o_log_dt[0], o_b_re[0], o_b_im[0], o_c_re[0], o_c_im[0])
    out_d = s5(z1, *sp, row(o_d_skip[0]), o_w_glu[0].astype(BF16), row(o_b_glu[0]),
               batch=batch, seq=seq, col_blk=3 * aw // sch)
    w_out1 = o_w_out[0].astype(BF16)
    w_router = jnp.concatenate(
        [jnp.pad(p, ((0, 0), (0, LANES - N_EXPERTS))) for p in _split3(o_router[0])], axis=1)
    x3, xp3, eid, gates = out_proj_route(out_c, out_d, w_out1[:aw], w_out1[aw:], x2, row(o_norm_ffn[0]),
                                         w_router, tm=512)
    tile_e, tile_v, src, dst, n_out_rows = route_slots(eid[:, :2], dff=o_exp_w1.shape[3])
    ys = expert_ffn(tile_e, tile_v, src, dst, xp3, o_exp_w1[0], o_exp_w3[0], o_exp_w2[0], n_out_rows=n_out_rows)
    out = combine(x3, gates, row(final_norm), ys)
    return out.reshape(batch, seq, d)
```

```python
import functools
import math

import jax
import jax.numpy as jnp
from jax import lax
from jax.experimental import pallas as pl
from jax.experimental.pallas import tpu as pltpu

F32 = jnp.float32
BF16 = jnp.bfloat16

RMS_EPS = 1e-6
LN_EPS = 1e-5
CONV_WIDTH = 31
MLSTM_HEADS = 8
MLSTM_QK_CONV = 4
MLSTM_CHUNK = 128
ATTN_HEADS = 8
ATTN_BLOCK = 128
DILATIONS = (1, 4, 16)
SSM_GROUP = 16
SSM_STATE = 64
N_EXPERTS = 8
LANES = 128
SUBLANES = 8
VMEM_LIMIT = 56 * 1024 * 1024


def _cparams(*sem):
    return pltpu.CompilerParams(dimension_semantics=sem, vmem_limit_bytes=VMEM_LIMIT)


def _rms(x, g):
    return x * lax.rsqrt(jnp.mean(x * x, axis=-1, keepdims=True) + RMS_EPS) * g


def _norm_matmul_kernel(x_ref, g_ref, w_ref, o_ref, xn_ref):
    @pl.when(pl.program_id(1) == 0)
    def _():
        xn_ref[...] = _rms(x_ref[...], g_ref[...]).astype(BF16)

    o_ref[...] = jnp.dot(xn_ref[...], w_ref[...], preferred_element_type=F32).astype(o_ref.dtype)


def _norm_matmul_aux_kernel(x_ref, g_ref, w_ref, wa_ref, o_ref, oa_ref, xn_ref):
    @pl.when(pl.program_id(1) == 0)
    def _():
        xn = _rms(x_ref[...], g_ref[...]).astype(BF16)
        xn_ref[...] = xn
        oa_ref[...] = jnp.dot(xn, wa_ref[...], preferred_element_type=F32)

    o_ref[...] = jnp.dot(xn_ref[...], w_ref[...], preferred_element_type=F32).astype(o_ref.dtype)


def norm_matmul(x, g, w, w_aux=None, *, tm, tn, n_out=None):
    t, d = x.shape
    n = w.shape[1] if n_out is None else n_out
    grid = (t // tm, n // tn)
    x_spec = pl.BlockSpec((tm, d), lambda i, j: (i, 0))
    g_spec = pl.BlockSpec((1, d), lambda i, j: (0, 0))
    w_spec = pl.BlockSpec((d, tn), lambda i, j: (0, j))
    o_spec = pl.BlockSpec((tm, tn), lambda i, j: (i, j))
    scratch = [pltpu.VMEM((tm, d), BF16)]
    if w_aux is None:
        return pl.pallas_call(
            _norm_matmul_kernel, grid=grid, in_specs=[x_spec, g_spec, w_spec], out_specs=o_spec,
            out_shape=jax.ShapeDtypeStruct((t, n), BF16), scratch_shapes=scratch,
            compiler_params=_cparams("parallel", "arbitrary"), name="norm_matmul")(x, g, w)
    na = w_aux.shape[1]
    return pl.pallas_call(
        _norm_matmul_aux_kernel, grid=grid,
        in_specs=[x_spec, g_spec, w_spec, pl.BlockSpec((d, na), lambda i, j: (0, 0))],
        out_specs=[o_spec, pl.BlockSpec((tm, na), lambda i, j: (i, 0))],
        out_shape=[jax.ShapeDtypeStruct((t, n), BF16), jax.ShapeDtypeStruct((t, na), F32)],
        scratch_shapes=scratch, compiler_params=_cparams("parallel", "arbitrary"),
        name="norm_matmul_aux")(x, g, w, w_aux)


CONV_HALO = 32
CONV_ROWS = 64
NORM_ROWS = 32


def _conv_kernel(u_ref, halo_ref, w_ref, cb_ref, lg_ref, lb_ref, o_ref, buf_ref, xs_ref, acc_ref, *, ts, ch):
    first = pl.program_id(1) == 0

    def glu(u):
        u = u.astype(F32)
        return u[:, :ch] * jax.nn.sigmoid(u[:, ch:])

    buf_ref[0:CONV_HALO, :] = jnp.where(first, 0.0, glu(halo_ref[...]))
    buf_ref[CONV_HALO:CONV_HALO + ts, :] = glu(u_ref[...])
    n_shift = ts + CONV_HALO - SUBLANES
    for b in range(1, SUBLANES):
        xs_ref[b, 0:n_shift, :] = buf_ref[b:b + n_shift, :]

    lead = CONV_HALO - (CONV_WIDTH - 1)

    nsub = CONV_ROWS // SUBLANES
    for lc in range(ch // LANES):
        cols = slice(lc * LANES, (lc + 1) * LANES)
        taps = [jnp.broadcast_to(w_ref[k:k + 1, cols], (SUBLANES, LANES)) for k in range(CONV_WIDTH)]
        bias = jnp.broadcast_to(cb_ref[:, cols], (SUBLANES, LANES))

        def conv_step(r, carry, cols=cols, taps=taps, bias=bias):
            base = pl.multiple_of(r * CONV_ROWS, CONV_ROWS)
            acc = [bias] * nsub
            for k in range(CONV_WIDTH):
                a, b = divmod(lead + k, SUBLANES)
                for j in range(nsub):
                    rows = pl.ds(base + SUBLANES * (a + j), SUBLANES)
                    xk = buf_ref[rows, cols] if b == 0 else xs_ref[b, rows, cols]
                    acc[j] = acc[j] + taps[k] * xk
            for j in range(nsub):
                acc_ref[pl.ds(base + SUBLANES * j, SUBLANES), cols] = acc[j]
            return carry

        lax.fori_loop(0, ts // CONV_ROWS, conv_step, 0)

    def norm_step(r, carry):
        base = pl.multiple_of(r * NORM_ROWS, NORM_ROWS)
        a = acc_ref[pl.ds(base, NORM_ROWS), :]
        mu = jnp.mean(a, axis=-1, keepdims=True)
        d = a - mu
        var = jnp.mean(d * d, axis=-1, keepdims=True)
        an = d * lax.rsqrt(var + LN_EPS) * lg_ref[...] + lb_ref[...]
        o_ref[pl.ds(base, NORM_ROWS), :] = (an * jax.nn.sigmoid(an)).astype(o_ref.dtype)
        return carry

    lax.fori_loop(0, ts // NORM_ROWS, norm_step, 0, unroll=4)


def conformer_conv(z, conv_w, conv_b, ln_g, ln_b, *, batch, seq, ts):
    ch = conv_w.shape[1]
    nts = seq // ts
    halo_blocks = ts // CONV_HALO
    kern = functools.partial(_conv_kernel, ts=ts, ch=ch)
    vec = lambda: pl.BlockSpec((1, ch), lambda b, i: (0, 0))
    return pl.pallas_call(
        kern, grid=(batch, nts),
        in_specs=[
            pl.BlockSpec((ts, 2 * ch), lambda b, i: (b * nts + i, 0)),
            pl.BlockSpec((CONV_HALO, 2 * ch),
                         lambda b, i: (jnp.maximum((b * nts + i) * halo_blocks - 1, 0), 0)),
            pl.BlockSpec((CONV_WIDTH, ch), lambda b, i: (0, 0)),
            vec(), vec(), vec(),
        ],
        out_specs=pl.BlockSpec((ts, ch), lambda b, i: (b * nts + i, 0)),
        out_shape=jax.ShapeDtypeStruct((batch * seq, ch), BF16),
        scratch_shapes=[
            pltpu.VMEM((CONV_HALO + ts, ch), F32),
            pltpu.VMEM((SUBLANES, CONV_HALO + ts, ch), F32),
            pltpu.VMEM((ts, ch), F32),
        ],
        compiler_params=_cparams("parallel", "arbitrary"), name="conformer_conv",
    )(z, z, conv_w, conv_b, ln_g, ln_b)


QK_HALO = 16


def _mlstm_kernel(zqk_ref, halo_ref, zv_ref, zo_ref, g_ref, cw_ref, cb_ref, gb_ref, o_ref,
                  qb_ref, c_ref, n_ref, m_ref, *, nh, dh):
    L = MLSTM_CHUNK
    first = pl.program_id(1) == 0

    @pl.when(first)
    def _():
        c_ref[...] = jnp.zeros_like(c_ref)
        n_ref[...] = jnp.zeros_like(n_ref)
        m_ref[...] = jnp.zeros_like(m_ref)

    qb_ref[0:QK_HALO, :] = jnp.where(first, 0.0, halo_ref[...].astype(F32))
    qb_ref[QK_HALO:QK_HALO + L, :] = zqk_ref[...].astype(F32)
    lead = QK_HALO - (MLSTM_QK_CONV - 1)
    acc = jnp.broadcast_to(cb_ref[...], (L, 2 * nh * dh))
    for k in range(MLSTM_QK_CONV):
        acc = acc + cw_ref[k:k + 1, :] * qb_ref[lead + k:lead + k + L, :]
    qk = acc * jax.nn.sigmoid(acc)
    row = lax.broadcasted_iota(jnp.int32, (L, L), 0)
    col = lax.broadcasted_iota(jnp.int32, (L, L), 1)

    g = g_ref[...] + gb_ref[...]
    logf = jax.nn.log_sigmoid(g)
    causal = col <= row
    tri = causal.astype(F32)
    bcum = jnp.dot(tri, logf, preferred_element_type=F32, precision=lax.Precision.HIGHEST)
    g_t = g.T
    b_t = bcum.T
    scale = dh ** -0.5

    for h in range(nh):
        q = qk[:, h * dh:(h + 1) * dh].astype(BF16)
        kf = qk[:, (nh + h) * dh:(nh + h + 1) * dh] * scale
        k = kf.astype(BF16)
        v = zv_ref[:, h * dh:(h + 1) * dh]
        b_col = bcum[:, nh + h:nh + h + 1]
        b_row = b_t[nh + h:nh + h + 1, :]
        i_col = g[:, h:h + 1]
        i_row = g_t[h:h + 1, :]
        m_prev = m_ref[h, 0:1, 0:1]
        c_prev = c_ref[h]
        n_prev = n_ref[h, 0:1, :]

        log_d = jnp.where(causal, b_col - b_row + i_row, -jnp.inf)
        inter = b_col + m_prev
        m_t = jnp.maximum(inter, jnp.max(log_d, axis=-1, keepdims=True))
        s = lax.dot_general(q, k, (((1,), (1,)), ((), ())), preferred_element_type=F32)
        s = s * jnp.exp(log_d - m_t)
        w_int = jnp.exp(inter - m_t)
        num = (jnp.dot(s.astype(BF16), v, preferred_element_type=F32)
               + w_int * jnp.dot(q, c_prev.astype(BF16), preferred_element_type=F32))
        qn = jnp.sum(q.astype(F32) * n_prev, axis=-1, keepdims=True)
        den = jnp.sum(s, axis=-1, keepdims=True) + w_int * qn
        hval = num / jnp.maximum(jnp.abs(den), jnp.exp(-m_t))

        b_last = b_col[L - 1:L, :]
        gk = b_last - b_col + i_col
        m_new = jnp.maximum(b_last + m_prev, jnp.max(gk, axis=0, keepdims=True))
        w_k = jnp.exp(gk - m_new)
        decay = jnp.exp(b_last + m_prev - m_new)
        kw = kf * w_k
        c_ref[h] = decay * c_prev + lax.dot_general(
            kw.astype(BF16), v, (((0,), (0,)), ((), ())), preferred_element_type=F32)
        n_ref[h] = jnp.broadcast_to(decay * n_prev + jnp.sum(kw, axis=0, keepdims=True), (SUBLANES, dh))
        m_ref[h] = jnp.broadcast_to(m_new, (SUBLANES, LANES))

        gate_o = jax.nn.sigmoid(zo_ref[:, h * dh:(h + 1) * dh].astype(F32))
        o_ref[:, h * dh:(h + 1) * dh] = (gate_o * hval).astype(o_ref.dtype)


def mlstm(z, gates, qk_conv_w, qk_conv_b, gate_bias, *, batch, seq, col0):
    L = MLSTM_CHUNK
    nh = MLSTM_HEADS
    w = qk_conv_w.shape[1] // 2
    dh = w // nh
    nc = seq // L
    qk_blk = col0 // (2 * w)
    v_blk = (col0 + 2 * w) // w
    o_blk = v_blk + 1
    kern = functools.partial(_mlstm_kernel, nh=nh, dh=dh)
    return pl.pallas_call(
        kern, grid=(batch, nc),
        in_specs=[
            pl.BlockSpec((L, 2 * w), lambda b, c: (b * nc + c, qk_blk)),
            pl.BlockSpec((QK_HALO, 2 * w),
                         lambda b, c: (jnp.maximum((b * nc + c) * (L // QK_HALO) - 1, 0), qk_blk)),
            pl.BlockSpec((L, w), lambda b, c: (b * nc + c, v_blk)),
            pl.BlockSpec((L, w), lambda b, c: (b * nc + c, o_blk)),
            pl.BlockSpec((L, LANES), lambda b, c: (b * nc + c, 0)),
            pl.BlockSpec((MLSTM_QK_CONV, 2 * w), lambda b, c: (0, 0)),
            pl.BlockSpec((1, 2 * w), lambda b, c: (0, 0)),
            pl.BlockSpec((1, LANES), lambda b, c: (0, 0)),
        ],
        out_specs=pl.BlockSpec((L, w), lambda b, c: (b * nc + c, 0)),
        out_shape=jax.ShapeDtypeStruct((batch * seq, w), BF16),
        scratch_shapes=[
            pltpu.VMEM((QK_HALO + L, 2 * w), F32),
            pltpu.VMEM((nh, dh, dh), F32),
            pltpu.VMEM((nh, SUBLANES, dh), F32),
            pltpu.VMEM((nh, SUBLANES, LANES), F32),
        ],
        compiler_params=_cparams("parallel", "arbitrary"), name="mlstm",
    )(z, z, z, z, gates, qk_conv_w, qk_conv_b, gate_bias)


def _out_proj_kernel(a_ref, b_ref, wa_ref, wb_ref, r_ref, o_ref):
    o_ref[...] = (r_ref[...]
                  + jnp.dot(a_ref[...], wa_ref[...], preferred_element_type=F32)
                  + jnp.dot(b_ref[...], wb_ref[...], preferred_element_type=F32))


def out_proj(a, b, wa, wb, res, *, tm):
    t, d = res.shape
    ka, kb = a.shape[1], b.shape[1]
    return pl.pallas_call(
        _out_proj_kernel, grid=(t // tm,),
        in_specs=[
            pl.BlockSpec((tm, ka), lambda i: (i, 0)),
            pl.BlockSpec((tm, kb), lambda i: (i, 0)),
            pl.BlockSpec((ka, d), lambda i: (0, 0)),
            pl.BlockSpec((kb, d), lambda i: (0, 0)),
            pl.BlockSpec((tm, d), lambda i: (i, 0)),
        ],
        out_specs=pl.BlockSpec((tm, d), lambda i: (i, 0)),
        out_shape=jax.ShapeDtypeStruct((t, d), F32),
        compiler_params=_cparams("parallel"), name="out_proj",
    )(a, b, wa, wb, res)


def _ffn_kernel(x_ref, g_ref, w1_ref, w3_ref, w2_ref, o_ref, xn_ref):
    f = pl.program_id(1)

    @pl.when(f == 0)
    def _():
        x = x_ref[...]
        xn_ref[...] = _rms(x, g_ref[...]).astype(BF16)
        o_ref[...] = x

    xn = xn_ref[...]
    a = jnp.dot(xn, w1_ref[...], preferred_element_type=F32)
    b = jnp.dot(xn, w3_ref[...], preferred_element_type=F32)
    hid = (a * jax.nn.sigmoid(a) * b).astype(BF16)
    o_ref[...] += jnp.dot(hid, w2_ref[...], preferred_element_type=F32)


def ffn(x, g, w1, w3, w2, *, tm, tf):
    t, d = x.shape
    dff = w1.shape[1]
    return pl.pallas_call(
        _ffn_kernel, grid=(t // tm, dff // tf),
        in_specs=[
            pl.BlockSpec((tm, d), lambda i, f: (i, 0)),
            pl.BlockSpec((1, d), lambda i, f: (0, 0)),
            pl.BlockSpec((d, tf), lambda i, f: (0, f)),
            pl.BlockSpec((d, tf), lambda i, f: (0, f)),
            pl.BlockSpec((tf, d), lambda i, f: (f, 0)),
        ],
        out_specs=pl.BlockSpec((tm, d), lambda i, f: (i, 0)),
        out_shape=jax.ShapeDtypeStruct((t, d), F32),
        scratch_shapes=[pltpu.VMEM((tm, d), BF16)],
        compiler_params=_cparams("parallel", "arbitrary"), name="ffn",
    )(x, g, w1, w3, w2)


def _attn_kernel(q_ref, k_ref, v_ref, o_ref, qf_ref, kf_ref, vf_ref, qd_ref, kd_ref, vd_ref,
                 od_ref, ld_ref, on_ref, ln_ref, *, seq, dh):
    T = ATTN_BLOCK
    nblk = seq // T
    qf_ref[...] = q_ref[...].astype(F32) * (dh ** -0.5)
    kf_ref[...] = k_ref[...].astype(F32)
    vf_ref[...] = v_ref[...].astype(F32)
    kd_ref[0:T, :] = jnp.zeros((T, dh), BF16)
    vd_ref[0:T, :] = jnp.zeros((T, dh), BF16)

    qi = lax.broadcasted_iota(jnp.int32, (T, 2 * T), 0)
    ki = lax.broadcasted_iota(jnp.int32, (T, 2 * T), 1)
    dist = T + qi - ki
    band = (dist >= 0) & (dist <= T)

    for g, dil in enumerate(DILATIONS):
        ls = seq // dil
        nb = ls // T
        for r in range(dil):
            rows = pl.ds(r, ls, stride=dil) if dil > 1 else pl.ds(0, ls)
            qd_ref[r * ls:(r + 1) * ls, :] = qf_ref[rows, :].astype(BF16)
            kd_ref[T + r * ls:T + (r + 1) * ls, :] = kf_ref[rows, :].astype(BF16)
            vd_ref[T + r * ls:T + (r + 1) * ls, :] = vf_ref[rows, :].astype(BF16)

        def block(n, carry):
            base = pl.multiple_of(n * T, T)
            qb = qd_ref[pl.ds(base, T), :]
            kb = kd_ref[pl.ds(base, 2 * T), :]
            vb = vd_ref[pl.ds(base, 2 * T), :]
            s = lax.dot_general(qb, kb, (((1,), (1,)), ((), ())), preferred_element_type=F32)
            kmin = jnp.where(n % nb == 0, T, 0)
            s = jnp.where(band & (ki >= kmin), s, -jnp.inf)
            m = jnp.max(s, axis=-1, keepdims=True)
            p = jnp.exp(s - m)
            l = jnp.sum(p, axis=-1, keepdims=True)
            o = jnp.dot(p.astype(BF16), vb, preferred_element_type=F32) / l
            od_ref[pl.ds(base, T), :] = o
            ld_ref[pl.ds(base, T), :] = jnp.broadcast_to(m + jnp.log(l), (T, dh))
            return carry

        lax.fori_loop(0, nblk, block, 0, unroll=8)

        for r in range(dil):
            rows = pl.ds(r, ls, stride=dil) if dil > 1 else pl.ds(0, ls)
            on_ref[g, rows, :] = od_ref[r * ls:(r + 1) * ls, :]
            ln_ref[g, rows, :] = ld_ref[r * ls:(r + 1) * ls, :]

    def merge(n, carry):
        rows = pl.ds(pl.multiple_of(n * T, T), T)
        lses = [ln_ref[g, rows, :] for g in range(len(DILATIONS))]
        mx = functools.reduce(jnp.maximum, lses)
        ws = [jnp.exp(l - mx) for l in lses]
        tot = functools.reduce(lambda a, b: a + b, ws)
        acc = ws[0] * on_ref[0, rows, :]
        for g in range(1, len(DILATIONS)):
            acc = acc + ws[g] * on_ref[g, rows, :]
        o_ref[rows, :] = (acc / tot).astype(o_ref.dtype)
        return carry

    lax.fori_loop(0, nblk, merge, 0)


def dilated_attention(z, *, batch, seq, width):
    nh = ATTN_HEADS
    dh = width // nh
    ng = len(DILATIONS)
    kern = functools.partial(_attn_kernel, seq=seq, dh=dh)
    blk = lambda off: pl.BlockSpec((seq, dh), lambda b, h: (b, off + h))
    return pl.pallas_call(
        kern, grid=(batch, nh),
        in_specs=[blk(0), blk(nh), blk(2 * nh)],
        out_specs=pl.BlockSpec((seq, dh), lambda b, h: (b, h)),
        out_shape=jax.ShapeDtypeStruct((batch * seq, width), BF16),
        scratch_shapes=[
            pltpu.VMEM((seq, dh), F32), pltpu.VMEM((seq, dh), F32), pltpu.VMEM((seq, dh), F32),
            pltpu.VMEM((seq, dh), BF16),
            pltpu.VMEM((ATTN_BLOCK + seq, dh), BF16), pltpu.VMEM((ATTN_BLOCK + seq, dh), BF16),
            pltpu.VMEM((seq, dh), F32), pltpu.VMEM((seq, dh), F32),
            pltpu.VMEM((ng, seq, dh), F32), pltpu.VMEM((ng, seq, dh), F32),
        ],
        compiler_params=_cparams("parallel", "parallel"), name="dilated_attention",
    )(z, z, z)


S5_SEGS = SUBLANES
S5_SEG = 64
S5_PITCH = S5_SEG + 4
S5_TILE = S5_SEGS * S5_SEG
S5_ROWS = S5_SEGS * S5_PITCH
S5_UNROLL = S5_SEG
S5_CHUNK_IN = 256
S5_SLABS = S5_CHUNK_IN // SSM_GROUP * SSM_STATE // LANES


def _s5_kernel(u_ref, bre_ref, bim_ref, cre_ref, cim_ref, lre_ref, lim_ref, dsk_ref, wg_ref, bg_ref,
               o_ref, ug_ref, sre_ref, sim_ref, cyre_ref, cyim_ref, y_ref, *, nchunk):
    first = pl.program_id(1) == 0

    @pl.when(first)
    def _():
        cyre_ref[...] = jnp.zeros_like(cyre_ref)
        cyim_ref[...] = jnp.zeros_like(cyim_ref)
        ug_ref[...] = jnp.zeros_like(ug_ref)

    for i in range(S5_SEGS):
        ug_ref[i * S5_PITCH:i * S5_PITCH + S5_SEG, :] = u_ref[i * S5_SEG:(i + 1) * S5_SEG, :].astype(F32)

    seg_id = lax.broadcasted_iota(jnp.int32, (S5_SEGS, LANES), 0)
    for c in range(nchunk):
        par = (c % 2) * S5_SLABS
        ucols = slice(c * S5_CHUNK_IN, (c + 1) * S5_CHUNK_IN)
        ub = ug_ref[:, ucols].astype(BF16)
        bu_re = jnp.dot(ub, bre_ref[c], preferred_element_type=F32)
        bu_im = jnp.dot(ub, bim_ref[c], preferred_element_type=F32)
        for s in range(S5_SLABS):
            sre_ref[par + s] = bu_re[:, s * LANES:(s + 1) * LANES]
            sim_ref[par + s] = bu_im[:, s * LANES:(s + 1) * LANES]

        lam_re = [jnp.broadcast_to(lre_ref[c, :, s * LANES:(s + 1) * LANES], (S5_SEGS, LANES))
                  for s in range(S5_SLABS)]
        lam_im = [jnp.broadcast_to(lim_ref[c, :, s * LANES:(s + 1) * LANES], (S5_SEGS, LANES))
                  for s in range(S5_SLABS)]

        def step(j, st, store):
            rows = pl.ds(j, S5_SEGS, stride=S5_PITCH)
            new = []
            for s in range(S5_SLABS):
                pr, pi = st[2 * s], st[2 * s + 1]
                nr = lam_re[s] * pr - lam_im[s] * pi + sre_ref[par + s, rows, :]
                ni = lam_re[s] * pi + lam_im[s] * pr + sim_ref[par + s, rows, :]
                if store:
                    sre_ref[par + s, rows, :] = nr
                    sim_ref[par + s, rows, :] = ni
                new += [nr, ni]
            return tuple(new)

        zero = tuple(jnp.zeros((S5_SEGS, LANES), F32) for _ in range(2 * S5_SLABS))
        ends = lax.fori_loop(0, S5_SEG, lambda j, st: step(j, st, False), zero, unroll=S5_UNROLL)

        pw_re, pw_im = [l[0:1] for l in lam_re], [l[0:1] for l in lam_im]
        for _ in range(int(math.log2(S5_SEG))):
            pw_re, pw_im = ([a * a - b * b for a, b in zip(pw_re, pw_im)],
                            [2.0 * a * b for a, b in zip(pw_re, pw_im)])
        init = []
        for s in range(S5_SLABS):
            cols = slice(s * LANES, (s + 1) * LANES)
            cr, ci = cyre_ref[c, 0:1, cols], cyim_ref[c, 0:1, cols]
            in_re = jnp.zeros((S5_SEGS, LANES), F32)
            in_im = jnp.zeros((S5_SEGS, LANES), F32)
            for i in range(S5_SEGS):
                in_re = jnp.where(seg_id == i, cr, in_re)
                in_im = jnp.where(seg_id == i, ci, in_im)
                er, ei = ends[2 * s][i:i + 1], ends[2 * s + 1][i:i + 1]
                cr, ci = (pw_re[s] * cr - pw_im[s] * ci + er, pw_re[s] * ci + pw_im[s] * cr + ei)
            cyre_ref[c, :, cols] = jnp.broadcast_to(cr, (SUBLANES, LANES))
            cyim_ref[c, :, cols] = jnp.broadcast_to(ci, (SUBLANES, LANES))
            init += [in_re, in_im]

        lax.fori_loop(0, S5_SEG, lambda j, st: step(j, st, True), tuple(init), unroll=S5_UNROLL)

        st_re = jnp.concatenate([sre_ref[par + s] for s in range(S5_SLABS)], axis=1).astype(BF16)
        st_im = jnp.concatenate([sim_ref[par + s] for s in range(S5_SLABS)], axis=1).astype(BF16)
        y_ref[:, ucols] = (jnp.dot(st_re, cre_ref[c], preferred_element_type=F32)
                           + jnp.dot(st_im, cim_ref[c], preferred_element_type=F32))

    for i in range(S5_SEGS):
        rows = slice(i * S5_PITCH, i * S5_PITCH + S5_SEG)
        y = y_ref[rows, :] + dsk_ref[...] * ug_ref[rows, :]
        y = jax.nn.gelu(y)
        gate = jnp.dot(y.astype(BF16), wg_ref[...], preferred_element_type=F32) + bg_ref[...]
        o_ref[i * S5_SEG:(i + 1) * S5_SEG, :] = (y * jax.nn.sigmoid(gate)).astype(o_ref.dtype)


def s5(z, b_re, b_im, c_re, c_im, lam_re, lam_im, d_skip, w_glu, b_glu, *, batch, seq, col_blk):
    nchunk, _, nstate = b_re.shape
    ch = nchunk * S5_CHUNK_IN
    nt = seq // S5_TILE
    kern = functools.partial(_s5_kernel, nchunk=nchunk)
    full = lambda shape: pl.BlockSpec(shape, lambda b, i: (0,) * len(shape))
    return pl.pallas_call(
        kern, grid=(batch, nt),
        in_specs=[
            pl.BlockSpec((S5_TILE, ch), lambda b, i: (b * nt + i, col_blk)),
            full(b_re.shape), full(b_im.shape), full(c_re.shape), full(c_im.shape),
            full(lam_re.shape), full(lam_im.shape), full((1, ch)), full((ch, ch)), full((1, ch)),
        ],
        out_specs=pl.BlockSpec((S5_TILE, ch), lambda b, i: (b * nt + i, 0)),
        out_shape=jax.ShapeDtypeStruct((batch * seq, ch), BF16),
        scratch_shapes=[
            pltpu.VMEM((S5_ROWS, ch), F32),
            pltpu.VMEM((2 * S5_SLABS, S5_ROWS, LANES), F32),
            pltpu.VMEM((2 * S5_SLABS, S5_ROWS, LANES), F32),
            pltpu.VMEM((nchunk, SUBLANES, nstate), F32),
            pltpu.VMEM((nchunk, SUBLANES, nstate), F32),
            pltpu.VMEM((S5_ROWS, ch), F32),
        ],
        compiler_params=_cparams("parallel", "arbitrary"), name="s5",
    )(z, b_re, b_im, c_re, c_im, lam_re, lam_im, d_skip, w_glu, b_glu)


def s5_params(lam_re, lam_im, log_dt, b_re, b_im, c_re, c_im):
    ng, ns = lam_re.shape
    gpc = S5_CHUNK_IN // SSM_GROUP
    nchunk = ng // gpc
    dt = jnp.exp(log_dt)[:, None]
    mag = jnp.exp(lam_re * dt)
    lb_re, lb_im = mag * jnp.cos(lam_im * dt), mag * jnp.sin(lam_im * dt)
    den = lam_re * lam_re + lam_im * lam_im
    f_re = ((lb_re - 1.0) * lam_re + lb_im * lam_im) / den
    f_im = (lb_im * lam_re - (lb_re - 1.0) * lam_im) / den
    bb_re = f_re[..., None] * b_re - f_im[..., None] * b_im
    bb_im = f_re[..., None] * b_im + f_im[..., None] * b_re
    eye = jnp.eye(gpc, dtype=F32)

    def pack_b(m):
        m = m.reshape(nchunk, gpc, ns, SSM_GROUP)
        return jnp.einsum('cgpk,gh->cgkhp', m, eye).reshape(nchunk, gpc * SSM_GROUP, gpc * ns).astype(BF16)

    def pack_c(m):
        m = m.reshape(nchunk, gpc, SSM_GROUP, ns)
        return jnp.einsum('cgkp,gh->cgphk', m, eye).reshape(nchunk, gpc * ns, gpc * SSM_GROUP).astype(BF16)

    return (pack_b(bb_re), pack_b(bb_im), pack_c(c_re), pack_c(-c_im),
            lb_re.reshape(nchunk, 1, gpc * ns), lb_im.reshape(nchunk, 1, gpc * ns))


def _pack_bf16_pair(lo, hi):
    lo_bits = lax.bitcast_convert_type(lo.astype(BF16).astype(F32), jnp.uint32) >> 16
    hi_bits = lax.bitcast_convert_type(hi.astype(BF16).astype(F32), jnp.uint32) & jnp.uint32(0xFFFF0000)
    return lo_bits | hi_bits


def _unpack_bf16_pair(w):
    lo = lax.bitcast_convert_type(w << 16, F32)
    hi = lax.bitcast_convert_type(w & jnp.uint32(0xFFFF0000), F32)
    return lo, hi


def _split3(v):
    hi = v.astype(BF16)
    r1 = v - hi.astype(F32)
    mid = r1.astype(BF16)
    lo = (r1 - mid.astype(F32)).astype(BF16)
    return hi, mid, lo


def _out_proj_route_kernel(a_ref, b_ref, wa_ref, wb_ref, r_ref, g_ref, wr_ref, x_ref, xp_ref, eid_ref, gate_ref):
    x = (r_ref[...]
         + jnp.dot(a_ref[...], wa_ref[...], preferred_element_type=F32)
         + jnp.dot(b_ref[...], wb_ref[...], preferred_element_type=F32))
    x_ref[...] = x
    xn = _rms(x, g_ref[...])
    half = xn.shape[1] // 2
    xp_ref[...] = _pack_bf16_pair(xn[:, :half], xn[:, half:])
    xh, xm, xl = _split3(xn)
    ph = jnp.dot(xh, wr_ref[...], preferred_element_type=F32)
    pm = jnp.dot(xm, wr_ref[:, :2 * LANES], preferred_element_type=F32)
    pl_ = jnp.dot(xl, wr_ref[:, :LANES], preferred_element_type=F32)
    logits = (ph[:, :LANES] + (ph[:, LANES:2 * LANES] + pm[:, :LANES])
              + (ph[:, 2 * LANES:] + pm[:, LANES:] + pl_))
    lane = lax.broadcasted_iota(jnp.int32, logits.shape, 1)
    logits = jnp.where(lane < N_EXPERTS, logits, -jnp.inf)
    v1 = jnp.max(logits, axis=-1, keepdims=True)
    i1 = jnp.min(jnp.where(logits == v1, lane, LANES), axis=-1, keepdims=True)
    rest = jnp.where(lane == i1, -jnp.inf, logits)
    v2 = jnp.max(rest, axis=-1, keepdims=True)
    i2 = jnp.min(jnp.where(rest == v2, lane, LANES), axis=-1, keepdims=True)
    e2 = jnp.exp(v2 - v1)
    g1 = 1.0 / (1.0 + e2)
    g2 = e2 / (1.0 + e2)
    eid_ref[...] = jnp.where(lane == 0, i1, jnp.where(lane == 1, i2, 0))
    gate_ref[...] = jnp.where(lane == 0, g1, jnp.where(lane == 1, g2, 0.0))


def out_proj_route(a, b, wa, wb, res, g, w_router, *, tm):
    t, d = res.shape
    ka, kb = a.shape[1], b.shape[1]
    row = lambda n: pl.BlockSpec((tm, n), lambda i: (i, 0))
    full = lambda r, c: pl.BlockSpec((r, c), lambda i: (0, 0))
    return pl.pallas_call(
        _out_proj_route_kernel, grid=(t // tm,),
        in_specs=[row(ka), row(kb), full(ka, d), full(kb, d), row(d), full(1, d), full(d, 3 * LANES)],
        out_specs=[row(d), row(d // 2), row(LANES), row(LANES)],
        out_shape=[jax.ShapeDtypeStruct((t, d), F32), jax.ShapeDtypeStruct((t, d // 2), jnp.uint32),
                   jax.ShapeDtypeStruct((t, LANES), jnp.int32), jax.ShapeDtypeStruct((t, LANES), F32)],
        compiler_params=_cparams("parallel"), name="out_proj_route",
    )(a, b, wa, wb, res, g, w_router)


EXPERT_TM = 1024
EXPERT_TF = 512
COMBINE_TC = 256


def _gather_rows_per_step(tm, nf):
    return -(-tm // ((nf - 1) * SUBLANES)) * SUBLANES


def _expert_kernel(te_ref, tv_ref, src_ref, dst_ref, xp_hbm, w1_ref, w3_ref, w2_ref, ys_hbm,
                   gbuf, xb, acc, obuf, gsem, ssem, *, tm, nf, gr):
    i = pl.program_id(0)
    f = pl.program_id(1)
    ntiles = pl.num_programs(0)
    gb = gr * (nf - 1)
    half = xb.shape[1] // 2

    def gather(tile, r):
        return pltpu.make_async_copy(xp_hbm.at[pl.ds(src_ref[tile * gb + r], 1), :],
                                     gbuf.at[pl.ds(r, 1), :], gsem)

    def scatter(tile, r):
        return pltpu.make_async_copy(obuf.at[pl.ds(r, 1), :],
                                     ys_hbm.at[pl.ds(dst_ref[(tile + 1) * gb + r], 1), :], ssem)

    def for_rows(n, fn):
        def body(r, carry):
            fn(r)
            return carry
        lax.fori_loop(0, n, body, 0, unroll=8)

    @pl.when((i == 0) & (f == 0))
    def _():
        obuf[...] = jnp.zeros_like(obuf)
        for_rows(gb, lambda r: gather(0, r).start())

    @pl.when(f == 0)
    def _():
        for_rows(gb, lambda r: gather(i, r).wait())
        lo, hi = _unpack_bf16_pair(gbuf[0:tm, :])
        xb[:, :half] = lo.astype(BF16)
        xb[:, half:] = hi.astype(BF16)
        acc[...] = jnp.zeros_like(acc)

    def swiglu_rows(rows):
        x = xb[0:rows, :]
        a = jnp.dot(x, w1_ref[0].astype(BF16), preferred_element_type=F32)
        b = jnp.dot(x, w3_ref[0].astype(BF16), preferred_element_type=F32)
        hid = (a * jax.nn.sigmoid(a) * b).astype(BF16)
        acc[0:rows, :] += jnp.dot(hid, w2_ref[0].astype(BF16), preferred_element_type=F32)

    def step(rows, moving):
        if moving:
            pace = src_ref[0] >> 31
            for r in range(gr):
                tok = src_ref[(i + 1) * gb + f * gr + r + pace]
                pltpu.make_async_copy(xp_hbm.at[pl.ds(tok, 1), :],
                                      gbuf.at[pl.ds(f * gr + r, 1), :], gsem).start()
                pace = tok >> 31
                row = dst_ref[i * gb + f * gr + r + pace]
                pltpu.make_async_copy(obuf.at[pl.ds(f * gr + r, 1), :],
                                      ys_hbm.at[pl.ds(row, 1), :], ssem).start()
                pace = row >> 31
        if rows:
            swiglu_rows(rows)

    used = tv_ref[i]
    is_last = f == nf - 1
    quarter = tm // 4
    conds = [(used > q * quarter) & (used <= (q + 1) * quarter) for q in range(4)] + [used == 0]
    for cond, rows in zip(conds, (quarter, 2 * quarter, 3 * quarter, tm, 0)):
        pl.when(cond & jnp.logical_not(is_last))(functools.partial(step, rows, True))
        pl.when(cond & is_last)(functools.partial(step, rows, False))

    @pl.when(is_last)
    def _():
        for_rows(gb, lambda r: scatter(i - 1, r).wait())
        y = acc[...]
        obuf[0:tm, :] = _pack_bf16_pair(y[:, :half], y[:, half:])

    @pl.when((i == ntiles - 1) & is_last)
    def _():
        for_rows(gb, lambda r: gather(i + 1, r).wait())
        for_rows(gb, lambda r: scatter(i, r).start())
        for_rows(gb, lambda r: scatter(i, r).wait())


def expert_ffn(tile_e, tile_v, src, dst, xp, w1, w3, w2, *, n_out_rows):
    tm, tf = EXPERT_TM, EXPERT_TF
    dh = xp.shape[1]
    d = 2 * dh
    dff = w1.shape[2]
    nf = dff // tf
    gr = _gather_rows_per_step(tm, nf)
    ntiles = tile_e.shape[0]
    fsel = lambda i, f, te, tv: jnp.where(tv[i] > 0, f, nf - 1)
    kern = functools.partial(_expert_kernel, tm=tm, nf=nf, gr=gr)
    return pl.pallas_call(
        kern,
        grid_spec=pltpu.PrefetchScalarGridSpec(
            num_scalar_prefetch=4, grid=(ntiles, nf),
            in_specs=[
                pl.BlockSpec(memory_space=pl.ANY),
                pl.BlockSpec((1, d, tf), lambda i, f, te, tv, s, t: (te[i], 0, fsel(i, f, te, tv))),
                pl.BlockSpec((1, d, tf), lambda i, f, te, tv, s, t: (te[i], 0, fsel(i, f, te, tv))),
                pl.BlockSpec((1, tf, d), lambda i, f, te, tv, s, t: (te[i], fsel(i, f, te, tv), 0)),
            ],
            out_specs=pl.BlockSpec(memory_space=pl.ANY),
            scratch_shapes=[
                pltpu.VMEM((gr * (nf - 1), dh), jnp.uint32),
                pltpu.VMEM((tm, d), BF16),
                pltpu.VMEM((tm, d), F32),
                pltpu.VMEM((gr * (nf - 1), dh), jnp.uint32),
                pltpu.SemaphoreType.DMA(()),
                pltpu.SemaphoreType.DMA(()),
            ]),
        out_shape=jax.ShapeDtypeStruct((n_out_rows, dh), jnp.uint32),
        compiler_params=_cparams("arbitrary", "arbitrary"), name="expert_ffn",
    )(tile_e, tile_v, src, dst, xp, w1, w3, w2)


def _combine_kernel(x_ref, gate_ref, g_ref, y1_ref, y2_ref, o_ref):
    gates = gate_ref[...]
    half = x_ref.shape[1] // 2
    lo1, hi1 = _unpack_bf16_pair(y1_ref[...])
    lo2, hi2 = _unpack_bf16_pair(y2_ref[...])
    g1, g2 = gates[:, 0:1], gates[:, 1:2]
    x = x_ref[...]
    x = jnp.concatenate([x[:, :half] + g1 * lo1 + g2 * lo2, x[:, half:] + g1 * hi1 + g2 * hi2], axis=1)
    o_ref[...] = _rms(x, g_ref[...])


def combine(x, gates, g, ys):
    t, d = x.shape
    tc = COMBINE_TC
    return pl.pallas_call(
        _combine_kernel, grid=(t // tc,),
        in_specs=[
            pl.BlockSpec((tc, d), lambda i: (i, 0)),
            pl.BlockSpec((tc, LANES), lambda i: (i, 0)),
            pl.BlockSpec((1, d), lambda i: (0, 0)),
            pl.BlockSpec((tc, d // 2), lambda i: (i, 0)),
            pl.BlockSpec((tc, d // 2), lambda i: (t // tc + i, 0)),
        ],
        out_specs=pl.BlockSpec((tc, d), lambda i: (i, 0)),
        out_shape=jax.ShapeDtypeStruct((t, d), F32),
        compiler_params=_cparams("parallel"), name="moe_combine",
    )(x, gates, g, ys, ys)


def route_slots(eid, *, dff):
    tm = EXPERT_TM
    nf = dff // EXPERT_TF
    gb = _gather_rows_per_step(tm, nf) * (nf - 1)
    t = eid.shape[0]
    na = 2 * t
    e_flat = eid.reshape(na)
    onehot = (e_flat[:, None] == jnp.arange(N_EXPERTS, dtype=jnp.int32)[None, :]).astype(jnp.int32)
    csum = jnp.cumsum(onehot, axis=0)
    rank = jnp.sum((csum - onehot) * onehot, axis=1)
    counts = csum[-1]
    padded = (counts + tm - 1) // tm * tm
    ends = jnp.cumsum(padded)
    slot = jnp.sum(onehot * (ends - padded)[None, :], axis=1) + rank
    ns = na + N_EXPERTS * tm
    ntiles = ns // tm
    asg = jnp.full((ns,), -1, jnp.int32).at[slot].set(jnp.arange(na, dtype=jnp.int32))
    is_pad = asg < 0
    tok = jnp.where(is_pad, 0, asg // 2)
    pad_rank = jnp.cumsum(is_pad.astype(jnp.int32)) - 1
    dst = jnp.where(is_pad, na + gb + pad_rank, (asg % 2) * t + asg // 2).reshape(ntiles, tm)
    over = (na + gb + (ns - na) + jnp.arange(ntiles * (gb - tm), dtype=jnp.int32)).reshape(ntiles, gb - tm)
    dst = jnp.concatenate([na + jnp.arange(gb, dtype=jnp.int32),
                           jnp.concatenate([dst, over], axis=1).reshape(-1)])
    src = jnp.pad(tok.reshape(ntiles, tm), ((0, 1), (0, gb - tm))).reshape(-1)
    starts = jnp.arange(ntiles, dtype=jnp.int32) * tm
    tile_e = jnp.sum((starts[:, None] >= ends[None, :]).astype(jnp.int32), axis=1)
    used_end = jnp.concatenate([ends - padded + counts, jnp.zeros((1,), jnp.int32)])
    tile_v = jnp.clip(used_end[tile_e] - starts, 0, tm)
    last_e = jnp.max(jnp.where(tile_v > 0, tile_e, 0))
    tile_e = jnp.where(tile_v > 0, tile_e, last_e)
    n_out_rows = na + gb + (ns - na) + ntiles * (gb - tm)
    n_out_rows = -(-n_out_rows // COMBINE_TC) * COMBINE_TC
    return tile_e, tile_v, src, dst, n_out_rows


def kernel(x, e_norm_mix, e_w_in, e_conv_w, e_conv_b, e_ln_g, e_ln_b, e_qk_conv_w, e_qk_conv_b, e_b_i, e_b_f, e_w_out, e_norm_ffn, e_ffn_w1, e_ffn_w3, e_ffn_w2, o_norm_mix, o_w_in, o_lam_re, o_lam_im, o_log_dt, o_b_re, o_b_im, o_c_re, o_c_im, o_d_skip, o_w_glu, o_b_glu, o_w_out, o_norm_ffn, o_router, o_exp_w1, o_exp_w3, o_exp_w2, final_norm):
    batch, seq, d = x.shape
    t = batch * seq
    xt = x.reshape(t, d)
    row = lambda v: v.reshape(1, -1)
    assert e_norm_mix.shape[0] == 1 and o_norm_mix.shape[0] == 1, "one even and one odd layer"

    ch = e_conv_w.shape[2]
    mw = e_qk_conv_w.shape[2] // 2
    nh = MLSTM_HEADS
    main_cols = 2 * ch + 4 * mw
    w_in = e_w_in[0].astype(BF16)
    w_gates = jnp.pad(w_in[:, main_cols:], ((0, 0), (0, LANES - 2 * nh)))
    z0, gates0 = norm_matmul(xt, row(e_norm_mix[0]), w_in, w_gates, tm=1024, tn=1024, n_out=main_cols)
    out_a = conformer_conv(z0, e_conv_w[0], row(e_conv_b[0]), row(e_ln_g[0]), row(e_ln_b[0]),
                           batch=batch, seq=seq, ts=256)
    gate_bias = jnp.pad(jnp.concatenate([e_b_i[0], e_b_f[0]]), (0, LANES - 2 * nh)).reshape(1, LANES)
    out_b = mlstm(z0, gates0, e_qk_conv_w[0], row(e_qk_conv_b[0]), gate_bias, batch=batch, seq=seq, col0=2 * ch)
    w_out = e_w_out[0].astype(BF16)
    x1 = out_proj(out_a, out_b, w_out[:ch], w_out[ch:], xt, tm=512)
    x2 = ffn(x1, row(e_norm_ffn[0]), e_ffn_w1[0].astype(BF16), e_ffn_w3[0].astype(BF16),
             e_ffn_w2[0].astype(BF16), tm=512, tf=512)

    aw = o_w_in.shape[2] - o_d_skip.shape[1]
    aw //= 3
    sch = o_d_skip.shape[1]
    z1 = norm_matmul(x2, row(o_norm_mix[0]), o_w_in[0].astype(BF16), tm=1024, tn=1024)
    out_c = dilated_attention(z1, batch=batch, seq=seq, width=aw)
    sp = s5_params(o_lam_re[0], o_lam_im[0], o_log_dt[0], o_b_re[0], o_b_im[0], o_c_re[0], o_c_im[0])
    out_d = s5(z1, *sp, row(o_d_skip[0]), o_w_glu[0].astype(BF16), row(o_b_glu[0]),
               batch=batch, seq=seq, col_blk=3 * aw // sch)
    w_out1 = o_w_out[0].astype(BF16)
    w_router = jnp.concatenate(
        [jnp.pad(p, ((0, 0), (0, LANES - N_EXPERTS))) for p in _split3(o_router[0])], axis=1)
    x3, xp3, eid, gates = out_proj_route(out_c, out_d, w_out1[:aw], w_out1[aw:], x2, row(o_norm_ffn[0]),
                                         w_router, tm=512)
    tile_e, tile_v, src, dst, n_out_rows = route_slots(eid[:, :2], dff=o_exp_w1.shape[3])
    ys = expert_ffn(tile_e, tile_v, src, dst, xp3, o_exp_w1[0], o_exp_w3[0], o_exp_w2[0], n_out_rows=n_out_rows)
    out = combine(x3, gates, row(final_norm), ys)
    return out.reshape(batch, seq, d)
```

```python
import functools
import math

import jax
import jax.numpy as jnp
from jax import lax
from jax.experimental import pallas as pl
from jax.experimental.pallas import tpu as pltpu

F32 = jnp.float32
BF16 = jnp.bfloat16

RMS_EPS = 1e-6
LN_EPS = 1e-5
CONV_WIDTH = 31
MLSTM_HEADS = 8
MLSTM_QK_CONV = 4
MLSTM_CHUNK = 128
ATTN_HEADS = 8
ATTN_BLOCK = 128
DILATIONS = (1, 4, 16)
SSM_GROUP = 16
SSM_STATE = 64
N_EXPERTS = 8
LANES = 128
SUBLANES = 8
VMEM_LIMIT = 56 * 1024 * 1024


def _cparams(*sem):
    return pltpu.CompilerParams(dimension_semantics=sem, vmem_limit_bytes=VMEM_LIMIT)


def _rms(x, g):
    return x * lax.rsqrt(jnp.mean(x * x, axis=-1, keepdims=True) + RMS_EPS) * g


def _norm_matmul_kernel(x_ref, g_ref, w_ref, o_ref, xn_ref):
    @pl.when(pl.program_id(1) == 0)
    def _():
        xn_ref[...] = _rms(x_ref[...], g_ref[...]).astype(BF16)

    o_ref[...] = jnp.dot(xn_ref[...], w_ref[...], preferred_element_type=F32).astype(o_ref.dtype)


def _norm_matmul_aux_kernel(x_ref, g_ref, w_ref, wa_ref, o_ref, oa_ref, xn_ref):
    @pl.when(pl.program_id(1) == 0)
    def _():
        xn = _rms(x_ref[...], g_ref[...]).astype(BF16)
        xn_ref[...] = xn
        oa_ref[...] = jnp.dot(xn, wa_ref[...], preferred_element_type=F32)

    o_ref[...] = jnp.dot(xn_ref[...], w_ref[...], preferred_element_type=F32).astype(o_ref.dtype)


def norm_matmul(x, g, w, w_aux=None, *, tm, tn, n_out=None):
    t, d = x.shape
    n = w.shape[1] if n_out is None else n_out
    grid = (t // tm, n // tn)
    x_spec = pl.BlockSpec((tm, d), lambda i, j: (i, 0))
    g_spec = pl.BlockSpec((1, d), lambda i, j: (0, 0))
    w_spec = pl.BlockSpec((d, tn), lambda i, j: (0, j))
    o_spec = pl.BlockSpec((tm, tn), lambda i, j: (i, j))
    scratch = [pltpu.VMEM((tm, d), BF16)]
    if w_aux is None:
        return pl.pallas_call(
            _norm_matmul_kernel, grid=grid, in_specs=[x_spec, g_spec, w_spec], out_specs=o_spec,
            out_shape=jax.ShapeDtypeStruct((t, n), BF16), scratch_shapes=scratch,
            compiler_params=_cparams("parallel", "arbitrary"), name="norm_matmul")(x, g, w)
    na = w_aux.shape[1]
    return pl.pallas_call(
        _norm_matmul_aux_kernel, grid=grid,
        in_specs=[x_spec, g_spec, w_spec, pl.BlockSpec((d, na), lambda i, j: (0, 0))],
        out_specs=[o_spec, pl.BlockSpec((tm, na), lambda i, j: (i, 0))],
        out_shape=[jax.ShapeDtypeStruct((t, n), BF16), jax.ShapeDtypeStruct((t, na), F32)],
        scratch_shapes=scratch, compiler_params=_cparams("parallel", "arbitrary"),
        name="norm_matmul_aux")(x, g, w, w_aux)


CONV_HALO = 32
CONV_ROWS = 64
NORM_ROWS = 32


def _conv_kernel(u_ref, halo_ref, w_ref, cb_ref, lg_ref, lb_ref, o_ref, buf_ref, xs_ref, acc_ref, *, ts, ch):
    first = pl.program_id(1) == 0

    def glu(u):
        u = u.astype(F32)
        return u[:, :ch] * jax.nn.sigmoid(u[:, ch:])

    buf_ref[0:CONV_HALO, :] = jnp.where(first, 0.0, glu(halo_ref[...]))
    buf_ref[CONV_HALO:CONV_HALO + ts, :] = glu(u_ref[...])
    n_shift = ts + CONV_HALO - SUBLANES
    for b in range(1, SUBLANES):
        xs_ref[b, 0:n_shift, :] = buf_ref[b:b + n_shift, :]

    lead = CONV_HALO - (CONV_WIDTH - 1)

    nsub = CONV_ROWS // SUBLANES
    for lc in range(ch // LANES):
        cols = slice(lc * LANES, (lc + 1) * LANES)
        taps = [jnp.broadcast_to(w_ref[k:k + 1, cols], (SUBLANES, LANES)) for k in range(CONV_WIDTH)]
        bias = jnp.broadcast_to(cb_ref[:, cols], (SUBLANES, LANES))

        def conv_step(r, carry, cols=cols, taps=taps, bias=bias):
            base = pl.multiple_of(r * CONV_ROWS, CONV_ROWS)
            acc = [bias] * nsub
            for k in range(CONV_WIDTH):
                a, b = divmod(lead + k, SUBLANES)
                for j in range(nsub):
                    rows = pl.ds(base + SUBLANES * (a + j), SUBLANES)
                    xk = buf_ref[rows, cols] if b == 0 else xs_ref[b, rows, cols]
                    acc[j] = acc[j] + taps[k] * xk
            for j in range(nsub):
                acc_ref[pl.ds(base + SUBLANES * j, SUBLANES), cols] = acc[j]
            return carry

        lax.fori_loop(0, ts // CONV_ROWS, conv_step, 0)

    def norm_step(r, carry):
        base = pl.multiple_of(r * NORM_ROWS, NORM_ROWS)
        a = acc_ref[pl.ds(base, NORM_ROWS), :]
        mu = jnp.mean(a, axis=-1, keepdims=True)
        d = a - mu
        var = jnp.mean(d * d, axis=-1, keepdims=True)
        an = d * lax.rsqrt(var + LN_EPS) * lg_ref[...] + lb_ref[...]
        o_ref[pl.ds(base, NORM_ROWS), :] = (an * jax.nn.sigmoid(an)).astype(o_ref.dtype)
        return carry

    lax.fori_loop(0, ts // NORM_ROWS, norm_step, 0, unroll=4)


def conformer_conv(z, conv_w, conv_b, ln_g, ln_b, *, batch, seq, ts):
    ch = conv_w.shape[1]
    nts = seq // ts
    halo_blocks = ts // CONV_HALO
    kern = functools.partial(_conv_kernel, ts=ts, ch=ch)
    vec = lambda: pl.BlockSpec((1, ch), lambda b, i: (0, 0))
    return pl.pallas_call(
        kern, grid=(batch, nts),
        in_specs=[
            pl.BlockSpec((ts, 2 * ch), lambda b, i: (b * nts + i, 0)),
            pl.BlockSpec((CONV_HALO, 2 * ch),
                         lambda b, i: (jnp.maximum((b * nts + i) * halo_blocks - 1, 0), 0)),
            pl.BlockSpec((CONV_WIDTH, ch), lambda b, i: (0, 0)),
            vec(), vec(), vec(),
        ],
        out_specs=pl.BlockSpec((ts, ch), lambda b, i: (b * nts + i, 0)),
        out_shape=jax.ShapeDtypeStruct((batch * seq, ch), BF16),
        scratch_shapes=[
            pltpu.VMEM((CONV_HALO + ts, ch), F32),
            pltpu.VMEM((SUBLANES, CONV_HALO + ts, ch), F32),
            pltpu.VMEM((ts, ch), F32),
        ],
        compiler_params=_cparams("parallel", "arbitrary"), name="conformer_conv",
    )(z, z, conv_w, conv_b, ln_g, ln_b)


QK_HALO = 16


def _mlstm_kernel(zqk_ref, halo_ref, zv_ref, zo_ref, g_ref, cw_ref, cb_ref, gb_ref, o_ref,
                  qb_ref, c_ref, n_ref, m_ref, *, nh, dh):
    L = MLSTM_CHUNK
    first = pl.program_id(1) == 0

    @pl.when(first)
    def _():
        c_ref[...] = jnp.zeros_like(c_ref)
        n_ref[...] = jnp.zeros_like(n_ref)
        m_ref[...] = jnp.zeros_like(m_ref)

    qb_ref[0:QK_HALO, :] = jnp.where(first, 0.0, halo_ref[...].astype(F32))
    qb_ref[QK_HALO:QK_HALO + L, :] = zqk_ref[...].astype(F32)
    lead = QK_HALO - (MLSTM_QK_CONV - 1)
    acc = jnp.broadcast_to(cb_ref[...], (L, 2 * nh * dh))
    for k in range(MLSTM_QK_CONV):
        acc = acc + cw_ref[k:k + 1, :] * qb_ref[lead + k:lead + k + L, :]
    qk = acc * jax.nn.sigmoid(acc)
    row = lax.broadcasted_iota(jnp.int32, (L, L), 0)
    col = lax.broadcasted_iota(jnp.int32, (L, L), 1)

    g = g_ref[...] + gb_ref[...]
    logf = jax.nn.log_sigmoid(g)
    causal = col <= row
    tri = causal.astype(F32)
    bcum = jnp.dot(tri, logf, preferred_element_type=F32, precision=lax.Precision.HIGHEST)
    g_t = g.T
    b_t = bcum.T
    scale = dh ** -0.5

    for h in range(nh):
        q = qk[:, h * dh:(h + 1) * dh].astype(BF16)
        kf = qk[:, (nh + h) * dh:(nh + h + 1) * dh] * scale
        k = kf.astype(BF16)
        v = zv_ref[:, h * dh:(h + 1) * dh]
        b_col = bcum[:, nh + h:nh + h + 1]
        b_row = b_t[nh + h:nh + h + 1, :]
        i_col = g[:, h:h + 1]
        i_row = g_t[h:h + 1, :]
        m_prev = m_ref[h, 0:1, 0:1]
        c_prev = c_ref[h]
        n_prev = n_ref[h, 0:1, :]

        log_d = jnp.where(causal, b_col - b_row + i_row, -jnp.inf)
        inter = b_col + m_prev
        m_t = jnp.maximum(inter, jnp.max(log_d, axis=-1, keepdims=True))
        s = lax.dot_general(q, k, (((1,), (1,)), ((), ())), preferred_element_type=F32)
        s = s * jnp.exp(log_d - m_t)
        w_int = jnp.exp(inter - m_t)
        num = (jnp.dot(s.astype(BF16), v, preferred_element_type=F32)
               + w_int * jnp.dot(q, c_prev.astype(BF16), preferred_element_type=F32))
        qn = jnp.sum(q.astype(F32) * n_prev, axis=-1, keepdims=True)
        den = jnp.sum(s, axis=-1, keepdims=True) + w_int * qn
        hval = num / jnp.maximum(jnp.abs(den), jnp.exp(-m_t))

        b_last = b_col[L - 1:L, :]
        gk = b_last - b_col + i_col
        m_new = jnp.maximum(b_last + m_prev, jnp.max(gk, axis=0, keepdims=True))
        w_k = jnp.exp(gk - m_new)
        decay = jnp.exp(b_last + m_prev - m_new)
        kw = kf * w_k
        c_ref[h] = decay * c_prev + lax.dot_general(
            kw.astype(BF16), v, (((0,), (0,)), ((), ())), preferred_element_type=F32)
        n_ref[h] = jnp.broadcast_to(decay * n_prev + jnp.sum(kw, axis=0, keepdims=True), (SUBLANES, dh))
        m_ref[h] = jnp.broadcast_to(m_new, (SUBLANES, LANES))

        gate_o = jax.nn.sigmoid(zo_ref[:, h * dh:(h + 1) * dh].astype(F32))
        o_ref[:, h * dh:(h + 1) * dh] = (gate_o * hval).astype(o_ref.dtype)


def mlstm(z, gates, qk_conv_w, qk_conv_b, gate_bias, *, batch, seq, col0):
    L = MLSTM_CHUNK
    nh = MLSTM_HEADS
    w = qk_conv_w.shape[1] // 2
    dh = w // nh
    nc = seq // L
    qk_blk = col0 // (2 * w)
    v_blk = (col0 + 2 * w) // w
    o_blk = v_blk + 1
    kern = functools.partial(_mlstm_kernel, nh=nh, dh=dh)
    return pl.pallas_call(
        kern, grid=(batch, nc),
        in_specs=[
            pl.BlockSpec((L, 2 * w), lambda b, c: (b * nc + c, qk_blk)),
            pl.BlockSpec((QK_HALO, 2 * w),
                         lambda b, c: (jnp.maximum((b * nc + c) * (L // QK_HALO) - 1, 0), qk_blk)),
            pl.BlockSpec((L, w), lambda b, c: (b * nc + c, v_blk)),
            pl.BlockSpec((L, w), lambda b, c: (b * nc + c, o_blk)),
            pl.BlockSpec((L, LANES), lambda b, c: (b * nc + c, 0)),
            pl.BlockSpec((MLSTM_QK_CONV, 2 * w), lambda b, c: (0, 0)),
            pl.BlockSpec((1, 2 * w), lambda b, c: (0, 0)),
            pl.BlockSpec((1, LANES), lambda b, c: (0, 0)),
        ],
        out_specs=pl.BlockSpec((L, w), lambda b, c: (b * nc + c, 0)),
        out_shape=jax.ShapeDtypeStruct((batch * seq, w), BF16),
        scratch_shapes=[
            pltpu.VMEM((QK_HALO + L, 2 * w), F32),
            pltpu.VMEM((nh, dh, dh), F32),
            pltpu.VMEM((nh, SUBLANES, dh), F32),
            pltpu.VMEM((nh, SUBLANES, LANES), F32),
        ],
        compiler_params=_cparams("parallel", "arbitrary"), name="mlstm",
    )(z, z, z, z, gates, qk_conv_w, qk_conv_b, gate_bias)


def _out_proj_kernel(a_ref, b_ref, wa_ref, wb_ref, r_ref, o_ref):
    o_ref[...] = (r_ref[...]
                  + jnp.dot(a_ref[...], wa_ref[...], preferred_element_type=F32)
                  + jnp.dot(b_ref[...], wb_ref[...], preferred_element_type=F32))


def out_proj(a, b, wa, wb, res, *, tm):
    t, d = res.shape
    ka, kb = a.shape[1], b.shape[1]
    return pl.pallas_call(
        _out_proj_kernel, grid=(t // tm,),
        in_specs=[
            pl.BlockSpec((tm, ka), lambda i: (i, 0)),
            pl.BlockSpec((tm, kb), lambda i: (i, 0)),
            pl.BlockSpec((ka, d), lambda i: (0, 0)),
            pl.BlockSpec((kb, d), lambda i: (0, 0)),
            pl.BlockSpec((tm, d), lambda i: (i, 0)),
        ],
        out_specs=pl.BlockSpec((tm, d), lambda i: (i, 0)),
        out_shape=jax.ShapeDtypeStruct((t, d), F32),
        compiler_params=_cparams("parallel"), name="out_proj",
    )(a, b, wa, wb, res)


def _ffn_kernel(x_ref, g_ref, w1_ref, w3_ref, w2_ref, o_ref, xn_ref):
    f = pl.program_id(1)

    @pl.when(f == 0)
    def _():
        x = x_ref[...]
        xn_ref[...] = _rms(x, g_ref[...]).astype(BF16)
        o_ref[...] = x

    xn = xn_ref[...]
    a = jnp.dot(xn, w1_ref[...], preferred_element_type=F32)
    b = jnp.dot(xn, w3_ref[...], preferred_element_type=F32)
    hid = (a * jax.nn.sigmoid(a) * b).astype(BF16)
    o_ref[...] += jnp.dot(hid, w2_ref[...], preferred_element_type=F32)


def ffn(x, g, w1, w3, w2, *, tm, tf):
    t, d = x.shape
    dff = w1.shape[1]
    return pl.pallas_call(
        _ffn_kernel, grid=(t // tm, dff // tf),
        in_specs=[
            pl.BlockSpec((tm, d), lambda i, f: (i, 0)),
            pl.BlockSpec((1, d), lambda i, f: (0, 0)),
            pl.BlockSpec((d, tf), lambda i, f: (0, f)),
            pl.BlockSpec((d, tf), lambda i, f: (0, f)),
            pl.BlockSpec((tf, d), lambda i, f: (f, 0)),
        ],
        out_specs=pl.BlockSpec((tm, d), lambda i, f: (i, 0)),
        out_shape=jax.ShapeDtypeStruct((t, d), F32),
        scratch_shapes=[pltpu.VMEM((tm, d), BF16)],
        compiler_params=_cparams("parallel", "arbitrary"), name="ffn",
    )(x, g, w1, w3, w2)


def _attn_kernel(q_ref, k_ref, v_ref, o_ref, qf_ref, kf_ref, vf_ref, qd_ref, kd_ref, vd_ref,
                 od_ref, ld_ref, on_ref, ln_ref, *, seq, dh):
    T = ATTN_BLOCK
    nblk = seq // T
    qf_ref[...] = q_ref[...].astype(F32) * (dh ** -0.5)
    kf_ref[...] = k_ref[...].astype(F32)
    vf_ref[...] = v_ref[...].astype(F32)
    kd_ref[0:T, :] = jnp.zeros((T, dh), BF16)
    vd_ref[0:T, :] = jnp.zeros((T, dh), BF16)

    qi = lax.broadcasted_iota(jnp.int32, (T, 2 * T), 0)
    ki = lax.broadcasted_iota(jnp.int32, (T, 2 * T), 1)
    dist = T + qi - ki
    band = (dist >= 0) & (dist <= T)

    for g, dil in enumerate(DILATIONS):
        ls = seq // dil
        nb = ls // T
        for r in range(dil):
            rows = pl.ds(r, ls, stride=dil) if dil > 1 else pl.ds(0, ls)
            qd_ref[r * ls:(r + 1) * ls, :] = qf_ref[rows, :].astype(BF16)
            kd_ref[T + r * ls:T + (r + 1) * ls, :] = kf_ref[rows, :].astype(BF16)
            vd_ref[T + r * ls:T + (r + 1) * ls, :] = vf_ref[rows, :].astype(BF16)

        def block(n, carry):
            base = pl.multiple_of(n * T, T)
            qb = qd_ref[pl.ds(base, T), :]
            kb = kd_ref[pl.ds(base, 2 * T), :]
            vb = vd_ref[pl.ds(base, 2 * T), :]
            s = lax.dot_general(qb, kb, (((1,), (1,)), ((), ())), preferred_element_type=F32)
            kmin = jnp.where(n % nb == 0, T, 0)
            s = jnp.where(band & (ki >= kmin), s, -jnp.inf)
            m = jnp.max(s, axis=-1, keepdims=True)
            p = jnp.exp(s - m)
            l = jnp.sum(p, axis=-1, keepdims=True)
            o = jnp.dot(p.astype(BF16), vb, preferred_element_type=F32) / l
            od_ref[pl.ds(base, T), :] = o
            ld_ref[pl.ds(base, T), :] = jnp.broadcast_to(m + jnp.log(l), (T, dh))
            return carry

        lax.fori_loop(0, nblk, block, 0, unroll=8)

        for r in range(dil):
            rows = pl.ds(r, ls, stride=dil) if dil > 1 else pl.ds(0, ls)
            on_ref[g, rows, :] = od_ref[r * ls:(r + 1) * ls, :]
            ln_ref[g, rows, :] = ld_ref[r * ls:(r + 1) * ls, :]

    def merge(n, carry):
        rows = pl.ds(pl.multiple_of(n * T, T), T)
        lses = [ln_ref[g, rows, :] for g in range(len(DILATIONS))]
        mx = functools.reduce(jnp.maximum, lses)
        ws = [jnp.exp(l - mx) for l in lses]
        tot = functools.reduce(lambda a, b: a + b, ws)
        acc = ws[0] * on_ref[0, rows, :]
        for g in range(1, len(DILATIONS)):
            acc = acc + ws[g] * on_ref[g, rows, :]
        o_ref[rows, :] = (acc / tot).astype(o_ref.dtype)
        return carry

    lax.fori_loop(0, nblk, merge, 0)


def dilated_attention(z, *, batch, seq, width):
    nh = ATTN_HEADS
    dh = width // nh
    ng = len(DILATIONS)
    kern = functools.partial(_attn_kernel, seq=seq, dh=dh)
    blk = lambda off: pl.BlockSpec((seq, dh), lambda b, h: (b, off + h))
    return pl.pallas_call(
        kern, grid=(batch, nh),
        in_specs=[blk(0), blk(nh), blk(2 * nh)],
        out_specs=pl.BlockSpec((seq, dh), lambda b, h: (b, h)),
        out_shape=jax.ShapeDtypeStruct((batch * seq, width), BF16),
        scratch_shapes=[
            pltpu.VMEM((seq, dh), F32), pltpu.VMEM((seq, dh), F32), pltpu.VMEM((seq, dh), F32),
            pltpu.VMEM((seq, dh), BF16),
            pltpu.VMEM((ATTN_BLOCK + seq, dh), BF16), pltpu.VMEM((ATTN_BLOCK + seq, dh), BF16),
            pltpu.VMEM((seq, dh), F32), pltpu.VMEM((seq, dh), F32),
            pltpu.VMEM((ng, seq, dh), F32), pltpu.VMEM((ng, seq, dh), F32),
        ],
        compiler_params=_cparams("parallel", "parallel"), name="dilated_attention",
    )(z, z, z)


S5_SEGS = SUBLANES
S5_SEG = 64
S5_PITCH = S5_SEG + 4
S5_TILE = S5_SEGS * S5_SEG
S5_ROWS = S5_SEGS * S5_PITCH
S5_UNROLL = S5_SEG
S5_CHUNK_IN = 256
S5_SLABS = S5_CHUNK_IN // SSM_GROUP * SSM_STATE // LANES


def _s5_kernel(u_ref, bre_ref, bim_ref, cre_ref, cim_ref, lre_ref, lim_ref, dsk_ref, wg_ref, bg_ref,
               o_ref, ug_ref, sre_ref, sim_ref, cyre_ref, cyim_ref, y_ref, *, nchunk):
    first = pl.program_id(1) == 0

    @pl.when(first)
    def _():
        cyre_ref[...] = jnp.zeros_like(cyre_ref)
        cyim_ref[...] = jnp.zeros_like(cyim_ref)
        ug_ref[...] = jnp.zeros_like(ug_ref)

    for i in range(S5_SEGS):
        ug_ref[i * S5_PITCH:i * S5_PITCH + S5_SEG, :] = u_ref[i * S5_SEG:(i + 1) * S5_SEG, :].astype(F32)

    seg_id = lax.broadcasted_iota(jnp.int32, (S5_SEGS, LANES), 0)
    for c in range(nchunk):
        par = (c % 2) * S5_SLABS
        ucols = slice(c * S5_CHUNK_IN, (c + 1) * S5_CHUNK_IN)
        ub = ug_ref[:, ucols].astype(BF16)
        bu_re = jnp.dot(ub, bre_ref[c], preferred_element_type=F32)
        bu_im = jnp.dot(ub, bim_ref[c], preferred_element_type=F32)
        for s in range(S5_SLABS):
            sre_ref[par + s] = bu_re[:, s * LANES:(s + 1) * LANES]
            sim_ref[par + s] = bu_im[:, s * LANES:(s + 1) * LANES]

        lam_re = [jnp.broadcast_to(lre_ref[c, :, s * LANES:(s + 1) * LANES], (S5_SEGS, LANES))
                  for s in range(S5_SLABS)]
        lam_im = [jnp.broadcast_to(lim_ref[c, :, s * LANES:(s + 1) * LANES], (S5_SEGS, LANES))
                  for s in range(S5_SLABS)]

        def step(j, st, store):
            rows = pl.ds(j, S5_SEGS, stride=S5_PITCH)
            new = []
            for s in range(S5_SLABS):
                pr, pi = st[2 * s], st[2 * s + 1]
                nr = lam_re[s] * pr - lam_im[s] * pi + sre_ref[par + s, rows, :]
                ni = lam_re[s] * pi + lam_im[s] * pr + sim_ref[par + s, rows, :]
                if store:
                    sre_ref[par + s, rows, :] = nr
                    sim_ref[par + s, rows, :] = ni
                new += [nr, ni]
            return tuple(new)

        zero = tuple(jnp.zeros((S5_SEGS, LANES), F32) for _ in range(2 * S5_SLABS))
        ends = lax.fori_loop(0, S5_SEG, lambda j, st: step(j, st, False), zero, unroll=S5_UNROLL)

        pw_re, pw_im = [l[0:1] for l in lam_re], [l[0:1] for l in lam_im]
        for _ in range(int(math.log2(S5_SEG))):
            pw_re, pw_im = ([a * a - b * b for a, b in zip(pw_re, pw_im)],
                            [2.0 * a * b for a, b in zip(pw_re, pw_im)])
        init = []
        for s in range(S5_SLABS):
            cols = slice(s * LANES, (s + 1) * LANES)
            cr, ci = cyre_ref[c, 0:1, cols], cyim_ref[c, 0:1, cols]
            in_re = jnp.zeros((S5_SEGS, LANES), F32)
            in_im = jnp.zeros((S5_SEGS, LANES), F32)
            for i in range(S5_SEGS):
                in_re = jnp.where(seg_id == i, cr, in_re)
                in_im = jnp.where(seg_id == i, ci, in_im)
                er, ei = ends[2 * s][i:i + 1], ends[2 * s + 1][i:i + 1]
                cr, ci = (pw_re[s] * cr - pw_im[s] * ci + er, pw_re[s] * ci + pw_im[s] * cr + ei)
            cyre_ref[c, :, cols] = jnp.broadcast_to(cr, (SUBLANES, LANES))
            cyim_ref[c, :, cols] = jnp.broadcast_to(ci, (SUBLANES, LANES))
            init += [in_re, in_im]

        lax.fori_loop(0, S5_SEG, lambda j, st: step(j, st, True), tuple(init), unroll=S5_UNROLL)

        st_re = jnp.concatenate([sre_ref[par + s] for s in range(S5_SLABS)], axis=1).astype(BF16)
        st_im = jnp.concatenate([sim_ref[par + s] for s in range(S5_SLABS)], axis=1).astype(BF16)
        y_ref[:, ucols] = (jnp.dot(st_re, cre_ref[c], preferred_element_type=F32)
                           + jnp.dot(st_im, cim_ref[c], preferred_element_type=F32))

    for i in range(S5_SEGS):
        rows = slice(i * S5_PITCH, i * S5_PITCH + S5_SEG)
        y = y_ref[rows, :] + dsk_ref[...] * ug_ref[rows, :]
        y = jax.nn.gelu(y)
        gate = jnp.dot(y.astype(BF16), wg_ref[...], preferred_element_type=F32) + bg_ref[...]
        o_ref[i * S5_SEG:(i + 1) * S5_SEG, :] = (y * jax.nn.sigmoid(gate)).astype(o_ref.dtype)


def s5(z, b_re, b_im, c_re, c_im, lam_re, lam_im, d_skip, w_glu, b_glu, *, batch, seq, col_blk):
    nchunk, _, nstate = b_re.shape
    ch = nchunk * S5_CHUNK_IN
    nt = seq // S5_TILE
    kern = functools.partial(_s5_kernel, nchunk=nchunk)
    full = lambda shape: pl.BlockSpec(shape, lambda b, i: (0,) * len(shape))
    return pl.pallas_call(
        kern, grid=(batch, nt),
        in_specs=[
            pl.BlockSpec((S5_TILE, ch), lambda b, i: (b * nt + i, col_blk)),
            full(b_re.shape), full(b_im.shape), full(c_re.shape), full(c_im.shape),
            full(lam_re.shape), full(lam_im.shape), full((1, ch)), full((ch, ch)), full((1, ch)),
        ],
        out_specs=pl.BlockSpec((S5_TILE, ch), lambda b, i: (b * nt + i, 0)),
        out_shape=jax.ShapeDtypeStruct((batch * seq, ch), BF16),
        scratch_shapes=[
            pltpu.VMEM((S5_ROWS, ch), F32),
            pltpu.VMEM((2 * S5_SLABS, S5_ROWS, LANES), F32),
            pltpu.VMEM((2 * S5_SLABS, S5_ROWS, LANES), F32),
            pltpu.VMEM((nchunk, SUBLANES, nstate), F32),
            pltpu.VMEM((nchunk, SUBLANES, nstate), F32),
            pltpu.VMEM((S5_ROWS, ch), F32),
        ],
        compiler_params=_cparams("parallel", "arbitrary"), name="s5",
    )(z, b_re, b_im, c_re, c_im, lam_re, lam_im, d_skip, w_glu, b_glu)


def s5_params(lam_re, lam_im, log_dt, b_re, b_im, c_re, c_im):
    ng, ns = lam_re.shape
    gpc = S5_CHUNK_IN // SSM_GROUP
    nchunk = ng // gpc
    dt = jnp.exp(log_dt)[:, None]
    mag = jnp.exp(lam_re * dt)
    lb_re, lb_im = mag * jnp.cos(lam_im * dt), mag * jnp.sin(lam_im * dt)
    den = lam_re * lam_re + lam_im * lam_im
    f_re = ((lb_re - 1.0) * lam_re + lb_im * lam_im) / den
    f_im = (lb_im * lam_re - (lb_re - 1.0) * lam_im) / den
    bb_re = f_re[..., None] * b_re - f_im[..., None] * b_im
    bb_im = f_re[..., None] * b_im + f_im[..., None] * b_re
    eye = jnp.eye(gpc, dtype=F32)

    def pack_b(m):
        m = m.reshape(nchunk, gpc, ns, SSM_GROUP)
        return jnp.einsum('cgpk,gh->cgkhp', m, eye).reshape(nchunk, gpc * SSM_GROUP, gpc * ns).astype(BF16)

    def pack_c(m):
        m = m.reshape(nchunk, gpc, SSM_GROUP, ns)
        return jnp.einsum('cgkp,gh->cgphk', m, eye).reshape(nchunk, gpc * ns, gpc * SSM_GROUP).astype(BF16)

    return (pack_b(bb_re), pack_b(bb_im), pack_c(c_re), pack_c(-c_im),
            lb_re.reshape(nchunk, 1, gpc * ns), lb_im.reshape(nchunk, 1, gpc * ns))


def _pack_bf16_pair(lo, hi):
    lo_bits = lax.bitcast_convert_type(lo.astype(BF16).astype(F32), jnp.uint32) >> 16
    hi_bits = lax.bitcast_convert_type(hi.astype(BF16).astype(F32), jnp.uint32) & jnp.uint32(0xFFFF0000)
    return lo_bits | hi_bits


def _unpack_bf16_pair(w):
    lo = lax.bitcast_convert_type(w << 16, F32)
    hi = lax.bitcast_convert_type(w & jnp.uint32(0xFFFF0000), F32)
    return lo, hi


def _split3(v):
    hi = v.astype(BF16)
    r1 = v - hi.astype(F32)
    mid = r1.astype(BF16)
    lo = (r1 - mid.astype(F32)).astype(BF16)
    return hi, mid, lo


def _out_proj_route_kernel(a_ref, b_ref, wa_ref, wb_ref, r_ref, g_ref, wr_ref, x_ref, xp_ref, eid_ref, gate_ref):
    x = (r_ref[...]
         + jnp.dot(a_ref[...], wa_ref[...], preferred_element_type=F32)
         + jnp.dot(b_ref[...], wb_ref[...], preferred_element_type=F32))
    x_ref[...] = x
    xn = _rms(x, g_ref[...])
    half = xn.shape[1] // 2
    xp_ref[...] = _pack_bf16_pair(xn[:, :half], xn[:, half:])
    xh, xm, xl = _split3(xn)
    ph = jnp.dot(xh, wr_ref[...], preferred_element_type=F32)
    pm = jnp.dot(xm, wr_ref[:, :2 * LANES], preferred_element_type=F32)
    pl_ = jnp.dot(xl, wr_ref[:, :LANES], preferred_element_type=F32)
    logits = (ph[:, :LANES] + (ph[:, LANES:2 * LANES] + pm[:, :LANES])
              + (ph[:, 2 * LANES:] + pm[:, LANES:] + pl_))
    lane = lax.broadcasted_iota(jnp.int32, logits.shape, 1)
    logits = jnp.where(lane < N_EXPERTS, logits, -jnp.inf)
    v1 = jnp.max(logits, axis=-1, keepdims=True)
    i1 = jnp.min(jnp.where(logits == v1, lane, LANES), axis=-1, keepdims=True)
    rest = jnp.where(lane == i1, -jnp.inf, logits)
    v2 = jnp.max(rest, axis=-1, keepdims=True)
    i2 = jnp.min(jnp.where(rest == v2, lane, LANES), axis=-1, keepdims=True)
    e2 = jnp.exp(v2 - v1)
    g1 = 1.0 / (1.0 + e2)
    g2 = e2 / (1.0 + e2)
    eid_ref[...] = jnp.where(lane == 0, i1, jnp.where(lane == 1, i2, 0))
    gate_ref[...] = jnp.where(lane == 0, g1, jnp.where(lane == 1, g2, 0.0))


def out_proj_route(a, b, wa, wb, res, g, w_router, *, tm):
    t, d = res.shape
    ka, kb = a.shape[1], b.shape[1]
    row = lambda n: pl.BlockSpec((tm, n), lambda i: (i, 0))
    full = lambda r, c: pl.BlockSpec((r, c), lambda i: (0, 0))
    return pl.pallas_call(
        _out_proj_route_kernel, grid=(t // tm,),
        in_specs=[row(ka), row(kb), full(ka, d), full(kb, d), row(d), full(1, d), full(d, 3 * LANES)],
        out_specs=[row(d), row(d // 2), row(LANES), row(LANES)],
        out_shape=[jax.ShapeDtypeStruct((t, d), F32), jax.ShapeDtypeStruct((t, d // 2), jnp.uint32),
                   jax.ShapeDtypeStruct((t, LANES), jnp.int32), jax.ShapeDtypeStruct((t, LANES), F32)],
        compiler_params=_cparams("parallel"), name="out_proj_route",
    )(a, b, wa, wb, res, g, w_router)


EXPERT_TM = 1024
EXPERT_TF = 512
COMBINE_TC = 512


def _gather_rows_per_step(tm, nf):
    return -(-tm // ((nf - 1) * SUBLANES)) * SUBLANES


def _expert_kernel(te_ref, tv_ref, src_ref, dst_ref, xp_hbm, w1_ref, w3_ref, w2_ref, ys_hbm,
                   gbuf, xb, acc, obuf, gsem, ssem, *, tm, nf, gr):
    i = pl.program_id(0)
    f = pl.program_id(1)
    ntiles = pl.num_programs(0)
    gb = gr * (nf - 1)
    half = xb.shape[1] // 2

    def gather(tile, r):
        return pltpu.make_async_copy(xp_hbm.at[pl.ds(src_ref[tile * gb + r], 1), :],
                                     gbuf.at[pl.ds(r, 1), :], gsem)

    def scatter(tile, r):
        return pltpu.make_async_copy(obuf.at[pl.ds(r, 1), :],
                                     ys_hbm.at[pl.ds(dst_ref[(tile + 1) * gb + r], 1), :], ssem)

    def for_rows(n, fn):
        def body(r, carry):
            fn(r)
            return carry
        lax.fori_loop(0, n, body, 0, unroll=8)

    @pl.when((i == 0) & (f == 0))
    def _():
        obuf[...] = jnp.zeros_like(obuf)
        for_rows(gb, lambda r: gather(0, r).start())

    @pl.when(f == 0)
    def _():
        for_rows(gb, lambda r: gather(i, r).wait())
        lo, hi = _unpack_bf16_pair(gbuf[0:tm, :])
        xb[:, :half] = lo.astype(BF16)
        xb[:, half:] = hi.astype(BF16)
        acc[...] = jnp.zeros_like(acc)

    def swiglu_rows(rows):
        x = xb[0:rows, :]
        a = jnp.dot(x, w1_ref[0].astype(BF16), preferred_element_type=F32)
        b = jnp.dot(x, w3_ref[0].astype(BF16), preferred_element_type=F32)
        hid = (a * jax.nn.sigmoid(a) * b).astype(BF16)
        acc[0:rows, :] += jnp.dot(hid, w2_ref[0].astype(BF16), preferred_element_type=F32)

    def step(rows, moving):
        if moving:
            for r in range(gr):
                gather(i + 1, f * gr + r).start()
                scatter(i - 1, f * gr + r).start()
        if rows:
            swiglu_rows(rows)

    used = tv_ref[i]
    is_last = f == nf - 1
    quarter = tm // 4
    conds = [(used > q * quarter) & (used <= (q + 1) * quarter) for q in range(4)] + [used == 0]
    for cond, rows in zip(conds, (quarter, 2 * quarter, 3 * quarter, tm, 0)):
        pl.when(cond & jnp.logical_not(is_last))(functools.partial(step, rows, True))
        pl.when(cond & is_last)(functools.partial(step, rows, False))

    @pl.when(is_last)
    def _():
        for_rows(gb, lambda r: scatter(i - 1, r).wait())
        y = acc[...]
        obuf[0:tm, :] = _pack_bf16_pair(y[:, :half], y[:, half:])

    @pl.when((i == ntiles - 1) & is_last)
    def _():
        for_rows(gb, lambda r: gather(i + 1, r).wait())
        for_rows(gb, lambda r: scatter(i, r).start())
        for_rows(gb, lambda r: scatter(i, r).wait())


def expert_ffn(tile_e, tile_v, src, dst, xp, w1, w3, w2, *, n_out_rows):
    tm, tf = EXPERT_TM, EXPERT_TF
    dh = xp.shape[1]
    d = 2 * dh
    dff = w1.shape[2]
    nf = dff // tf
    gr = _gather_rows_per_step(tm, nf)
    ntiles = tile_e.shape[0]
    fsel = lambda i, f, te, tv: jnp.where(tv[i] > 0, f, nf - 1)
    kern = functools.partial(_expert_kernel, tm=tm, nf=nf, gr=gr)
    return pl.pallas_call(
        kern,
        grid_spec=pltpu.PrefetchScalarGridSpec(
            num_scalar_prefetch=4, grid=(ntiles, nf),
            in_specs=[
                pl.BlockSpec(memory_space=pl.ANY),
                pl.BlockSpec((1, d, tf), lambda i, f, te, tv, s, t: (te[i], 0, fsel(i, f, te, tv))),
                pl.BlockSpec((1, d, tf), lambda i, f, te, tv, s, t: (te[i], 0, fsel(i, f, te, tv))),
                pl.BlockSpec((1, tf, d), lambda i, f, te, tv, s, t: (te[i], fsel(i, f, te, tv), 0)),
            ],
            out_specs=pl.BlockSpec(memory_space=pl.ANY),
            scratch_shapes=[
                pltpu.VMEM((gr * (nf - 1), dh), jnp.uint32),
                pltpu.VMEM((tm, d), BF16),
                pltpu.VMEM((tm, d), F32),
                pltpu.VMEM((gr * (nf - 1), dh), jnp.uint32),
                pltpu.SemaphoreType.DMA(()),
                pltpu.SemaphoreType.DMA(()),
            ]),
        out_shape=jax.ShapeDtypeStruct((n_out_rows, dh), jnp.uint32),
        compiler_params=_cparams("arbitrary", "arbitrary"), name="expert_ffn",
    )(tile_e, tile_v, src, dst, xp, w1, w3, w2)


def _combine_kernel(x_ref, gate_ref, g_ref, y1_ref, y2_ref, o_ref):
    gates = gate_ref[...]
    half = x_ref.shape[1] // 2
    lo1, hi1 = _unpack_bf16_pair(y1_ref[...])
    lo2, hi2 = _unpack_bf16_pair(y2_ref[...])
    g1, g2 = gates[:, 0:1], gates[:, 1:2]
    x = x_ref[...]
    x = jnp.concatenate([x[:, :half] + g1 * lo1 + g2 * lo2, x[:, half:] + g1 * hi1 + g2 * hi2], axis=1)
    o_ref[...] = _rms(x, g_ref[...])


def combine(x, gates, g, ys):
    t, d = x.shape
    tc = COMBINE_TC
    return pl.pallas_call(
        _combine_kernel, grid=(t // tc,),
        in_specs=[
            pl.BlockSpec((tc, d), lambda i: (i, 0)),
            pl.BlockSpec((tc, LANES), lambda i: (i, 0)),
            pl.BlockSpec((1, d), lambda i: (0, 0)),
            pl.BlockSpec((tc, d // 2), lambda i: (i, 0)),
            pl.BlockSpec((tc, d // 2), lambda i: (t // tc + i, 0)),
        ],
        out_specs=pl.BlockSpec((tc, d), lambda i: (i, 0)),
        out_shape=jax.ShapeDtypeStruct((t, d), F32),
        compiler_params=_cparams("parallel"), name="moe_combine",
    )(x, gates, g, ys, ys)


def route_slots(eid, *, dff):
    tm = EXPERT_TM
    nf = dff // EXPERT_TF
    gb = _gather_rows_per_step(tm, nf) * (nf - 1)
    t = eid.shape[0]
    na = 2 * t
    e_flat = eid.reshape(na)
    onehot = (e_flat[:, None] == jnp.arange(N_EXPERTS, dtype=jnp.int32)[None, :]).astype(jnp.int32)
    csum = jnp.cumsum(onehot, axis=0)
    rank = jnp.sum((csum - onehot) * onehot, axis=1)
    counts = csum[-1]
    padded = (counts + tm - 1) // tm * tm
    ends = jnp.cumsum(padded)
    slot = jnp.sum(onehot * (ends - padded)[None, :], axis=1) + rank
    ns = na + N_EXPERTS * tm
    ntiles = ns // tm
    asg = jnp.full((ns,), -1, jnp.int32).at[slot].set(jnp.arange(na, dtype=jnp.int32))
    is_pad = asg < 0
    tok = jnp.where(is_pad, 0, asg // 2)
    pad_rank = jnp.cumsum(is_pad.astype(jnp.int32)) - 1
    dst = jnp.where(is_pad, na + gb + pad_rank, (asg % 2) * t + asg // 2).reshape(ntiles, tm)
    over = (na + gb + (ns - na) + jnp.arange(ntiles * (gb - tm), dtype=jnp.int32)).reshape(ntiles, gb - tm)
    dst = jnp.concatenate([na + jnp.arange(gb, dtype=jnp.int32),
                           jnp.concatenate([dst, over], axis=1).reshape(-1)])
    src = jnp.pad(tok.reshape(ntiles, tm), ((0, 1), (0, gb - tm))).reshape(-1)
    starts = jnp.arange(ntiles, dtype=jnp.int32) * tm
    tile_e = jnp.sum((starts[:, None] >= ends[None, :]).astype(jnp.int32), axis=1)
    used_end = jnp.concatenate([ends - padded + counts, jnp.zeros((1,), jnp.int32)])
    tile_v = jnp.clip(used_end[tile_e] - starts, 0, tm)
    last_e = jnp.max(jnp.where(tile_v > 0, tile_e, 0))
    tile_e = jnp.where(tile_v > 0, tile_e, last_e)
    n_out_rows = na + gb + (ns - na) + ntiles * (gb - tm)
    n_out_rows = -(-n_out_rows // COMBINE_TC) * COMBINE_TC
    return tile_e, tile_v, src, dst, n_out_rows


def kernel(x, e_norm_mix, e_w_in, e_conv_w, e_conv_b, e_ln_g, e_ln_b, e_qk_conv_w, e_qk_conv_b, e_b_i, e_b_f, e_w_out, e_norm_ffn, e_ffn_w1, e_ffn_w3, e_ffn_w2, o_norm_mix, o_w_in, o_lam_re, o_lam_im, o_log_dt, o_b_re, o_b_im, o_c_re, o_c_im, o_d_skip, o_w_glu, o_b_glu, o_w_out, o_norm_ffn, o_router, o_exp_w1, o_exp_w3, o_exp_w2, final_norm):
    batch, seq, d = x.shape
    t = batch * seq
    xt = x.reshape(t, d)
    row = lambda v: v.reshape(1, -1)
    assert e_norm_mix.shape[0] == 1 and o_norm_mix.shape[0] == 1, "one even and one odd layer"

    ch = e_conv_w.shape[2]
    mw = e_qk_conv_w.shape[2] // 2
    nh = MLSTM_HEADS
    main_cols = 2 * ch + 4 * mw
    w_in = e_w_in[0].astype(BF16)
    w_gates = jnp.pad(w_in[:, main_cols:], ((0, 0), (0, LANES - 2 * nh)))
    z0, gates0 = norm_matmul(xt, row(e_norm_mix[0]), w_in, w_gates, tm=1024, tn=2048, n_out=main_cols)
    out_a = conformer_conv(z0, e_conv_w[0], row(e_conv_b[0]), row(e_ln_g[0]), row(e_ln_b[0]),
                           batch=batch, seq=seq, ts=512)
    gate_bias = jnp.pad(jnp.concatenate([e_b_i[0], e_b_f[0]]), (0, LANES - 2 * nh)).reshape(1, LANES)
    out_b = mlstm(z0, gates0, e_qk_conv_w[0], row(e_qk_conv_b[0]), gate_bias, batch=batch, seq=seq, col0=2 * ch)
    w_out = e_w_out[0].astype(BF16)
    x1 = out_proj(out_a, out_b, w_out[:ch], w_out[ch:], xt, tm=512)
    x2 = ffn(x1, row(e_norm_ffn[0]), e_ffn_w1[0].astype(BF16), e_ffn_w3[0].astype(BF16),
             e_ffn_w2[0].astype(BF16), tm=1024, tf=512)

    aw = o_w_in.shape[2] - o_d_skip.shape[1]
    aw //= 3
    sch = o_d_skip.shape[1]
    z1 = norm_matmul(x2, row(o_norm_mix[0]), o_w_in[0].astype(BF16), tm=1024, tn=2048)
    out_c = dilated_attention(z1, batch=batch, seq=seq, width=aw)
    sp = s5_params(o_lam_re[0], o_lam_im[0], o_log_dt[0], o_b_re[0], o_b_im[0], o_c_re[0], o_c_im[0])
    out_d = s5(z1, *sp, row(o_d_skip[0]), o_w_glu[0].astype(BF16), row(o_b_glu[0]),
               batch=batch, seq=seq, col_blk=3 * aw // sch)
    w_out1 = o_w_out[0].astype(BF16)
    w_router = jnp.concatenate(
        [jnp.pad(p, ((0, 0), (0, LANES - N_EXPERTS))) for p in _split3(o_router[0])], axis=1)
    x3, xp3, eid, gates = out_proj_route(out_c, out_d, w_out1[:aw], w_out1[aw:], x2, row(o_norm_ffn[0]),
                                         w_router, tm=512)
    tile_e, tile_v, src, dst, n_out_rows = route_slots(eid[:, :2], dff=o_exp_w1.shape[3])
    ys = expert_ffn(tile_e, tile_v, src, dst, xp3, o_exp_w1[0], o_exp_w3[0], o_exp_w2[0], n_out_rows=n_out_rows)
    out = combine(x3, gates, row(final_norm), ys)
    return out.reshape(batch, seq, d)
```

```python
import functools
import math

import jax
import jax.numpy as jnp
from jax import lax
from jax.experimental import pallas as pl
from jax.experimental.pallas import tpu as pltpu

F32 = jnp.float32
BF16 = jnp.bfloat16

RMS_EPS = 1e-6
LN_EPS = 1e-5
CONV_WIDTH = 31
MLSTM_HEADS = 8
MLSTM_QK_CONV = 4
MLSTM_CHUNK = 128
ATTN_HEADS = 8
ATTN_BLOCK = 128
DILATIONS = (1, 4, 16)
SSM_GROUP = 16
SSM_STATE = 64
N_EXPERTS = 8
LANES = 128
SUBLANES = 8
VMEM_LIMIT = 56 * 1024 * 1024


def _cparams(*sem):
    return pltpu.CompilerParams(dimension_semantics=sem, vmem_limit_bytes=VMEM_LIMIT)


def _rms(x, g):
    return x * lax.rsqrt(jnp.mean(x * x, axis=-1, keepdims=True) + RMS_EPS) * g


def _norm_matmul_kernel(x_ref, g_ref, w_ref, o_ref, *, tn):
    xn = _rms(x_ref[...], g_ref[...]).astype(BF16)
    for j in range(o_ref.shape[1] // tn):
        cols = slice(j * tn, (j + 1) * tn)
        o_ref[:, cols] = jnp.dot(xn, w_ref[:, cols], preferred_element_type=F32).astype(o_ref.dtype)


def _norm_matmul_aux_kernel(x_ref, g_ref, w_ref, wa_ref, o_ref, oa_ref, *, tn):
    xn = _rms(x_ref[...], g_ref[...]).astype(BF16)
    oa_ref[...] = jnp.dot(xn, wa_ref[...], preferred_element_type=F32)
    for j in range(o_ref.shape[1] // tn):
        cols = slice(j * tn, (j + 1) * tn)
        o_ref[:, cols] = jnp.dot(xn, w_ref[:, cols], preferred_element_type=F32).astype(o_ref.dtype)


def norm_matmul(x, g, w, w_aux=None, *, tm, tn, n_out=None):
    t, d = x.shape
    n = w.shape[1] if n_out is None else n_out
    resident = lambda shape: pl.BlockSpec(shape, lambda i: (0, 0), pipeline_mode=pl.Buffered(1))
    x_spec = pl.BlockSpec((tm, d), lambda i: (i, 0))
    g_spec = pl.BlockSpec((1, d), lambda i: (0, 0))
    o_spec = pl.BlockSpec((tm, n), lambda i: (i, 0))
    if w_aux is None:
        return pl.pallas_call(
            functools.partial(_norm_matmul_kernel, tn=tn), grid=(t // tm,),
            in_specs=[x_spec, g_spec, resident(w.shape)], out_specs=o_spec,
            out_shape=jax.ShapeDtypeStruct((t, n), BF16),
            compiler_params=_cparams("parallel"), name="norm_matmul")(x, g, w)
    na = w_aux.shape[1]
    return pl.pallas_call(
        functools.partial(_norm_matmul_aux_kernel, tn=tn), grid=(t // tm,),
        in_specs=[x_spec, g_spec, resident(w.shape), resident(w_aux.shape)],
        out_specs=[o_spec, pl.BlockSpec((tm, na), lambda i: (i, 0))],
        out_shape=[jax.ShapeDtypeStruct((t, n), BF16), jax.ShapeDtypeStruct((t, na), F32)],
        compiler_params=_cparams("parallel"), name="norm_matmul_aux")(x, g, w, w_aux)


CONV_HALO = 32
CONV_ROWS = 64
NORM_ROWS = 32


def _conv_kernel(u_ref, halo_ref, w_ref, cb_ref, lg_ref, lb_ref, o_ref, buf_ref, xs_ref, acc_ref, *, ts, ch):
    first = pl.program_id(1) == 0

    def glu(u):
        u = u.astype(F32)
        return u[:, :ch] * jax.nn.sigmoid(u[:, ch:])

    buf_ref[0:CONV_HALO, :] = jnp.where(first, 0.0, glu(halo_ref[...]))
    buf_ref[CONV_HALO:CONV_HALO + ts, :] = glu(u_ref[...])
    n_shift = ts + CONV_HALO - SUBLANES
    for b in range(1, SUBLANES):
        xs_ref[b, 0:n_shift, :] = buf_ref[b:b + n_shift, :]

    lead = CONV_HALO - (CONV_WIDTH - 1)

    nsub = CONV_ROWS // SUBLANES
    for lc in range(ch // LANES):
        cols = slice(lc * LANES, (lc + 1) * LANES)
        taps = [jnp.broadcast_to(w_ref[k:k + 1, cols], (SUBLANES, LANES)) for k in range(CONV_WIDTH)]
        bias = jnp.broadcast_to(cb_ref[:, cols], (SUBLANES, LANES))

        def conv_step(r, carry, cols=cols, taps=taps, bias=bias):
            base = pl.multiple_of(r * CONV_ROWS, CONV_ROWS)
            acc = [bias] * nsub
            for k in range(CONV_WIDTH):
                a, b = divmod(lead + k, SUBLANES)
                for j in range(nsub):
                    rows = pl.ds(base + SUBLANES * (a + j), SUBLANES)
                    xk = buf_ref[rows, cols] if b == 0 else xs_ref[b, rows, cols]
                    acc[j] = acc[j] + taps[k] * xk
            for j in range(nsub):
                acc_ref[pl.ds(base + SUBLANES * j, SUBLANES), cols] = acc[j]
            return carry

        lax.fori_loop(0, ts // CONV_ROWS, conv_step, 0)

    def norm_step(r, carry):
        base = pl.multiple_of(r * NORM_ROWS, NORM_ROWS)
        a = acc_ref[pl.ds(base, NORM_ROWS), :]
        mu = jnp.mean(a, axis=-1, keepdims=True)
        d = a - mu
        var = jnp.mean(d * d, axis=-1, keepdims=True)
        an = d * lax.rsqrt(var + LN_EPS) * lg_ref[...] + lb_ref[...]
        o_ref[pl.ds(base, NORM_ROWS), :] = (an * jax.nn.sigmoid(an)).astype(o_ref.dtype)
        return carry

    lax.fori_loop(0, ts // NORM_ROWS, norm_step, 0, unroll=4)


def conformer_conv(z, conv_w, conv_b, ln_g, ln_b, *, batch, seq, ts):
    ch = conv_w.shape[1]
    nts = seq // ts
    halo_blocks = ts // CONV_HALO
    kern = functools.partial(_conv_kernel, ts=ts, ch=ch)
    vec = lambda: pl.BlockSpec((1, ch), lambda b, i: (0, 0))
    return pl.pallas_call(
        kern, grid=(batch, nts),
        in_specs=[
            pl.BlockSpec((ts, 2 * ch), lambda b, i: (b * nts + i, 0)),
            pl.BlockSpec((CONV_HALO, 2 * ch),
                         lambda b, i: (jnp.maximum((b * nts + i) * halo_blocks - 1, 0), 0)),
            pl.BlockSpec((CONV_WIDTH, ch), lambda b, i: (0, 0)),
            vec(), vec(), vec(),
        ],
        out_specs=pl.BlockSpec((ts, ch), lambda b, i: (b * nts + i, 0)),
        out_shape=jax.ShapeDtypeStruct((batch * seq, ch), BF16),
        scratch_shapes=[
            pltpu.VMEM((CONV_HALO + ts, ch), F32),
            pltpu.VMEM((SUBLANES, CONV_HALO + ts, ch), F32),
            pltpu.VMEM((ts, ch), F32),
        ],
        compiler_params=_cparams("parallel", "arbitrary"), name="conformer_conv",
    )(z, z, conv_w, conv_b, ln_g, ln_b)


QK_HALO = 16


def _mlstm_kernel(zqk_ref, halo_ref, zv_ref, zo_ref, g_ref, cw_ref, cb_ref, gb_ref, o_ref,
                  qb_ref, c_ref, n_ref, m_ref, *, nh, dh):
    L = MLSTM_CHUNK
    first = pl.program_id(1) == 0

    @pl.when(first)
    def _():
        c_ref[...] = jnp.zeros_like(c_ref)
        n_ref[...] = jnp.zeros_like(n_ref)
        m_ref[...] = jnp.zeros_like(m_ref)

    qb_ref[0:QK_HALO, :] = jnp.where(first, 0.0, halo_ref[...].astype(F32))
    qb_ref[QK_HALO:QK_HALO + L, :] = zqk_ref[...].astype(F32)
    lead = QK_HALO - (MLSTM_QK_CONV - 1)
    acc = jnp.broadcast_to(cb_ref[...], (L, 2 * nh * dh))
    for k in range(MLSTM_QK_CONV):
        acc = acc + cw_ref[k:k + 1, :] * qb_ref[lead + k:lead + k + L, :]
    qk = acc * jax.nn.sigmoid(acc)
    row = lax.broadcasted_iota(jnp.int32, (L, L), 0)
    col = lax.broadcasted_iota(jnp.int32, (L, L), 1)

    g = g_ref[...] + gb_ref[...]
    logf = jax.nn.log_sigmoid(g)
    causal = col <= row
    tri = causal.astype(F32)
    bcum = jnp.dot(tri, logf, preferred_element_type=F32, precision=lax.Precision.HIGHEST)
    g_t = g.T
    b_t = bcum.T
    scale = dh ** -0.5

    for h in range(nh):
        q = qk[:, h * dh:(h + 1) * dh].astype(BF16)
        kf = qk[:, (nh + h) * dh:(nh + h + 1) * dh] * scale
        k = kf.astype(BF16)
        v = zv_ref[:, h * dh:(h + 1) * dh]
        b_col = bcum[:, nh + h:nh + h + 1]
        b_row = b_t[nh + h:nh + h + 1, :]
        i_col = g[:, h:h + 1]
        i_row = g_t[h:h + 1, :]
        m_prev = m_ref[h, 0:1, 0:1]
        c_prev = c_ref[h]
        n_prev = n_ref[h, 0:1, :]

        log_d = jnp.where(causal, b_col - b_row + i_row, -jnp.inf)
        inter = b_col + m_prev
        m_t = jnp.maximum(inter, jnp.max(log_d, axis=-1, keepdims=True))
        s = lax.dot_general(q, k, (((1,), (1,)), ((), ())), preferred_element_type=F32)
        s = s * jnp.exp(log_d - m_t)
        w_int = jnp.exp(inter - m_t)
        num = (jnp.dot(s.astype(BF16), v, preferred_element_type=F32)
               + w_int * jnp.dot(q, c_prev.astype(BF16), preferred_element_type=F32))
        qn = jnp.sum(q.astype(F32) * n_prev, axis=-1, keepdims=True)
        den = jnp.sum(s, axis=-1, keepdims=True) + w_int * qn
        hval = num / jnp.maximum(jnp.abs(den), jnp.exp(-m_t))

        b_last = b_col[L - 1:L, :]
        gk = b_last - b_col + i_col
        m_new = jnp.maximum(b_last + m_prev, jnp.max(gk, axis=0, keepdims=True))
        w_k = jnp.exp(gk - m_new)
        decay = jnp.exp(b_last + m_prev - m_new)
        kw = kf * w_k
        c_ref[h] = decay * c_prev + lax.dot_general(
            kw.astype(BF16), v, (((0,), (0,)), ((), ())), preferred_element_type=F32)
        n_ref[h] = jnp.broadcast_to(decay * n_prev + jnp.sum(kw, axis=0, keepdims=True), (SUBLANES, dh))
        m_ref[h] = jnp.broadcast_to(m_new, (SUBLANES, LANES))

        gate_o = jax.nn.sigmoid(zo_ref[:, h * dh:(h + 1) * dh].astype(F32))
        o_ref[:, h * dh:(h + 1) * dh] = (gate_o * hval).astype(o_ref.dtype)


def mlstm(z, gates, qk_conv_w, qk_conv_b, gate_bias, *, batch, seq, col0):
    L = MLSTM_CHUNK
    nh = MLSTM_HEADS
    w = qk_conv_w.shape[1] // 2
    dh = w // nh
    nc = seq // L
    qk_blk = col0 // (2 * w)
    v_blk = (col0 + 2 * w) // w
    o_blk = v_blk + 1
    kern = functools.partial(_mlstm_kernel, nh=nh, dh=dh)
    return pl.pallas_call(
        kern, grid=(batch, nc),
        in_specs=[
            pl.BlockSpec((L, 2 * w), lambda b, c: (b * nc + c, qk_blk)),
            pl.BlockSpec((QK_HALO, 2 * w),
                         lambda b, c: (jnp.maximum((b * nc + c) * (L // QK_HALO) - 1, 0), qk_blk)),
            pl.BlockSpec((L, w), lambda b, c: (b * nc + c, v_blk)),
            pl.BlockSpec((L, w), lambda b, c: (b * nc + c, o_blk)),
            pl.BlockSpec((L, LANES), lambda b, c: (b * nc + c, 0)),
            pl.BlockSpec((MLSTM_QK_CONV, 2 * w), lambda b, c: (0, 0)),
            pl.BlockSpec((1, 2 * w), lambda b, c: (0, 0)),
            pl.BlockSpec((1, LANES), lambda b, c: (0, 0)),
        ],
        out_specs=pl.BlockSpec((L, w), lambda b, c: (b * nc + c, 0)),
        out_shape=jax.ShapeDtypeStruct((batch * seq, w), BF16),
        scratch_shapes=[
            pltpu.VMEM((QK_HALO + L, 2 * w), F32),
            pltpu.VMEM((nh, dh, dh), F32),
            pltpu.VMEM((nh, SUBLANES, dh), F32),
            pltpu.VMEM((nh, SUBLANES, LANES), F32),
        ],
        compiler_params=_cparams("parallel", "arbitrary"), name="mlstm",
    )(z, z, z, z, gates, qk_conv_w, qk_conv_b, gate_bias)


def _out_proj_kernel(a_ref, b_ref, wa_ref, wb_ref, r_ref, o_ref):
    o_ref[...] = (r_ref[...]
                  + jnp.dot(a_ref[...], wa_ref[...], preferred_element_type=F32)
                  + jnp.dot(b_ref[...], wb_ref[...], preferred_element_type=F32))


def out_proj(a, b, wa, wb, res, *, tm):
    t, d = res.shape
    ka, kb = a.shape[1], b.shape[1]
    return pl.pallas_call(
        _out_proj_kernel, grid=(t // tm,),
        in_specs=[
            pl.BlockSpec((tm, ka), lambda i: (i, 0)),
            pl.BlockSpec((tm, kb), lambda i: (i, 0)),
            pl.BlockSpec((ka, d), lambda i: (0, 0)),
            pl.BlockSpec((kb, d), lambda i: (0, 0)),
            pl.BlockSpec((tm, d), lambda i: (i, 0)),
        ],
        out_specs=pl.BlockSpec((tm, d), lambda i: (i, 0)),
        out_shape=jax.ShapeDtypeStruct((t, d), F32),
        compiler_params=_cparams("parallel"), name="out_proj",
    )(a, b, wa, wb, res)


def _ffn_kernel(x_ref, g_ref, w1_ref, w3_ref, w2_ref, o_ref, xn_ref):
    f = pl.program_id(1)

    @pl.when(f == 0)
    def _():
        x = x_ref[...]
        xn_ref[...] = _rms(x, g_ref[...]).astype(BF16)
        o_ref[...] = x

    xn = xn_ref[...]
    a = jnp.dot(xn, w1_ref[...], preferred_element_type=F32)
    b = jnp.dot(xn, w3_ref[...], preferred_element_type=F32)
    hid = (a * jax.nn.sigmoid(a) * b).astype(BF16)
    o_ref[...] += jnp.dot(hid, w2_ref[...], preferred_element_type=F32)


def ffn(x, g, w1, w3, w2, *, tm, tf):
    t, d = x.shape
    dff = w1.shape[1]
    return pl.pallas_call(
        _ffn_kernel, grid=(t // tm, dff // tf),
        in_specs=[
            pl.BlockSpec((tm, d), lambda i, f: (i, 0)),
            pl.BlockSpec((1, d), lambda i, f: (0, 0)),
            pl.BlockSpec((d, tf), lambda i, f: (0, f)),
            pl.BlockSpec((d, tf), lambda i, f: (0, f)),
            pl.BlockSpec((tf, d), lambda i, f: (f, 0)),
        ],
        out_specs=pl.BlockSpec((tm, d), lambda i, f: (i, 0)),
        out_shape=jax.ShapeDtypeStruct((t, d), F32),
        scratch_shapes=[pltpu.VMEM((tm, d), BF16)],
        compiler_params=_cparams("parallel", "arbitrary"), name="ffn",
    )(x, g, w1, w3, w2)


def _attn_kernel(q_ref, k_ref, v_ref, o_ref, qf_ref, kf_ref, vf_ref, qd_ref, kd_ref, vd_ref,
                 od_ref, ld_ref, on_ref, ln_ref, *, seq, dh):
    T = ATTN_BLOCK
    nblk = seq // T
    qf_ref[...] = q_ref[...].astype(F32) * (dh ** -0.5)
    kf_ref[...] = k_ref[...].astype(F32)
    vf_ref[...] = v_ref[...].astype(F32)
    kd_ref[0:T, :] = jnp.zeros((T, dh), BF16)
    vd_ref[0:T, :] = jnp.zeros((T, dh), BF16)

    qi = lax.broadcasted_iota(jnp.int32, (T, 2 * T), 0)
    ki = lax.broadcasted_iota(jnp.int32, (T, 2 * T), 1)
    dist = T + qi - ki
    band = (dist >= 0) & (dist <= T)

    for g, dil in enumerate(DILATIONS):
        ls = seq // dil
        nb = ls // T
        for r in range(dil):
            rows = pl.ds(r, ls, stride=dil) if dil > 1 else pl.ds(0, ls)
            qd_ref[r * ls:(r + 1) * ls, :] = qf_ref[rows, :].astype(BF16)
            kd_ref[T + r * ls:T + (r + 1) * ls, :] = kf_ref[rows, :].astype(BF16)
            vd_ref[T + r * ls:T + (r + 1) * ls, :] = vf_ref[rows, :].astype(BF16)

        def block(n, carry):
            base = pl.multiple_of(n * T, T)
            qb = qd_ref[pl.ds(base, T), :]
            kb = kd_ref[pl.ds(base, 2 * T), :]
            vb = vd_ref[pl.ds(base, 2 * T), :]
            s = lax.dot_general(qb, kb, (((1,), (1,)), ((), ())), preferred_element_type=F32)
            kmin = jnp.where(n % nb == 0, T, 0)
            s = jnp.where(band & (ki >= kmin), s, -jnp.inf)
            m = jnp.max(s, axis=-1, keepdims=True)
            p = jnp.exp(s - m)
            l = jnp.sum(p, axis=-1, keepdims=True)
            o = jnp.dot(p.astype(BF16), vb, preferred_element_type=F32) / l
            od_ref[pl.ds(base, T), :] = o
            ld_ref[pl.ds(base, T), :] = jnp.broadcast_to(m + jnp.log(l), (T, dh))
            return carry

        lax.fori_loop(0, nblk, block, 0, unroll=8)

        for r in range(dil):
            rows = pl.ds(r, ls, stride=dil) if dil > 1 else pl.ds(0, ls)
            on_ref[g, rows, :] = od_ref[r * ls:(r + 1) * ls, :]
            ln_ref[g, rows, :] = ld_ref[r * ls:(r + 1) * ls, :]

    def merge(n, carry):
        rows = pl.ds(pl.multiple_of(n * T, T), T)
        lses = [ln_ref[g, rows, :] for g in range(len(DILATIONS))]
        mx = functools.reduce(jnp.maximum, lses)
        ws = [jnp.exp(l - mx) for l in lses]
        tot = functools.reduce(lambda a, b: a + b, ws)
        acc = ws[0] * on_ref[0, rows, :]
        for g in range(1, len(DILATIONS)):
            acc = acc + ws[g] * on_ref[g, rows, :]
        o_ref[rows, :] = (acc / tot).astype(o_ref.dtype)
        return carry

    lax.fori_loop(0, nblk, merge, 0)


def dilated_attention(z, *, batch, seq, width):
    nh = ATTN_HEADS
    dh = width // nh
    ng = len(DILATIONS)
    kern = functools.partial(_attn_kernel, seq=seq, dh=dh)
    blk = lambda off: pl.BlockSpec((seq, dh), lambda b, h: (b, off + h))
    return pl.pallas_call(
        kern, grid=(batch, nh),
        in_specs=[blk(0), blk(nh), blk(2 * nh)],
        out_specs=pl.BlockSpec((seq, dh), lambda b, h: (b, h)),
        out_shape=jax.ShapeDtypeStruct((batch * seq, width), BF16),
        scratch_shapes=[
            pltpu.VMEM((seq, dh), F32), pltpu.VMEM((seq, dh), F32), pltpu.VMEM((seq, dh), F32),
            pltpu.VMEM((seq, dh), BF16),
            pltpu.VMEM((ATTN_BLOCK + seq, dh), BF16), pltpu.VMEM((ATTN_BLOCK + seq, dh), BF16),
            pltpu.VMEM((seq, dh), F32), pltpu.VMEM((seq, dh), F32),
            pltpu.VMEM((ng, seq, dh), F32), pltpu.VMEM((ng, seq, dh), F32),
        ],
        compiler_params=_cparams("parallel", "parallel"), name="dilated_attention",
    )(z, z, z)


S5_SEGS = SUBLANES
S5_SEG = 64
S5_PITCH = S5_SEG + 4
S5_TILE = S5_SEGS * S5_SEG
S5_ROWS = S5_SEGS * S5_PITCH
S5_UNROLL = S5_SEG
S5_CHUNK_IN = 256
S5_SLABS = S5_CHUNK_IN // SSM_GROUP * SSM_STATE // LANES


def _s5_kernel(u_ref, bre_ref, bim_ref, cre_ref, cim_ref, lre_ref, lim_ref, dsk_ref, wg_ref, bg_ref,
               o_ref, ug_ref, sre_ref, sim_ref, cyre_ref, cyim_ref, y_ref, *, nchunk):
    first = pl.program_id(1) == 0

    @pl.when(first)
    def _():
        cyre_ref[...] = jnp.zeros_like(cyre_ref)
        cyim_ref[...] = jnp.zeros_like(cyim_ref)
        ug_ref[...] = jnp.zeros_like(ug_ref)

    for i in range(S5_SEGS):
        ug_ref[i * S5_PITCH:i * S5_PITCH + S5_SEG, :] = u_ref[i * S5_SEG:(i + 1) * S5_SEG, :].astype(F32)

    seg_id = lax.broadcasted_iota(jnp.int32, (S5_SEGS, LANES), 0)
    for c in range(nchunk):
        par = (c % 2) * S5_SLABS
        ucols = slice(c * S5_CHUNK_IN, (c + 1) * S5_CHUNK_IN)
        ub = ug_ref[:, ucols].astype(BF16)
        bu_re = jnp.dot(ub, bre_ref[c], preferred_element_type=F32)
        bu_im = jnp.dot(ub, bim_ref[c], preferred_element_type=F32)
        for s in range(S5_SLABS):
            sre_ref[par + s] = bu_re[:, s * LANES:(s + 1) * LANES]
            sim_ref[par + s] = bu_im[:, s * LANES:(s + 1) * LANES]

        lam_re = [jnp.broadcast_to(lre_ref[c, :, s * LANES:(s + 1) * LANES], (S5_SEGS, LANES))
                  for s in range(S5_SLABS)]
        lam_im = [jnp.broadcast_to(lim_ref[c, :, s * LANES:(s + 1) * LANES], (S5_SEGS, LANES))
                  for s in range(S5_SLABS)]

        def step(j, st, store):
            rows = pl.ds(j, S5_SEGS, stride=S5_PITCH)
            new = []
            for s in range(S5_SLABS):
                pr, pi = st[2 * s], st[2 * s + 1]
                nr = lam_re[s] * pr - lam_im[s] * pi + sre_ref[par + s, rows, :]
                ni = lam_re[s] * pi + lam_im[s] * pr + sim_ref[par + s, rows, :]
                if store:
                    sre_ref[par + s, rows, :] = nr
                    sim_ref[par + s, rows, :] = ni
                new += [nr, ni]
            return tuple(new)

        zero = tuple(jnp.zeros((S5_SEGS, LANES), F32) for _ in range(2 * S5_SLABS))
        ends = lax.fori_loop(0, S5_SEG, lambda j, st: step(j, st, False), zero, unroll=S5_UNROLL)

        pw_re, pw_im = [l[0:1] for l in lam_re], [l[0:1] for l in lam_im]
        for _ in range(int(math.log2(S5_SEG))):
            pw_re, pw_im = ([a * a - b * b for a, b in zip(pw_re, pw_im)],
                            [2.0 * a * b for a, b in zip(pw_re, pw_im)])
        init = []
        for s in range(S5_SLABS):
            cols = slice(s * LANES, (s + 1) * LANES)
            cr, ci = cyre_ref[c, 0:1, cols], cyim_ref[c, 0:1, cols]
            in_re = jnp.zeros((S5_SEGS, LANES), F32)
            in_im = jnp.zeros((S5_SEGS, LANES), F32)
            for i in range(S5_SEGS):
                in_re = jnp.where(seg_id == i, cr, in_re)
                in_im = jnp.where(seg_id == i, ci, in_im)
                er, ei = ends[2 * s][i:i + 1], ends[2 * s + 1][i:i + 1]
                cr, ci = (pw_re[s] * cr - pw_im[s] * ci + er, pw_re[s] * ci + pw_im[s] * cr + ei)
            cyre_ref[c, :, cols] = jnp.broadcast_to(cr, (SUBLANES, LANES))
            cyim_ref[c, :, cols] = jnp.broadcast_to(ci, (SUBLANES, LANES))
            init += [in_re, in_im]

        lax.fori_loop(0, S5_SEG, lambda j, st: step(j, st, True), tuple(init), unroll=S5_UNROLL)

        st_re = jnp.concatenate([sre_ref[par + s] for s in range(S5_SLABS)], axis=1).astype(BF16)
        st_im = jnp.concatenate([sim_ref[par + s] for s in range(S5_SLABS)], axis=1).astype(BF16)
        y_ref[:, ucols] = (jnp.dot(st_re, cre_ref[c], preferred_element_type=F32)
                           + jnp.dot(st_im, cim_ref[c], preferred_element_type=F32))

    for i in range(S5_SEGS):
        rows = slice(i * S5_PITCH, i * S5_PITCH + S5_SEG)
        y = y_ref[rows, :] + dsk_ref[...] * ug_ref[rows, :]
        y = jax.nn.gelu(y)
        gate = jnp.dot(y.astype(BF16), wg_ref[...], preferred_element_type=F32) + bg_ref[...]
        o_ref[i * S5_SEG:(i + 1) * S5_SEG, :] = (y * jax.nn.sigmoid(gate)).astype(o_ref.dtype)


def s5(z, b_re, b_im, c_re, c_im, lam_re, lam_im, d_skip, w_glu, b_glu, *, batch, seq, col_blk):
    nchunk, _, nstate = b_re.shape
    ch = nchunk * S5_CHUNK_IN
    nt = seq // S5_TILE
    kern = functools.partial(_s5_kernel, nchunk=nchunk)
    full = lambda shape: pl.BlockSpec(shape, lambda b, i: (0,) * len(shape))
    return pl.pallas_call(
        kern, grid=(batch, nt),
        in_specs=[
            pl.BlockSpec((S5_TILE, ch), lambda b, i: (b * nt + i, col_blk)),
            full(b_re.shape), full(b_im.shape), full(c_re.shape), full(c_im.shape),
            full(lam_re.shape), full(lam_im.shape), full((1, ch)), full((ch, ch)), full((1, ch)),
        ],
        out_specs=pl.BlockSpec((S5_TILE, ch), lambda b, i: (b * nt + i, 0)),
        out_shape=jax.ShapeDtypeStruct((batch * seq, ch), BF16),
        scratch_shapes=[
            pltpu.VMEM((S5_ROWS, ch), F32),
            pltpu.VMEM((2 * S5_SLABS, S5_ROWS, LANES), F32),
            pltpu.VMEM((2 * S5_SLABS, S5_ROWS, LANES), F32),
            pltpu.VMEM((nchunk, SUBLANES, nstate), F32),
            pltpu.VMEM((nchunk, SUBLANES, nstate), F32),
            pltpu.VMEM((S5_ROWS, ch), F32),
        ],
        compiler_params=_cparams("parallel", "arbitrary"), name="s5",
    )(z, b_re, b_im, c_re, c_im, lam_re, lam_im, d_skip, w_glu, b_glu)


def s5_params(lam_re, lam_im, log_dt, b_re, b_im, c_re, c_im):
    ng, ns = lam_re.shape
    gpc = S5_CHUNK_IN // SSM_GROUP
    nchunk = ng // gpc
    dt = jnp.exp(log_dt)[:, None]
    mag = jnp.exp(lam_re * dt)
    lb_re, lb_im = mag * jnp.cos(lam_im * dt), mag * jnp.sin(lam_im * dt)
    den = lam_re * lam_re + lam_im * lam_im
    f_re = ((lb_re - 1.0) * lam_re + lb_im * lam_im) / den
    f_im = (lb_im * lam_re - (lb_re - 1.0) * lam_im) / den
    bb_re = f_re[..., None] * b_re - f_im[..., None] * b_im
    bb_im = f_re[..., None] * b_im + f_im[..., None] * b_re
    eye = jnp.eye(gpc, dtype=F32)

    def pack_b(m):
        m = m.reshape(nchunk, gpc, ns, SSM_GROUP)
        return jnp.einsum('cgpk,gh->cgkhp', m, eye).reshape(nchunk, gpc * SSM_GROUP, gpc * ns).astype(BF16)

    def pack_c(m):
        m = m.reshape(nchunk, gpc, SSM_GROUP, ns)
        return jnp.einsum('cgkp,gh->cgphk', m, eye).reshape(nchunk, gpc * ns, gpc * SSM_GROUP).astype(BF16)

    return (pack_b(bb_re), pack_b(bb_im), pack_c(c_re), pack_c(-c_im),
            lb_re.reshape(nchunk, 1, gpc * ns), lb_im.reshape(nchunk, 1, gpc * ns))


def _pack_bf16_pair(lo, hi):
    lo_bits = lax.bitcast_convert_type(lo.astype(BF16).astype(F32), jnp.uint32) >> 16
    hi_bits = lax.bitcast_convert_type(hi.astype(BF16).astype(F32), jnp.uint32) & jnp.uint32(0xFFFF0000)
    return lo_bits | hi_bits


def _unpack_bf16_pair(w):
    lo = lax.bitcast_convert_type(w << 16, F32)
    hi = lax.bitcast_convert_type(w & jnp.uint32(0xFFFF0000), F32)
    return lo, hi


def _split3(v):
    hi = v.astype(BF16)
    r1 = v - hi.astype(F32)
    mid = r1.astype(BF16)
    lo = (r1 - mid.astype(F32)).astype(BF16)
    return hi, mid, lo


def _out_proj_route_kernel(a_ref, b_ref, wa_ref, wb_ref, r_ref, g_ref, wr_ref, x_ref, xp_ref, eid_ref, gate_ref):
    x = (r_ref[...]
         + jnp.dot(a_ref[...], wa_ref[...], preferred_element_type=F32)
         + jnp.dot(b_ref[...], wb_ref[...], preferred_element_type=F32))
    x_ref[...] = x
    xn = _rms(x, g_ref[...])
    half = xn.shape[1] // 2
    xp_ref[...] = _pack_bf16_pair(xn[:, :half], xn[:, half:])
    xh, xm, xl = _split3(xn)
    ph = jnp.dot(xh, wr_ref[...], preferred_element_type=F32)
    pm = jnp.dot(xm, wr_ref[:, :2 * LANES], preferred_element_type=F32)
    pl_ = jnp.dot(xl, wr_ref[:, :LANES], preferred_element_type=F32)
    logits = (ph[:, :LANES] + (ph[:, LANES:2 * LANES] + pm[:, :LANES])
              + (ph[:, 2 * LANES:] + pm[:, LANES:] + pl_))
    lane = lax.broadcasted_iota(jnp.int32, logits.shape, 1)
    logits = jnp.where(lane < N_EXPERTS, logits, -jnp.inf)
    v1 = jnp.max(logits, axis=-1, keepdims=True)
    i1 = jnp.min(jnp.where(logits == v1, lane, LANES), axis=-1, keepdims=True)
    rest = jnp.where(lane == i1, -jnp.inf, logits)
    v2 = jnp.max(rest, axis=-1, keepdims=True)
    i2 = jnp.min(jnp.where(rest == v2, lane, LANES), axis=-1, keepdims=True)
    e2 = jnp.exp(v2 - v1)
    g1 = 1.0 / (1.0 + e2)
    g2 = e2 / (1.0 + e2)
    eid_ref[...] = jnp.where(lane == 0, i1, jnp.where(lane == 1, i2, 0))
    gate_ref[...] = jnp.where(lane == 0, g1, jnp.where(lane == 1, g2, 0.0))


def out_proj_route(a, b, wa, wb, res, g, w_router, *, tm):
    t, d = res.shape
    ka, kb = a.shape[1], b.shape[1]
    row = lambda n: pl.BlockSpec((tm, n), lambda i: (i, 0))
    full = lambda r, c: pl.BlockSpec((r, c), lambda i: (0, 0))
    return pl.pallas_call(
        _out_proj_route_kernel, grid=(t // tm,),
        in_specs=[row(ka), row(kb), full(ka, d), full(kb, d), row(d), full(1, d), full(d, 3 * LANES)],
        out_specs=[row(d), row(d // 2), row(LANES), row(LANES)],
        out_shape=[jax.ShapeDtypeStruct((t, d), F32), jax.ShapeDtypeStruct((t, d // 2), jnp.uint32),
                   jax.ShapeDtypeStruct((t, LANES), jnp.int32), jax.ShapeDtypeStruct((t, LANES), F32)],
        compiler_params=_cparams("parallel"), name="out_proj_route",
    )(a, b, wa, wb, res, g, w_router)


EXPERT_TM = 1536
EXPERT_TF = 256
COMBINE_TC = 512


def _gather_rows_per_step(tm, nf):
    return -(-tm // ((nf - 1) * SUBLANES)) * SUBLANES


def _expert_kernel(te_ref, tv_ref, src_ref, dst_ref, xp_hbm, w1_ref, w3_ref, w2_ref, ys_hbm,
                   gbuf, xb, acc, obuf, gsem, ssem, *, tm, nf, gr):
    i = pl.program_id(0)
    f = pl.program_id(1)
    ntiles = pl.num_programs(0)
    gb = gr * (nf - 1)
    half = xb.shape[1] // 2

    def gather(tile, r):
        return pltpu.make_async_copy(xp_hbm.at[pl.ds(src_ref[tile * gb + r], 1), :],
                                     gbuf.at[pl.ds(r, 1), :], gsem)

    def scatter(tile, r):
        return pltpu.make_async_copy(obuf.at[pl.ds(r, 1), :],
                                     ys_hbm.at[pl.ds(dst_ref[(tile + 1) * gb + r], 1), :], ssem)

    def for_rows(n, fn):
        def body(r, carry):
            fn(r)
            return carry
        lax.fori_loop(0, n, body, 0, unroll=8)

    @pl.when((i == 0) & (f == 0))
    def _():
        obuf[...] = jnp.zeros_like(obuf)
        for_rows(gb, lambda r: gather(0, r).start())

    @pl.when(f == 0)
    def _():
        for_rows(gb, lambda r: gather(i, r).wait())
        lo, hi = _unpack_bf16_pair(gbuf[0:tm, :])
        xb[:, :half] = lo.astype(BF16)
        xb[:, half:] = hi.astype(BF16)
        acc[...] = jnp.zeros_like(acc)

    def swiglu_rows(rows):
        x = xb[0:rows, :]
        a = jnp.dot(x, w1_ref[0].astype(BF16), preferred_element_type=F32)
        b = jnp.dot(x, w3_ref[0].astype(BF16), preferred_element_type=F32)
        hid = (a * jax.nn.sigmoid(a) * b).astype(BF16)
        acc[0:rows, :] += jnp.dot(hid, w2_ref[0].astype(BF16), preferred_element_type=F32)

    def step(rows, moving):
        if moving:
            for r in range(gr):
                gather(i + 1, f * gr + r).start()
                scatter(i - 1, f * gr + r).start()
        if rows:
            swiglu_rows(rows)

    used = tv_ref[i]
    is_last = f == nf - 1
    quarter = tm // 4
    conds = [(used > q * quarter) & (used <= (q + 1) * quarter) for q in range(4)] + [used == 0]
    for cond, rows in zip(conds, (quarter, 2 * quarter, 3 * quarter, tm, 0)):
        pl.when(cond & jnp.logical_not(is_last))(functools.partial(step, rows, True))
        pl.when(cond & is_last)(functools.partial(step, rows, False))

    @pl.when(is_last)
    def _():
        for_rows(gb, lambda r: scatter(i - 1, r).wait())
        y = acc[...]
        obuf[0:tm, :] = _pack_bf16_pair(y[:, :half], y[:, half:])

    @pl.when((i == ntiles - 1) & is_last)
    def _():
        for_rows(gb, lambda r: gather(i + 1, r).wait())
        for_rows(gb, lambda r: scatter(i, r).start())
        for_rows(gb, lambda r: scatter(i, r).wait())


def expert_ffn(tile_e, tile_v, src, dst, xp, w1, w3, w2, *, n_out_rows):
    tm, tf = EXPERT_TM, EXPERT_TF
    dh = xp.shape[1]
    d = 2 * dh
    dff = w1.shape[2]
    nf = dff // tf
    gr = _gather_rows_per_step(tm, nf)
    ntiles = tile_e.shape[0]
    fsel = lambda i, f, te, tv: jnp.where(tv[i] > 0, f, nf - 1)
    kern = functools.partial(_expert_kernel, tm=tm, nf=nf, gr=gr)
    return pl.pallas_call(
        kern,
        grid_spec=pltpu.PrefetchScalarGridSpec(
            num_scalar_prefetch=4, grid=(ntiles, nf),
            in_specs=[
                pl.BlockSpec(memory_space=pl.ANY),
                pl.BlockSpec((1, d, tf), lambda i, f, te, tv, s, t: (te[i], 0, fsel(i, f, te, tv))),
                pl.BlockSpec((1, d, tf), lambda i, f, te, tv, s, t: (te[i], 0, fsel(i, f, te, tv))),
                pl.BlockSpec((1, tf, d), lambda i, f, te, tv, s, t: (te[i], fsel(i, f, te, tv), 0)),
            ],
            out_specs=pl.BlockSpec(memory_space=pl.ANY),
            scratch_shapes=[
                pltpu.VMEM((gr * (nf - 1), dh), jnp.uint32),
                pltpu.VMEM((tm, d), BF16),
                pltpu.VMEM((tm, d), F32),
                pltpu.VMEM((gr * (nf - 1), dh), jnp.uint32),
                pltpu.SemaphoreType.DMA(()),
                pltpu.SemaphoreType.DMA(()),
            ]),
        out_shape=jax.ShapeDtypeStruct((n_out_rows, dh), jnp.uint32),
        compiler_params=_cparams("arbitrary", "arbitrary"), name="expert_ffn",
    )(tile_e, tile_v, src, dst, xp, w1, w3, w2)


def _combine_kernel(x_ref, gate_ref, g_ref, y1_ref, y2_ref, o_ref):
    gates = gate_ref[...]
    half = x_ref.shape[1] // 2
    lo1, hi1 = _unpack_bf16_pair(y1_ref[...])
    lo2, hi2 = _unpack_bf16_pair(y2_ref[...])
    g1, g2 = gates[:, 0:1], gates[:, 1:2]
    x = x_ref[...]
    x = jnp.concatenate([x[:, :half] + g1 * lo1 + g2 * lo2, x[:, half:] + g1 * hi1 + g2 * hi2], axis=1)
    o_ref[...] = _rms(x, g_ref[...])


def combine(x, gates, g, ys):
    t, d = x.shape
    tc = COMBINE_TC
    return pl.pallas_call(
        _combine_kernel, grid=(t // tc,),
        in_specs=[
            pl.BlockSpec((tc, d), lambda i: (i, 0)),
            pl.BlockSpec((tc, LANES), lambda i: (i, 0)),
            pl.BlockSpec((1, d), lambda i: (0, 0)),
            pl.BlockSpec((tc, d // 2), lambda i: (i, 0)),
            pl.BlockSpec((tc, d // 2), lambda i: (t // tc + i, 0)),
        ],
        out_specs=pl.BlockSpec((tc, d), lambda i: (i, 0)),
        out_shape=jax.ShapeDtypeStruct((t, d), F32),
        compiler_params=_cparams("parallel"), name="moe_combine",
    )(x, gates, g, ys, ys)


def route_slots(eid, *, dff):
    tm = EXPERT_TM
    nf = dff // EXPERT_TF
    gb = _gather_rows_per_step(tm, nf) * (nf - 1)
    t = eid.shape[0]
    na = 2 * t
    e_flat = eid.reshape(na)
    onehot = (e_flat[:, None] == jnp.arange(N_EXPERTS, dtype=jnp.int32)[None, :]).astype(jnp.int32)
    csum = jnp.cumsum(onehot, axis=0)
    rank = jnp.sum((csum - onehot) * onehot, axis=1)
    counts = csum[-1]
    padded = (counts + tm - 1) // tm * tm
    ends = jnp.cumsum(padded)
    slot = jnp.sum(onehot * (ends - padded)[None, :], axis=1) + rank
    ntiles = -(-(na + N_EXPERTS * tm) // tm)
    ns = ntiles * tm
    asg = jnp.full((ns,), -1, jnp.int32).at[slot].set(jnp.arange(na, dtype=jnp.int32))
    is_pad = asg < 0
    tok = jnp.where(is_pad, 0, asg // 2)
    pad_rank = jnp.cumsum(is_pad.astype(jnp.int32)) - 1
    dst = jnp.where(is_pad, na + gb + pad_rank, (asg % 2) * t + asg // 2).reshape(ntiles, tm)
    over = (na + gb + (ns - na) + jnp.arange(ntiles * (gb - tm), dtype=jnp.int32)).reshape(ntiles, gb - tm)
    dst = jnp.concatenate([na + jnp.arange(gb, dtype=jnp.int32),
                           jnp.concatenate([dst, over], axis=1).reshape(-1)])
    src = jnp.pad(tok.reshape(ntiles, tm), ((0, 1), (0, gb - tm))).reshape(-1)
    starts = jnp.arange(ntiles, dtype=jnp.int32) * tm
    tile_e = jnp.sum((starts[:, None] >= ends[None, :]).astype(jnp.int32), axis=1)
    used_end = jnp.concatenate([ends - padded + counts, jnp.zeros((1,), jnp.int32)])
    tile_v = jnp.clip(used_end[tile_e] - starts, 0, tm)
    last_e = jnp.max(jnp.where(tile_v > 0, tile_e, 0))
    tile_e = jnp.where(tile_v > 0, tile_e, last_e)
    n_out_rows = na + gb + (ns - na) + ntiles * (gb - tm)
    n_out_rows = -(-n_out_rows // COMBINE_TC) * COMBINE_TC
    return tile_e, tile_v, src, dst, n_out_rows


def kernel(x, e_norm_mix, e_w_in, e_conv_w, e_conv_b, e_ln_g, e_ln_b, e_qk_conv_w, e_qk_conv_b, e_b_i, e_b_f, e_w_out, e_norm_ffn, e_ffn_w1, e_ffn_w3, e_ffn_w2, o_norm_mix, o_w_in, o_lam_re, o_lam_im, o_log_dt, o_b_re, o_b_im, o_c_re, o_c_im, o_d_skip, o_w_glu, o_b_glu, o_w_out, o_norm_ffn, o_router, o_exp_w1, o_exp_w3, o_exp_w2, final_norm):
    batch, seq, d = x.shape
    t = batch * seq
    xt = x.reshape(t, d)
    row = lambda v: v.reshape(1, -1)
    assert e_norm_mix.shape[0] == 1 and o_norm_mix.shape[0] == 1, "one even and one odd layer"

    ch = e_conv_w.shape[2]
    mw = e_qk_conv_w.shape[2] // 2
    nh = MLSTM_HEADS
    main_cols = 2 * ch + 4 * mw
    w_in = e_w_in[0].astype(BF16)
    w_gates = jnp.pad(w_in[:, main_cols:], ((0, 0), (0, LANES - 2 * nh)))
    z0, gates0 = norm_matmul(xt, row(e_norm_mix[0]), w_in, w_gates, tm=512, tn=1024, n_out=main_cols)
    out_a = conformer_conv(z0, e_conv_w[0], row(e_conv_b[0]), row(e_ln_g[0]), row(e_ln_b[0]),
                           batch=batch, seq=seq, ts=512)
    gate_bias = jnp.pad(jnp.concatenate([e_b_i[0], e_b_f[0]]), (0, LANES - 2 * nh)).reshape(1, LANES)
    out_b = mlstm(z0, gates0, e_qk_conv_w[0], row(e_qk_conv_b[0]), gate_bias, batch=batch, seq=seq, col0=2 * ch)
    w_out = e_w_out[0].astype(BF16)
    x1 = out_proj(out_a, out_b, w_out[:ch], w_out[ch:], xt, tm=512)
    x2 = ffn(x1, row(e_norm_ffn[0]), e_ffn_w1[0].astype(BF16), e_ffn_w3[0].astype(BF16),
             e_ffn_w2[0].astype(BF16), tm=1024, tf=512)

    aw = o_w_in.shape[2] - o_d_skip.shape[1]
    aw //= 3
    sch = o_d_skip.shape[1]
    z1 = norm_matmul(x2, row(o_norm_mix[0]), o_w_in[0].astype(BF16), tm=512, tn=1024)
    out_c = dilated_attention(z1, batch=batch, seq=seq, width=aw)
    sp = s5_params(o_lam_re[0], o_lam_im[0], o_log_dt[0], o_b_re[0], o_b_im[0], o_c_re[0], o_c_im[0])
    out_d = s5(z1, *sp, row(o_d_skip[0]), o_w_glu[0].astype(BF16), row(o_b_glu[0]),
               batch=batch, seq=seq, col_blk=3 * aw // sch)
    w_out1 = o_w_out[0].astype(BF16)
    w_router = jnp.concatenate(
        [jnp.pad(p, ((0, 0), (0, LANES - N_EXPERTS))) for p in _split3(o_router[0])], axis=1)
    x3, xp3, eid, gates = out_proj_route(out_c, out_d, w_out1[:aw], w_out1[aw:], x2, row(o_norm_ffn[0]),
                                         w_router, tm=512)
    tile_e, tile_v, src, dst, n_out_rows = route_slots(eid[:, :2], dff=o_exp_w1.shape[3])
    ys = expert_ffn(tile_e, tile_v, src, dst, xp3, o_exp_w1[0], o_exp_w3[0], o_exp_w2[0], n_out_rows=n_out_rows)
    out = combine(x3, gates, row(final_norm), ys)
    return out.reshape(batch, seq, d)
```

```python
import functools
import math

import jax
import jax.numpy as jnp
from jax import lax
from jax.experimental import pallas as pl
from jax.experimental.pallas import tpu as pltpu

F32 = jnp.float32
BF16 = jnp.bfloat16

RMS_EPS = 1e-6
LN_EPS = 1e-5
CONV_WIDTH = 31
MLSTM_HEADS = 8
MLSTM_QK_CONV = 4
MLSTM_CHUNK = 128
ATTN_HEADS = 8
ATTN_BLOCK = 128
DILATIONS = (1, 4, 16)
SSM_GROUP = 16
SSM_STATE = 64
N_EXPERTS = 8
LANES = 128
SUBLANES = 8
VMEM_LIMIT = 56 * 1024 * 1024


def _cparams(*sem):
    return pltpu.CompilerParams(dimension_semantics=sem, vmem_limit_bytes=VMEM_LIMIT)


def _rms(x, g):
    return x * lax.rsqrt(jnp.mean(x * x, axis=-1, keepdims=True) + RMS_EPS) * g


def _norm_matmul_kernel(x_ref, g_ref, w_ref, o_ref, *, tn):
    xn = _rms(x_ref[...], g_ref[...]).astype(BF16)
    for j in range(o_ref.shape[1] // tn):
        cols = slice(j * tn, (j + 1) * tn)
        o_ref[:, cols] = jnp.dot(xn, w_ref[:, cols], preferred_element_type=F32).astype(o_ref.dtype)


def _norm_matmul_aux_kernel(x_ref, g_ref, w_ref, wa_ref, o_ref, oa_ref, *, tn):
    xn = _rms(x_ref[...], g_ref[...]).astype(BF16)
    oa_ref[...] = jnp.dot(xn, wa_ref[...], preferred_element_type=F32)
    for j in range(o_ref.shape[1] // tn):
        cols = slice(j * tn, (j + 1) * tn)
        o_ref[:, cols] = jnp.dot(xn, w_ref[:, cols], preferred_element_type=F32).astype(o_ref.dtype)


def norm_matmul(x, g, w, w_aux=None, *, tm, tn, n_out=None):
    t, d = x.shape
    n = w.shape[1] if n_out is None else n_out
    resident = lambda shape: pl.BlockSpec(shape, lambda i: (0, 0), pipeline_mode=pl.Buffered(1))
    x_spec = pl.BlockSpec((tm, d), lambda i: (i, 0))
    g_spec = pl.BlockSpec((1, d), lambda i: (0, 0))
    o_spec = pl.BlockSpec((tm, n), lambda i: (i, 0))
    if w_aux is None:
        return pl.pallas_call(
            functools.partial(_norm_matmul_kernel, tn=tn), grid=(t // tm,),
            in_specs=[x_spec, g_spec, resident(w.shape)], out_specs=o_spec,
            out_shape=jax.ShapeDtypeStruct((t, n), BF16),
            compiler_params=_cparams("parallel"), name="norm_matmul")(x, g, w)
    na = w_aux.shape[1]
    return pl.pallas_call(
        functools.partial(_norm_matmul_aux_kernel, tn=tn), grid=(t // tm,),
        in_specs=[x_spec, g_spec, resident(w.shape), resident(w_aux.shape)],
        out_specs=[o_spec, pl.BlockSpec((tm, na), lambda i: (i, 0))],
        out_shape=[jax.ShapeDtypeStruct((t, n), BF16), jax.ShapeDtypeStruct((t, na), F32)],
        compiler_params=_cparams("parallel"), name="norm_matmul_aux")(x, g, w, w_aux)


CONV_HALO = 32
CONV_ROWS = 64
NORM_ROWS = 32


def _conv_kernel(u_ref, halo_ref, w_ref, cb_ref, lg_ref, lb_ref, o_ref, buf_ref, xs_ref, acc_ref, *, ts, ch):
    first = pl.program_id(1) == 0

    def glu(u):
        u = u.astype(F32)
        return u[:, :ch] * jax.nn.sigmoid(u[:, ch:])

    buf_ref[0:CONV_HALO, :] = jnp.where(first, 0.0, glu(halo_ref[...]))
    buf_ref[CONV_HALO:CONV_HALO + ts, :] = glu(u_ref[...])
    n_shift = ts + CONV_HALO - SUBLANES
    for b in range(1, SUBLANES):
        xs_ref[b, 0:n_shift, :] = buf_ref[b:b + n_shift, :]

    lead = CONV_HALO - (CONV_WIDTH - 1)

    nsub = CONV_ROWS // SUBLANES
    for lc in range(ch // LANES):
        cols = slice(lc * LANES, (lc + 1) * LANES)
        taps = [jnp.broadcast_to(w_ref[k:k + 1, cols], (SUBLANES, LANES)) for k in range(CONV_WIDTH)]
        bias = jnp.broadcast_to(cb_ref[:, cols], (SUBLANES, LANES))

        def conv_step(r, carry, cols=cols, taps=taps, bias=bias):
            base = pl.multiple_of(r * CONV_ROWS, CONV_ROWS)
            acc = [bias] * nsub
            for k in range(CONV_WIDTH):
                a, b = divmod(lead + k, SUBLANES)
                for j in range(nsub):
                    rows = pl.ds(base + SUBLANES * (a + j), SUBLANES)
                    xk = buf_ref[rows, cols] if b == 0 else xs_ref[b, rows, cols]
                    acc[j] = acc[j] + taps[k] * xk
            for j in range(nsub):
                acc_ref[pl.ds(base + SUBLANES * j, SUBLANES), cols] = acc[j]
            return carry

        lax.fori_loop(0, ts // CONV_ROWS, conv_step, 0)

    def norm_step(r, carry):
        base = pl.multiple_of(r * NORM_ROWS, NORM_ROWS)
        a = acc_ref[pl.ds(base, NORM_ROWS), :]
        mu = jnp.mean(a, axis=-1, keepdims=True)
        d = a - mu
        var = jnp.mean(d * d, axis=-1, keepdims=True)
        an = d * lax.rsqrt(var + LN_EPS) * lg_ref[...] + lb_ref[...]
        o_ref[pl.ds(base, NORM_ROWS), :] = (an * jax.nn.sigmoid(an)).astype(o_ref.dtype)
        return carry

    lax.fori_loop(0, ts // NORM_ROWS, norm_step, 0, unroll=4)


def conformer_conv(z, conv_w, conv_b, ln_g, ln_b, *, batch, seq, ts):
    ch = conv_w.shape[1]
    nts = seq // ts
    halo_blocks = ts // CONV_HALO
    kern = functools.partial(_conv_kernel, ts=ts, ch=ch)
    vec = lambda: pl.BlockSpec((1, ch), lambda b, i: (0, 0))
    return pl.pallas_call(
        kern, grid=(batch, nts),
        in_specs=[
            pl.BlockSpec((ts, 2 * ch), lambda b, i: (b * nts + i, 0)),
            pl.BlockSpec((CONV_HALO, 2 * ch),
                         lambda b, i: (jnp.maximum((b * nts + i) * halo_blocks - 1, 0), 0)),
            pl.BlockSpec((CONV_WIDTH, ch), lambda b, i: (0, 0)),
            vec(), vec(), vec(),
        ],
        out_specs=pl.BlockSpec((ts, ch), lambda b, i: (b * nts + i, 0)),
        out_shape=jax.ShapeDtypeStruct((batch * seq, ch), BF16),
        scratch_shapes=[
            pltpu.VMEM((CONV_HALO + ts, ch), F32),
            pltpu.VMEM((SUBLANES, CONV_HALO + ts, ch), F32),
            pltpu.VMEM((ts, ch), F32),
        ],
        compiler_params=_cparams("parallel", "arbitrary"), name="conformer_conv",
    )(z, z, conv_w, conv_b, ln_g, ln_b)


QK_HALO = 16


def _mlstm_kernel(zqk_ref, halo_ref, zv_ref, zo_ref, g_ref, cw_ref, cb_ref, gb_ref, o_ref,
                  qb_ref, c_ref, n_ref, m_ref, *, nh, dh):
    L = MLSTM_CHUNK
    first = pl.program_id(1) == 0

    @pl.when(first)
    def _():
        c_ref[...] = jnp.zeros_like(c_ref)
        n_ref[...] = jnp.zeros_like(n_ref)
        m_ref[...] = jnp.zeros_like(m_ref)

    qb_ref[0:QK_HALO, :] = jnp.where(first, 0.0, halo_ref[...].astype(F32))
    qb_ref[QK_HALO:QK_HALO + L, :] = zqk_ref[...].astype(F32)
    lead = QK_HALO - (MLSTM_QK_CONV - 1)
    acc = jnp.broadcast_to(cb_ref[...], (L, 2 * nh * dh))
    for k in range(MLSTM_QK_CONV):
        acc = acc + cw_ref[k:k + 1, :] * qb_ref[lead + k:lead + k + L, :]
    qk = acc * jax.nn.sigmoid(acc)
    row = lax.broadcasted_iota(jnp.int32, (L, L), 0)
    col = lax.broadcasted_iota(jnp.int32, (L, L), 1)

    g = g_ref[...] + gb_ref[...]
    logf = jax.nn.log_sigmoid(g)
    causal = col <= row
    tri = causal.astype(F32)
    bcum = jnp.dot(tri, logf, preferred_element_type=F32, precision=lax.Precision.HIGHEST)
    g_t = g.T
    b_t = bcum.T
    scale = dh ** -0.5

    for h in range(nh):
        q = qk[:, h * dh:(h + 1) * dh].astype(BF16)
        kf = qk[:, (nh + h) * dh:(nh + h + 1) * dh] * scale
        k = kf.astype(BF16)
        v = zv_ref[:, h * dh:(h + 1) * dh]
        b_col = bcum[:, nh + h:nh + h + 1]
        b_row = b_t[nh + h:nh + h + 1, :]
        i_col = g[:, h:h + 1]
        i_row = g_t[h:h + 1, :]
        m_prev = m_ref[h, 0:1, 0:1]
        c_prev = c_ref[h]
        n_prev = n_ref[h, 0:1, :]

        log_d = jnp.where(causal, b_col - b_row + i_row, -jnp.inf)
        inter = b_col + m_prev
        m_t = jnp.maximum(inter, jnp.max(log_d, axis=-1, keepdims=True))
        s = lax.dot_general(q, k, (((1,), (1,)), ((), ())), preferred_element_type=F32)
        s = s * jnp.exp(log_d - m_t)
        w_int = jnp.exp(inter - m_t)
        num = (jnp.dot(s.astype(BF16), v, preferred_element_type=F32)
               + w_int * jnp.dot(q, c_prev.astype(BF16), preferred_element_type=F32))
        qn = jnp.sum(q.astype(F32) * n_prev, axis=-1, keepdims=True)
        den = jnp.sum(s, axis=-1, keepdims=True) + w_int * qn
        hval = num / jnp.maximum(jnp.abs(den), jnp.exp(-m_t))

        b_last = b_col[L - 1:L, :]
        gk = b_last - b_col + i_col
        m_new = jnp.maximum(b_last + m_prev, jnp.max(gk, axis=0, keepdims=True))
        w_k = jnp.exp(gk - m_new)
        decay = jnp.exp(b_last + m_prev - m_new)
        kw = kf * w_k
        c_ref[h] = decay * c_prev + lax.dot_general(
            kw.astype(BF16), v, (((0,), (0,)), ((), ())), preferred_element_type=F32)
        n_ref[h] = jnp.broadcast_to(decay * n_prev + jnp.sum(kw, axis=0, keepdims=True), (SUBLANES, dh))
        m_ref[h] = jnp.broadcast_to(m_new, (SUBLANES, LANES))

        gate_o = jax.nn.sigmoid(zo_ref[:, h * dh:(h + 1) * dh].astype(F32))
        o_ref[:, h * dh:(h + 1) * dh] = (gate_o * hval).astype(o_ref.dtype)


def mlstm(z, gates, qk_conv_w, qk_conv_b, gate_bias, *, batch, seq, col0):
    L = MLSTM_CHUNK
    nh = MLSTM_HEADS
    w = qk_conv_w.shape[1] // 2
    dh = w // nh
    nc = seq // L
    qk_blk = col0 // (2 * w)
    v_blk = (col0 + 2 * w) // w
    o_blk = v_blk + 1
    kern = functools.partial(_mlstm_kernel, nh=nh, dh=dh)
    return pl.pallas_call(
        kern, grid=(batch, nc),
        in_specs=[
            pl.BlockSpec((L, 2 * w), lambda b, c: (b * nc + c, qk_blk)),
            pl.BlockSpec((QK_HALO, 2 * w),
                         lambda b, c: (jnp.maximum((b * nc + c) * (L // QK_HALO) - 1, 0), qk_blk)),
            pl.BlockSpec((L, w), lambda b, c: (b * nc + c, v_blk)),
            pl.BlockSpec((L, w), lambda b, c: (b * nc + c, o_blk)),
            pl.BlockSpec((L, LANES), lambda b, c: (b * nc + c, 0)),
            pl.BlockSpec((MLSTM_QK_CONV, 2 * w), lambda b, c: (0, 0)),
            pl.BlockSpec((1, 2 * w), lambda b, c: (0, 0)),
            pl.BlockSpec((1, LANES), lambda b, c: (0, 0)),
        ],
        out_specs=pl.BlockSpec((L, w), lambda b, c: (b * nc + c, 0)),
        out_shape=jax.ShapeDtypeStruct((batch * seq, w), BF16),
        scratch_shapes=[
            pltpu.VMEM((QK_HALO + L, 2 * w), F32),
            pltpu.VMEM((nh, dh, dh), F32),
            pltpu.VMEM((nh, SUBLANES, dh), F32),
            pltpu.VMEM((nh, SUBLANES, LANES), F32),
        ],
        compiler_params=_cparams("parallel", "arbitrary"), name="mlstm",
    )(z, z, z, z, gates, qk_conv_w, qk_conv_b, gate_bias)


def _out_proj_kernel(a_ref, b_ref, wa_ref, wb_ref, r_ref, o_ref):
    o_ref[...] = (r_ref[...]
                  + jnp.dot(a_ref[...], wa_ref[...], preferred_element_type=F32)
                  + jnp.dot(b_ref[...], wb_ref[...], preferred_element_type=F32))


def out_proj(a, b, wa, wb, res, *, tm):
    t, d = res.shape
    ka, kb = a.shape[1], b.shape[1]
    return pl.pallas_call(
        _out_proj_kernel, grid=(t // tm,),
        in_specs=[
            pl.BlockSpec((tm, ka), lambda i: (i, 0)),
            pl.BlockSpec((tm, kb), lambda i: (i, 0)),
            pl.BlockSpec((ka, d), lambda i: (0, 0)),
            pl.BlockSpec((kb, d), lambda i: (0, 0)),
            pl.BlockSpec((tm, d), lambda i: (i, 0)),
        ],
        out_specs=pl.BlockSpec((tm, d), lambda i: (i, 0)),
        out_shape=jax.ShapeDtypeStruct((t, d), F32),
        compiler_params=_cparams("parallel"), name="out_proj",
    )(a, b, wa, wb, res)


def _ffn_kernel(x_ref, g_ref, w1_ref, w3_ref, w2_ref, o_ref, xn_ref):
    f = pl.program_id(1)

    @pl.when(f == 0)
    def _():
        x = x_ref[...]
        xn_ref[...] = _rms(x, g_ref[...]).astype(BF16)
        o_ref[...] = x

    xn = xn_ref[...]
    a = jnp.dot(xn, w1_ref[...], preferred_element_type=F32)
    b = jnp.dot(xn, w3_ref[...], preferred_element_type=F32)
    hid = (a * jax.nn.sigmoid(a) * b).astype(BF16)
    o_ref[...] += jnp.dot(hid, w2_ref[...], preferred_element_type=F32)


def ffn(x, g, w1, w3, w2, *, tm, tf):
    t, d = x.shape
    dff = w1.shape[1]
    return pl.pallas_call(
        _ffn_kernel, grid=(t // tm, dff // tf),
        in_specs=[
            pl.BlockSpec((tm, d), lambda i, f: (i, 0)),
            pl.BlockSpec((1, d), lambda i, f: (0, 0)),
            pl.BlockSpec((d, tf), lambda i, f: (0, f)),
            pl.BlockSpec((d, tf), lambda i, f: (0, f)),
            pl.BlockSpec((tf, d), lambda i, f: (f, 0)),
        ],
        out_specs=pl.BlockSpec((tm, d), lambda i, f: (i, 0)),
        out_shape=jax.ShapeDtypeStruct((t, d), F32),
        scratch_shapes=[pltpu.VMEM((tm, d), BF16)],
        compiler_params=_cparams("parallel", "arbitrary"), name="ffn",
    )(x, g, w1, w3, w2)


def _attn_kernel(q_ref, k_ref, v_ref, o_ref, qf_ref, kf_ref, vf_ref, qd_ref, kd_ref, vd_ref,
                 od_ref, ld_ref, on_ref, ln_ref, *, seq, dh):
    T = ATTN_BLOCK
    nblk = seq // T
    qf_ref[...] = q_ref[...].astype(F32) * (dh ** -0.5)
    kf_ref[...] = k_ref[...].astype(F32)
    vf_ref[...] = v_ref[...].astype(F32)
    kd_ref[0:T, :] = jnp.zeros((T, dh), BF16)
    vd_ref[0:T, :] = jnp.zeros((T, dh), BF16)

    qi = lax.broadcasted_iota(jnp.int32, (T, 2 * T), 0)
    ki = lax.broadcasted_iota(jnp.int32, (T, 2 * T), 1)
    dist = T + qi - ki
    band = (dist >= 0) & (dist <= T)

    for g, dil in enumerate(DILATIONS):
        ls = seq // dil
        nb = ls // T
        for r in range(dil):
            rows = pl.ds(r, ls, stride=dil) if dil > 1 else pl.ds(0, ls)
            qd_ref[r * ls:(r + 1) * ls, :] = qf_ref[rows, :].astype(BF16)
            kd_ref[T + r * ls:T + (r + 1) * ls, :] = kf_ref[rows, :].astype(BF16)
            vd_ref[T + r * ls:T + (r + 1) * ls, :] = vf_ref[rows, :].astype(BF16)

        def block(n, carry):
            base = pl.multiple_of(n * T, T)
            qb = qd_ref[pl.ds(base, T), :]
            kb = kd_ref[pl.ds(base, 2 * T), :]
            vb = vd_ref[pl.ds(base, 2 * T), :]
            s = lax.dot_general(qb, kb, (((1,), (1,)), ((), ())), preferred_element_type=F32)
            kmin = jnp.where(n % nb == 0, T, 0)
            s = jnp.where(band & (ki >= kmin), s, -jnp.inf)
            m = jnp.max(s, axis=-1, keepdims=True)
            p = jnp.exp(s - m)
            l = jnp.sum(p, axis=-1, keepdims=True)
            o = jnp.dot(p.astype(BF16), vb, preferred_element_type=F32) / l
            od_ref[pl.ds(base, T), :] = o
            ld_ref[pl.ds(base, T), :] = jnp.broadcast_to(m + jnp.log(l), (T, dh))
            return carry

        lax.fori_loop(0, nblk, block, 0, unroll=8)

        for r in range(dil):
            rows = pl.ds(r, ls, stride=dil) if dil > 1 else pl.ds(0, ls)
            on_ref[g, rows, :] = od_ref[r * ls:(r + 1) * ls, :]
            ln_ref[g, rows, :] = ld_ref[r * ls:(r + 1) * ls, :]

    def merge(n, carry):
        rows = pl.ds(pl.multiple_of(n * T, T), T)
        lses = [ln_ref[g, rows, :] for g in range(len(DILATIONS))]
        mx = functools.reduce(jnp.maximum, lses)
        ws = [jnp.exp(l - mx) for l in lses]
        tot = functools.reduce(lambda a, b: a + b, ws)
        acc = ws[0] * on_ref[0, rows, :]
        for g in range(1, len(DILATIONS)):
            acc = acc + ws[g] * on_ref[g, rows, :]
        o_ref[rows, :] = (acc / tot).astype(o_ref.dtype)
        return carry

    lax.fori_loop(0, nblk, merge, 0)


def dilated_attention(z, *, batch, seq, width):
    nh = ATTN_HEADS
    dh = width // nh
    ng = len(DILATIONS)
    kern = functools.partial(_attn_kernel, seq=seq, dh=dh)
    blk = lambda off: pl.BlockSpec((seq, dh), lambda b, h: (b, off + h))
    return pl.pallas_call(
        kern, grid=(batch, nh),
        in_specs=[blk(0), blk(nh), blk(2 * nh)],
        out_specs=pl.BlockSpec((seq, dh), lambda b, h: (b, h)),
        out_shape=jax.ShapeDtypeStruct((batch * seq, width), BF16),
        scratch_shapes=[
            pltpu.VMEM((seq, dh), F32), pltpu.VMEM((seq, dh), F32), pltpu.VMEM((seq, dh), F32),
            pltpu.VMEM((seq, dh), BF16),
            pltpu.VMEM((ATTN_BLOCK + seq, dh), BF16), pltpu.VMEM((ATTN_BLOCK + seq, dh), BF16),
            pltpu.VMEM((seq, dh), F32), pltpu.VMEM((seq, dh), F32),
            pltpu.VMEM((ng, seq, dh), F32), pltpu.VMEM((ng, seq, dh), F32),
        ],
        compiler_params=_cparams("parallel", "parallel"), name="dilated_attention",
    )(z, z, z)


S5_SEGS = SUBLANES
S5_SEG = 64
S5_PITCH = S5_SEG + 4
S5_TILE = S5_SEGS * S5_SEG
S5_ROWS = S5_SEGS * S5_PITCH
S5_UNROLL = S5_SEG
S5_CHUNK_IN = 256
S5_SLABS = S5_CHUNK_IN // SSM_GROUP * SSM_STATE // LANES


def _s5_kernel(u_ref, bre_ref, bim_ref, cre_ref, cim_ref, lre_ref, lim_ref, dsk_ref, wg_ref, bg_ref,
               o_ref, ug_ref, sre_ref, sim_ref, cyre_ref, cyim_ref, y_ref, *, nchunk):
    first = pl.program_id(1) == 0

    @pl.when(first)
    def _():
        cyre_ref[...] = jnp.zeros_like(cyre_ref)
        cyim_ref[...] = jnp.zeros_like(cyim_ref)
        ug_ref[...] = jnp.zeros_like(ug_ref)

    for i in range(S5_SEGS):
        ug_ref[i * S5_PITCH:i * S5_PITCH + S5_SEG, :] = u_ref[i * S5_SEG:(i + 1) * S5_SEG, :].astype(F32)

    seg_id = lax.broadcasted_iota(jnp.int32, (S5_SEGS, LANES), 0)
    for c in range(nchunk):
        par = (c % 2) * S5_SLABS
        ucols = slice(c * S5_CHUNK_IN, (c + 1) * S5_CHUNK_IN)
        ub = ug_ref[:, ucols].astype(BF16)
        bu_re = jnp.dot(ub, bre_ref[c], preferred_element_type=F32)
        bu_im = jnp.dot(ub, bim_ref[c], preferred_element_type=F32)
        for s in range(S5_SLABS):
            sre_ref[par + s] = bu_re[:, s * LANES:(s + 1) * LANES]
            sim_ref[par + s] = bu_im[:, s * LANES:(s + 1) * LANES]

        lam_re = [jnp.broadcast_to(lre_ref[c, :, s * LANES:(s + 1) * LANES], (S5_SEGS, LANES))
                  for s in range(S5_SLABS)]
        lam_im = [jnp.broadcast_to(lim_ref[c, :, s * LANES:(s + 1) * LANES], (S5_SEGS, LANES))
                  for s in range(S5_SLABS)]

        def step(j, st, store):
            rows = pl.ds(j, S5_SEGS, stride=S5_PITCH)
            new = []
            for s in range(S5_SLABS):
                pr, pi = st[2 * s], st[2 * s + 1]
                nr = lam_re[s] * pr - lam_im[s] * pi + sre_ref[par + s, rows, :]
                ni = lam_re[s] * pi + lam_im[s] * pr + sim_ref[par + s, rows, :]
                if store:
                    sre_ref[par + s, rows, :] = nr
                    sim_ref[par + s, rows, :] = ni
                new += [nr, ni]
            return tuple(new)

        zero = tuple(jnp.zeros((S5_SEGS, LANES), F32) for _ in range(2 * S5_SLABS))
        ends = lax.fori_loop(0, S5_SEG, lambda j, st: step(j, st, False), zero, unroll=S5_UNROLL)

        pw_re, pw_im = [l[0:1] for l in lam_re], [l[0:1] for l in lam_im]
        for _ in range(int(math.log2(S5_SEG))):
            pw_re, pw_im = ([a * a - b * b for a, b in zip(pw_re, pw_im)],
                            [2.0 * a * b for a, b in zip(pw_re, pw_im)])
        init = []
        for s in range(S5_SLABS):
            cols = slice(s * LANES, (s + 1) * LANES)
            cr, ci = cyre_ref[c, 0:1, cols], cyim_ref[c, 0:1, cols]
            in_re = jnp.zeros((S5_SEGS, LANES), F32)
            in_im = jnp.zeros((S5_SEGS, LANES), F32)
            for i in range(S5_SEGS):
                in_re = jnp.where(seg_id == i, cr, in_re)
                in_im = jnp.where(seg_id == i, ci, in_im)
                er, ei = ends[2 * s][i:i + 1], ends[2 * s + 1][i:i + 1]
                cr, ci = (pw_re[s] * cr - pw_im[s] * ci + er, pw_re[s] * ci + pw_im[s] * cr + ei)
            cyre_ref[c, :, cols] = jnp.broadcast_to(cr, (SUBLANES, LANES))
            cyim_ref[c, :, cols] = jnp.broadcast_to(ci, (SUBLANES, LANES))
            init += [in_re, in_im]

        lax.fori_loop(0, S5_SEG, lambda j, st: step(j, st, True), tuple(init), unroll=S5_UNROLL)

        st_re = jnp.concatenate([sre_ref[par + s] for s in range(S5_SLABS)], axis=1).astype(BF16)
        st_im = jnp.concatenate([sim_ref[par + s] for s in range(S5_SLABS)], axis=1).astype(BF16)
        y_ref[:, ucols] = (jnp.dot(st_re, cre_ref[c], preferred_element_type=F32)
                           + jnp.dot(st_im, cim_ref[c], preferred_element_type=F32))

    for i in range(S5_SEGS):
        rows = slice(i * S5_PITCH, i * S5_PITCH + S5_SEG)
        y = y_ref[rows, :] + dsk_ref[...] * ug_ref[rows, :]
        y = jax.nn.gelu(y)
        gate = jnp.dot(y.astype(BF16), wg_ref[...], preferred_element_type=F32) + bg_ref[...]
        o_ref[i * S5_SEG:(i + 1) * S5_SEG, :] = (y * jax.nn.sigmoid(gate)).astype(o_ref.dtype)


def s5(z, b_re, b_im, c_re, c_im, lam_re, lam_im, d_skip, w_glu, b_glu, *, batch, seq, col_blk):
    nchunk, _, nstate = b_re.shape
    ch = nchunk * S5_CHUNK_IN
    nt = seq // S5_TILE
    kern = functools.partial(_s5_kernel, nchunk=nchunk)
    full = lambda shape: pl.BlockSpec(shape, lambda b, i: (0,) * len(shape))
    return pl.pallas_call(
        kern, grid=(batch, nt),
        in_specs=[
            pl.BlockSpec((S5_TILE, ch), lambda b, i: (b * nt + i, col_blk)),
            full(b_re.shape), full(b_im.shape), full(c_re.shape), full(c_im.shape),
            full(lam_re.shape), full(lam_im.shape), full((1, ch)), full((ch, ch)), full((1, ch)),
        ],
        out_specs=pl.BlockSpec((S5_TILE, ch), lambda b, i: (b * nt + i, 0)),
        out_shape=jax.ShapeDtypeStruct((batch * seq, ch), BF16),
        scratch_shapes=[
            pltpu.VMEM((S5_ROWS, ch), F32),
            pltpu.VMEM((2 * S5_SLABS, S5_ROWS, LANES), F32),
            pltpu.VMEM((2 * S5_SLABS, S5_ROWS, LANES), F32),
            pltpu.VMEM((nchunk, SUBLANES, nstate), F32),
            pltpu.VMEM((nchunk, SUBLANES, nstate), F32),
            pltpu.VMEM((S5_ROWS, ch), F32),
        ],
        compiler_params=_cparams("parallel", "arbitrary"), name="s5",
    )(z, b_re, b_im, c_re, c_im, lam_re, lam_im, d_skip, w_glu, b_glu)


def s5_params(lam_re, lam_im, log_dt, b_re, b_im, c_re, c_im):
    ng, ns = lam_re.shape
    gpc = S5_CHUNK_IN // SSM_GROUP
    nchunk = ng // gpc
    dt = jnp.exp(log_dt)[:, None]
    mag = jnp.exp(lam_re * dt)
    lb_re, lb_im = mag * jnp.cos(lam_im * dt), mag * jnp.sin(lam_im * dt)
    den = lam_re * lam_re + lam_im * lam_im
    f_re = ((lb_re - 1.0) * lam_re + lb_im * lam_im) / den
    f_im = (lb_im * lam_re - (lb_re - 1.0) * lam_im) / den
    bb_re = f_re[..., None] * b_re - f_im[..., None] * b_im
    bb_im = f_re[..., None] * b_im + f_im[..., None] * b_re
    eye = jnp.eye(gpc, dtype=F32)

    def pack_b(m):
        m = m.reshape(nchunk, gpc, ns, SSM_GROUP)
        return jnp.einsum('cgpk,gh->cgkhp', m, eye).reshape(nchunk, gpc * SSM_GROUP, gpc * ns).astype(BF16)

    def pack_c(m):
        m = m.reshape(nchunk, gpc, SSM_GROUP, ns)
        return jnp.einsum('cgkp,gh->cgphk', m, eye).reshape(nchunk, gpc * ns, gpc * SSM_GROUP).astype(BF16)

    return (pack_b(bb_re), pack_b(bb_im), pack_c(c_re), pack_c(-c_im),
            lb_re.reshape(nchunk, 1, gpc * ns), lb_im.reshape(nchunk, 1, gpc * ns))


def _pack_bf16_pair(lo, hi):
    lo_bits = lax.bitcast_convert_type(lo.astype(BF16).astype(F32), jnp.uint32) >> 16
    hi_bits = lax.bitcast_convert_type(hi.astype(BF16).astype(F32), jnp.uint32) & jnp.uint32(0xFFFF0000)
    return lo_bits | hi_bits


def _unpack_bf16_pair(w):
    lo = lax.bitcast_convert_type(w << 16, F32)
    hi = lax.bitcast_convert_type(w & jnp.uint32(0xFFFF0000), F32)
    return lo, hi


def _split3(v):
    hi = v.astype(BF16)
    r1 = v - hi.astype(F32)
    mid = r1.astype(BF16)
    lo = (r1 - mid.astype(F32)).astype(BF16)
    return hi, mid, lo


def _out_proj_route_kernel(a_ref, b_ref, wa_ref, wb_ref, r_ref, g_ref, wr_ref, x_ref, xp_ref, eid_ref, gate_ref):
    x = (r_ref[...]
         + jnp.dot(a_ref[...], wa_ref[...], preferred_element_type=F32)
         + jnp.dot(b_ref[...], wb_ref[...], preferred_element_type=F32))
    x_ref[...] = x
    xn = _rms(x, g_ref[...])
    half = xn.shape[1] // 2
    xp_ref[...] = _pack_bf16_pair(xn[:, :half], xn[:, half:])
    xh, xm, xl = _split3(xn)
    ph = jnp.dot(xh, wr_ref[...], preferred_element_type=F32)
    pm = jnp.dot(xm, wr_ref[:, :2 * LANES], preferred_element_type=F32)
    pl_ = jnp.dot(xl, wr_ref[:, :LANES], preferred_element_type=F32)
    logits = (ph[:, :LANES] + (ph[:, LANES:2 * LANES] + pm[:, :LANES])
              + (ph[:, 2 * LANES:] + pm[:, LANES:] + pl_))
    lane = lax.broadcasted_iota(jnp.int32, logits.shape, 1)
    logits = jnp.where(lane < N_EXPERTS, logits, -jnp.inf)
    v1 = jnp.max(logits, axis=-1, keepdims=True)
    i1 = jnp.min(jnp.where(logits == v1, lane, LANES), axis=-1, keepdims=True)
    rest = jnp.where(lane == i1, -jnp.inf, logits)
    v2 = jnp.max(rest, axis=-1, keepdims=True)
    i2 = jnp.min(jnp.where(rest == v2, lane, LANES), axis=-1, keepdims=True)
    e2 = jnp.exp(v2 - v1)
    g1 = 1.0 / (1.0 + e2)
    g2 = e2 / (1.0 + e2)
    eid_ref[...] = jnp.where(lane == 0, i1, jnp.where(lane == 1, i2, 0))
    gate_ref[...] = jnp.where(lane == 0, g1, jnp.where(lane == 1, g2, 0.0))


def out_proj_route(a, b, wa, wb, res, g, w_router, *, tm):
    t, d = res.shape
    ka, kb = a.shape[1], b.shape[1]
    row = lambda n: pl.BlockSpec((tm, n), lambda i: (i, 0))
    full = lambda r, c: pl.BlockSpec((r, c), lambda i: (0, 0))
    return pl.pallas_call(
        _out_proj_route_kernel, grid=(t // tm,),
        in_specs=[row(ka), row(kb), full(ka, d), full(kb, d), row(d), full(1, d), full(d, 3 * LANES)],
        out_specs=[row(d), row(d // 2), row(LANES), row(LANES)],
        out_shape=[jax.ShapeDtypeStruct((t, d), F32), jax.ShapeDtypeStruct((t, d // 2), jnp.uint32),
                   jax.ShapeDtypeStruct((t, LANES), jnp.int32), jax.ShapeDtypeStruct((t, LANES), F32)],
        compiler_params=_cparams("parallel"), name="out_proj_route",
    )(a, b, wa, wb, res, g, w_router)


EXPERT_TM = 1024
EXPERT_TF = 512
COMBINE_TC = 512


def _gather_rows_per_step(tm, nf):
    return -(-tm // ((nf - 1) * SUBLANES)) * SUBLANES


def _expert_kernel(te_ref, tv_ref, src_ref, dst_ref, xp_hbm, w1_ref, w3_ref, w2_ref, ys_hbm,
                   gbuf, xb, acc, obuf, gsem, ssem, *, tm, nf, gr):
    i = pl.program_id(0)
    f = pl.program_id(1)
    ntiles = pl.num_programs(0)
    gb = gr * (nf - 1)
    half = xb.shape[1] // 2

    def gather(tile, r):
        return pltpu.make_async_copy(xp_hbm.at[pl.ds(src_ref[tile * gb + r], 1), :],
                                     gbuf.at[pl.ds(r, 1), :], gsem)

    def scatter(tile, r):
        return pltpu.make_async_copy(obuf.at[pl.ds(r, 1), :],
                                     ys_hbm.at[pl.ds(dst_ref[(tile + 1) * gb + r], 1), :], ssem)

    def for_rows(n, fn):
        def body(r, carry):
            fn(r)
            return carry
        lax.fori_loop(0, n, body, 0, unroll=8)

    @pl.when((i == 0) & (f == 0))
    def _():
        obuf[...] = jnp.zeros_like(obuf)
        for_rows(gb, lambda r: gather(0, r).start())

    @pl.when(f == 0)
    def _():
        for_rows(gb, lambda r: gather(i, r).wait())
        lo, hi = _unpack_bf16_pair(gbuf[0:tm, :])
        xb[:, :half] = lo.astype(BF16)
        xb[:, half:] = hi.astype(BF16)
        acc[...] = jnp.zeros_like(acc)

    def swiglu_rows(rows):
        x = xb[0:rows, :]
        a = jnp.dot(x, w1_ref[0].astype(BF16), preferred_element_type=F32)
        b = jnp.dot(x, w3_ref[0].astype(BF16), preferred_element_type=F32)
        hid = (a * jax.nn.sigmoid(a) * b).astype(BF16)
        acc[0:rows, :] += jnp.dot(hid, w2_ref[0].astype(BF16), preferred_element_type=F32)

    def step(rows, moving):
        if moving:
            for r in range(gr):
                gather(i + 1, f * gr + r).start()
                scatter(i - 1, f * gr + r).start()
        if rows:
            swiglu_rows(rows)

    used = tv_ref[i]
    is_last = f == nf - 1
    quarter = tm // 4
    conds = [(used > q * quarter) & (used <= (q + 1) * quarter) for q in range(4)] + [used == 0]
    for cond, rows in zip(conds, (quarter, 2 * quarter, 3 * quarter, tm, 0)):
        pl.when(cond & jnp.logical_not(is_last))(functools.partial(step, rows, True))
        pl.when(cond & is_last)(functools.partial(step, rows, False))

    @pl.when(is_last)
    def _():
        for_rows(gb, lambda r: scatter(i - 1, r).wait())
        y = acc[...]
        obuf[0:tm, :] = _pack_bf16_pair(y[:, :half], y[:, half:])

    @pl.when((i == ntiles - 1) & is_last)
    def _():
        for_rows(gb, lambda r: gather(i + 1, r).wait())
        for_rows(gb, lambda r: scatter(i, r).start())
        for_rows(gb, lambda r: scatter(i, r).wait())


def expert_ffn(tile_e, tile_v, src, dst, xp, w1, w3, w2, *, n_out_rows):
    tm, tf = EXPERT_TM, EXPERT_TF
    dh = xp.shape[1]
    d = 2 * dh
    dff = w1.shape[2]
    nf = dff // tf
    gr = _gather_rows_per_step(tm, nf)
    ntiles = tile_e.shape[0]
    fsel = lambda i, f, te, tv: jnp.where(tv[i] > 0, f, nf - 1)
    kern = functools.partial(_expert_kernel, tm=tm, nf=nf, gr=gr)
    return pl.pallas_call(
        kern,
        grid_spec=pltpu.PrefetchScalarGridSpec(
            num_scalar_prefetch=4, grid=(ntiles, nf),
            in_specs=[
                pl.BlockSpec(memory_space=pl.ANY),
                pl.BlockSpec((1, d, tf), lambda i, f, te, tv, s, t: (te[i], 0, fsel(i, f, te, tv))),
                pl.BlockSpec((1, d, tf), lambda i, f, te, tv, s, t: (te[i], 0, fsel(i, f, te, tv))),
                pl.BlockSpec((1, tf, d), lambda i, f, te, tv, s, t: (te[i], fsel(i, f, te, tv), 0)),
            ],
            out_specs=pl.BlockSpec(memory_space=pl.ANY),
            scratch_shapes=[
                pltpu.VMEM((gr * (nf - 1), dh), jnp.uint32),
                pltpu.VMEM((tm, d), BF16),
                pltpu.VMEM((tm, d), F32),
                pltpu.VMEM((gr * (nf - 1), dh), jnp.uint32),
                pltpu.SemaphoreType.DMA(()),
                pltpu.SemaphoreType.DMA(()),
            ]),
        out_shape=jax.ShapeDtypeStruct((n_out_rows, dh), jnp.uint32),
        compiler_params=_cparams("arbitrary", "arbitrary"), name="expert_ffn",
    )(tile_e, tile_v, src, dst, xp, w1, w3, w2)


def _combine_kernel(x_ref, gate_ref, g_ref, y1_ref, y2_ref, o_ref):
    gates = gate_ref[...]
    half = x_ref.shape[1] // 2
    lo1, hi1 = _unpack_bf16_pair(y1_ref[...])
    lo2, hi2 = _unpack_bf16_pair(y2_ref[...])
    g1, g2 = gates[:, 0:1], gates[:, 1:2]
    x = x_ref[...]
    x = jnp.concatenate([x[:, :half] + g1 * lo1 + g2 * lo2, x[:, half:] + g1 * hi1 + g2 * hi2], axis=1)
    o_ref[...] = _rms(x, g_ref[...])


def combine(x, gates, g, ys):
    t, d = x.shape
    tc = COMBINE_TC
    return pl.pallas_call(
        _combine_kernel, grid=(t // tc,),
        in_specs=[
            pl.BlockSpec((tc, d), lambda i: (i, 0)),
            pl.BlockSpec((tc, LANES), lambda i: (i, 0)),
            pl.BlockSpec((1, d), lambda i: (0, 0)),
            pl.BlockSpec((tc, d // 2), lambda i: (i, 0)),
            pl.BlockSpec((tc, d // 2), lambda i: (t // tc + i, 0)),
        ],
        out_specs=pl.BlockSpec((tc, d), lambda i: (i, 0)),
        out_shape=jax.ShapeDtypeStruct((t, d), F32),
        compiler_params=_cparams("parallel"), name="moe_combine",
    )(x, gates, g, ys, ys)


def route_slots(eid, *, dff):
    tm = EXPERT_TM
    nf = dff // EXPERT_TF
    gb = _gather_rows_per_step(tm, nf) * (nf - 1)
    t = eid.shape[0]
    na = 2 * t
    e_flat = eid.reshape(na)
    onehot = (e_flat[:, None] == jnp.arange(N_EXPERTS, dtype=jnp.int32)[None, :]).astype(jnp.int32)
    csum = jnp.cumsum(onehot, axis=0)
    rank = jnp.sum((csum - onehot) * onehot, axis=1)
    counts = csum[-1]
    padded = (counts + tm - 1) // tm * tm
    ends = jnp.cumsum(padded)
    slot = jnp.sum(onehot * (ends - padded)[None, :], axis=1) + rank
    ntiles = -(-(na + N_EXPERTS * tm) // tm)
    ns = ntiles * tm
    asg = jnp.full((ns,), -1, jnp.int32).at[slot].set(jnp.arange(na, dtype=jnp.int32))
    is_pad = asg < 0
    tok = jnp.where(is_pad, 0, asg // 2)
    pad_rank = jnp.cumsum(is_pad.astype(jnp.int32)) - 1
    dst = jnp.where(is_pad, na + gb + pad_rank, (asg % 2) * t + asg // 2).reshape(ntiles, tm)
    over = (na + gb + (ns - na) + jnp.arange(ntiles * (gb - tm), dtype=jnp.int32)).reshape(ntiles, gb - tm)
    dst = jnp.concatenate([na + jnp.arange(gb, dtype=jnp.int32),
                           jnp.concatenate([dst, over], axis=1).reshape(-1)])
    src = jnp.pad(tok.reshape(ntiles, tm), ((0, 1), (0, gb - tm))).reshape(-1)
    starts = jnp.arange(ntiles, dtype=jnp.int32) * tm
    tile_e = jnp.sum((starts[:, None] >= ends[None, :]).astype(jnp.int32), axis=1)
    used_end = jnp.concatenate([ends - padded + counts, jnp.zeros((1,), jnp.int32)])
    tile_v = jnp.clip(used_end[tile_e] - starts, 0, tm)
    last_e = jnp.max(jnp.where(tile_v > 0, tile_e, 0))
    tile_e = jnp.where(tile_v > 0, tile_e, last_e)
    n_out_rows = na + gb + (ns - na) + ntiles * (gb - tm)
    n_out_rows = -(-n_out_rows // COMBINE_TC) * COMBINE_TC
    return tile_e, tile_v, src, dst, n_out_rows


def kernel(x, e_norm_mix, e_w_in, e_conv_w, e_conv_b, e_ln_g, e_ln_b, e_qk_conv_w, e_qk_conv_b, e_b_i, e_b_f, e_w_out, e_norm_ffn, e_ffn_w1, e_ffn_w3, e_ffn_w2, o_norm_mix, o_w_in, o_lam_re, o_lam_im, o_log_dt, o_b_re, o_b_im, o_c_re, o_c_im, o_d_skip, o_w_glu, o_b_glu, o_w_out, o_norm_ffn, o_router, o_exp_w1, o_exp_w3, o_exp_w2, final_norm):
    batch, seq, d = x.shape
    t = batch * seq
    xt = x.reshape(t, d)
    row = lambda v: v.reshape(1, -1)
    assert e_norm_mix.shape[0] == 1 and o_norm_mix.shape[0] == 1, "one even and one odd layer"

    ch = e_conv_w.shape[2]
    mw = e_qk_conv_w.shape[2] // 2
    nh = MLSTM_HEADS
    main_cols = 2 * ch + 4 * mw
    w_in = e_w_in[0].astype(BF16)
    w_gates = jnp.pad(w_in[:, main_cols:], ((0, 0), (0, LANES - 2 * nh)))
    z0, gates0 = norm_matmul(xt, row(e_norm_mix[0]), w_in, w_gates, tm=512, tn=1024, n_out=main_cols)
    out_a = conformer_conv(z0, e_conv_w[0], row(e_conv_b[0]), row(e_ln_g[0]), row(e_ln_b[0]),
                           batch=batch, seq=seq, ts=512)
    gate_bias = jnp.pad(jnp.concatenate([e_b_i[0], e_b_f[0]]), (0, LANES - 2 * nh)).reshape(1, LANES)
    out_b = mlstm(z0, gates0, e_qk_conv_w[0], row(e_qk_conv_b[0]), gate_bias, batch=batch, seq=seq, col0=2 * ch)
    w_out = e_w_out[0].astype(BF16)
    x1 = out_proj(out_a, out_b, w_out[:ch], w_out[ch:], xt, tm=512)
    x2 = ffn(x1, row(e_norm_ffn[0]), e_ffn_w1[0].astype(BF16), e_ffn_w3[0].astype(BF16),
             e_ffn_w2[0].astype(BF16), tm=1024, tf=512)

    aw = o_w_in.shape[2] - o_d_skip.shape[1]
    aw //= 3
    sch = o_d_skip.shape[1]
    z1 = norm_matmul(x2, row(o_norm_mix[0]), o_w_in[0].astype(BF16), tm=512, tn=1024)
    out_c = dilated_attention(z1, batch=batch, seq=seq, width=aw)
    sp = s5_params(o_lam_re[0], o_lam_im[0], o_log_dt[0], o_b_re[0], o_b_im[0], o_c_re[0], o_c_im[0])
    out_d = s5(z1, *sp, row(o_d_skip[0]), o_w_glu[0].astype(BF16), row(o_b_glu[0]),
               batch=batch, seq=seq, col_blk=3 * aw // sch)
    w_out1 = o_w_out[0].astype(BF16)
    w_router = jnp.concatenate(
        [jnp.pad(p, ((0, 0), (0, LANES - N_EXPERTS))) for p in _split3(o_router[0])], axis=1)
    x3, xp3, eid, gates = out_proj_route(out_c, out_d, w_out1[:aw], w_out1[aw:], x2, row(o_norm_ffn[0]),
                                         w_router, tm=512)
    tile_e, tile_v, src, dst, n_out_rows = route_slots(eid[:, :2], dff=o_exp_w1.shape[3])
    ys = expert_ffn(tile_e, tile_v, src, dst, xp3, o_exp_w1[0], o_exp_w3[0], o_exp_w2[0], n_out_rows=n_out_rows)
    out = combine(x3, gates, row(final_norm), ys)
    return out.reshape(batch, seq, d)
```

```python
import functools
import math

import jax
import jax.numpy as jnp
from jax import lax
from jax.experimental import pallas as pl
from jax.experimental.pallas import tpu as pltpu

F32 = jnp.float32
BF16 = jnp.bfloat16

RMS_EPS = 1e-6
LN_EPS = 1e-5
CONV_WIDTH = 31
MLSTM_HEADS = 8
MLSTM_QK_CONV = 4
MLSTM_CHUNK = 128
ATTN_HEADS = 8
ATTN_BLOCK = 128
DILATIONS = (1, 4, 16)
SSM_GROUP = 16
SSM_STATE = 64
N_EXPERTS = 8
LANES = 128
SUBLANES = 8
VMEM_LIMIT = 56 * 1024 * 1024


def _cparams(*sem):
    return pltpu.CompilerParams(dimension_semantics=sem, vmem_limit_bytes=VMEM_LIMIT)


def _rms(x, g):
    return x * lax.rsqrt(jnp.mean(x * x, axis=-1, keepdims=True) + RMS_EPS) * g


def _norm_matmul_kernel(x_ref, g_ref, w_ref, o_ref, *, tn):
    xn = _rms(x_ref[...], g_ref[...]).astype(BF16)
    for j in range(o_ref.shape[1] // tn):
        cols = slice(j * tn, (j + 1) * tn)
        o_ref[:, cols] = jnp.dot(xn, w_ref[:, cols], preferred_element_type=F32).astype(o_ref.dtype)


def _norm_matmul_aux_kernel(x_ref, g_ref, w_ref, wa_ref, o_ref, oa_ref, *, tn):
    xn = _rms(x_ref[...], g_ref[...]).astype(BF16)
    oa_ref[...] = jnp.dot(xn, wa_ref[...], preferred_element_type=F32)
    for j in range(o_ref.shape[1] // tn):
        cols = slice(j * tn, (j + 1) * tn)
        o_ref[:, cols] = jnp.dot(xn, w_ref[:, cols], preferred_element_type=F32).astype(o_ref.dtype)


def norm_matmul(x, g, w, w_aux=None, *, tm, tn, n_out=None):
    t, d = x.shape
    n = w.shape[1] if n_out is None else n_out
    resident = lambda shape: pl.BlockSpec(shape, lambda i: (0, 0), pipeline_mode=pl.Buffered(1))
    x_spec = pl.BlockSpec((tm, d), lambda i: (i, 0))
    g_spec = pl.BlockSpec((1, d), lambda i: (0, 0))
    o_spec = pl.BlockSpec((tm, n), lambda i: (i, 0))
    if w_aux is None:
        return pl.pallas_call(
            functools.partial(_norm_matmul_kernel, tn=tn), grid=(t // tm,),
            in_specs=[x_spec, g_spec, resident(w.shape)], out_specs=o_spec,
            out_shape=jax.ShapeDtypeStruct((t, n), BF16),
            compiler_params=_cparams("parallel"), name="norm_matmul")(x, g, w)
    na = w_aux.shape[1]
    return pl.pallas_call(
        functools.partial(_norm_matmul_aux_kernel, tn=tn), grid=(t // tm,),
        in_specs=[x_spec, g_spec, resident(w.shape), resident(w_aux.shape)],
        out_specs=[o_spec, pl.BlockSpec((tm, na), lambda i: (i, 0))],
        out_shape=[jax.ShapeDtypeStruct((t, n), BF16), jax.ShapeDtypeStruct((t, na), F32)],
        compiler_params=_cparams("parallel"), name="norm_matmul_aux")(x, g, w, w_aux)


CONV_HALO = 32
CONV_ROWS = 64
NORM_ROWS = 32


def _conv_kernel(u_ref, halo_ref, w_ref, cb_ref, lg_ref, lb_ref, o_ref, buf_ref, xs_ref, acc_ref, *, ts, ch):
    first = pl.program_id(1) == 0

    def glu(u):
        u = u.astype(F32)
        return u[:, :ch] * jax.nn.sigmoid(u[:, ch:])

    buf_ref[0:CONV_HALO, :] = jnp.where(first, 0.0, glu(halo_ref[...]))
    buf_ref[CONV_HALO:CONV_HALO + ts, :] = glu(u_ref[...])
    n_shift = ts + CONV_HALO - SUBLANES
    for b in range(1, SUBLANES):
        xs_ref[b, 0:n_shift, :] = buf_ref[b:b + n_shift, :]

    lead = CONV_HALO - (CONV_WIDTH - 1)

    nsub = CONV_ROWS // SUBLANES
    for lc in range(ch // LANES):
        cols = slice(lc * LANES, (lc + 1) * LANES)
        taps = [jnp.broadcast_to(w_ref[k:k + 1, cols], (SUBLANES, LANES)) for k in range(CONV_WIDTH)]
        bias = jnp.broadcast_to(cb_ref[:, cols], (SUBLANES, LANES))

        def conv_step(r, carry, cols=cols, taps=taps, bias=bias):
            base = pl.multiple_of(r * CONV_ROWS, CONV_ROWS)
            acc = [bias] * nsub
            for k in range(CONV_WIDTH):
                a, b = divmod(lead + k, SUBLANES)
                for j in range(nsub):
                    rows = pl.ds(base + SUBLANES * (a + j), SUBLANES)
                    xk = buf_ref[rows, cols] if b == 0 else xs_ref[b, rows, cols]
                    acc[j] = acc[j] + taps[k] * xk
            for j in range(nsub):
                acc_ref[pl.ds(base + SUBLANES * j, SUBLANES), cols] = acc[j]
            return carry

        lax.fori_loop(0, ts // CONV_ROWS, conv_step, 0)

    def norm_step(r, carry):
        base = pl.multiple_of(r * NORM_ROWS, NORM_ROWS)
        a = acc_ref[pl.ds(base, NORM_ROWS), :]
        mu = jnp.mean(a, axis=-1, keepdims=True)
        d = a - mu
        var = jnp.mean(d * d, axis=-1, keepdims=True)
        an = d * lax.rsqrt(var + LN_EPS) * lg_ref[...] + lb_ref[...]
        o_ref[pl.ds(base, NORM_ROWS), :] = (an * jax.nn.sigmoid(an)).astype(o_ref.dtype)
        return carry

    lax.fori_loop(0, ts // NORM_ROWS, norm_step, 0, unroll=4)


def conformer_conv(z, conv_w, conv_b, ln_g, ln_b, *, batch, seq, ts):
    ch = conv_w.shape[1]
    nts = seq // ts
    halo_blocks = ts // CONV_HALO
    kern = functools.partial(_conv_kernel, ts=ts, ch=ch)
    vec = lambda: pl.BlockSpec((1, ch), lambda b, i: (0, 0))
    return pl.pallas_call(
        kern, grid=(batch, nts),
        in_specs=[
            pl.BlockSpec((ts, 2 * ch), lambda b, i: (b * nts + i, 0)),
            pl.BlockSpec((CONV_HALO, 2 * ch),
                         lambda b, i: (jnp.maximum((b * nts + i) * halo_blocks - 1, 0), 0)),
            pl.BlockSpec((CONV_WIDTH, ch), lambda b, i: (0, 0)),
            vec(), vec(), vec(),
        ],
        out_specs=pl.BlockSpec((ts, ch), lambda b, i: (b * nts + i, 0)),
        out_shape=jax.ShapeDtypeStruct((batch * seq, ch), BF16),
        scratch_shapes=[
            pltpu.VMEM((CONV_HALO + ts, ch), F32),
            pltpu.VMEM((SUBLANES, CONV_HALO + ts, ch), F32),
            pltpu.VMEM((ts, ch), F32),
        ],
        compiler_params=_cparams("parallel", "arbitrary"), name="conformer_conv",
    )(z, z, conv_w, conv_b, ln_g, ln_b)


QK_HALO = 16


def _mlstm_kernel(zqk_ref, halo_ref, zv_ref, zo_ref, g_ref, cw_ref, cb_ref, gb_ref, o_ref,
                  qb_ref, c_ref, n_ref, m_ref, *, nh, dh):
    L = MLSTM_CHUNK
    first = pl.program_id(1) == 0

    @pl.when(first)
    def _():
        c_ref[...] = jnp.zeros_like(c_ref)
        n_ref[...] = jnp.zeros_like(n_ref)
        m_ref[...] = jnp.zeros_like(m_ref)

    qb_ref[0:QK_HALO, :] = jnp.where(first, 0.0, halo_ref[...].astype(F32))
    qb_ref[QK_HALO:QK_HALO + L, :] = zqk_ref[...].astype(F32)
    lead = QK_HALO - (MLSTM_QK_CONV - 1)
    acc = jnp.broadcast_to(cb_ref[...], (L, 2 * nh * dh))
    for k in range(MLSTM_QK_CONV):
        acc = acc + cw_ref[k:k + 1, :] * qb_ref[lead + k:lead + k + L, :]
    qk = acc * jax.nn.sigmoid(acc)
    row = lax.broadcasted_iota(jnp.int32, (L, L), 0)
    col = lax.broadcasted_iota(jnp.int32, (L, L), 1)

    g = g_ref[...] + gb_ref[...]
    logf = jax.nn.log_sigmoid(g)
    causal = col <= row
    tri = causal.astype(F32)
    bcum = jnp.dot(tri, logf, preferred_element_type=F32, precision=lax.Precision.HIGHEST)
    g_t = g.T
    b_t = bcum.T
    scale = dh ** -0.5

    for h in range(nh):
        q = qk[:, h * dh:(h + 1) * dh].astype(BF16)
        kf = qk[:, (nh + h) * dh:(nh + h + 1) * dh] * scale
        k = kf.astype(BF16)
        v = zv_ref[:, h * dh:(h + 1) * dh]
        b_col = bcum[:, nh + h:nh + h + 1]
        b_row = b_t[nh + h:nh + h + 1, :]
        i_col = g[:, h:h + 1]
        i_row = g_t[h:h + 1, :]
        m_prev = m_ref[h, 0:1, 0:1]
        c_prev = c_ref[h]
        n_prev = n_ref[h, 0:1, :]

        log_d = jnp.where(causal, b_col - b_row + i_row, -jnp.inf)
        inter = b_col + m_prev
        m_t = jnp.maximum(inter, jnp.max(log_d, axis=-1, keepdims=True))
        s = lax.dot_general(q, k, (((1,), (1,)), ((), ())), preferred_element_type=F32)
        s = s * jnp.exp(log_d - m_t)
        w_int = jnp.exp(inter - m_t)
        num = (jnp.dot(s.astype(BF16), v, preferred_element_type=F32)
               + w_int * jnp.dot(q, c_prev.astype(BF16), preferred_element_type=F32))
        qn = jnp.sum(q.astype(F32) * n_prev, axis=-1, keepdims=True)
        den = jnp.sum(s, axis=-1, keepdims=True) + w_int * qn
        hval = num / jnp.maximum(jnp.abs(den), jnp.exp(-m_t))

        b_last = b_col[L - 1:L, :]
        gk = b_last - b_col + i_col
        m_new = jnp.maximum(b_last + m_prev, jnp.max(gk, axis=0, keepdims=True))
        w_k = jnp.exp(gk - m_new)
        decay = jnp.exp(b_last + m_prev - m_new)
        kw = kf * w_k
        c_ref[h] = decay * c_prev + lax.dot_general(
            kw.astype(BF16), v, (((0,), (0,)), ((), ())), preferred_element_type=F32)
        n_ref[h] = jnp.broadcast_to(decay * n_prev + jnp.sum(kw, axis=0, keepdims=True), (SUBLANES, dh))
        m_ref[h] = jnp.broadcast_to(m_new, (SUBLANES, LANES))

        gate_o = jax.nn.sigmoid(zo_ref[:, h * dh:(h + 1) * dh].astype(F32))
        o_ref[:, h * dh:(h + 1) * dh] = (gate_o * hval).astype(o_ref.dtype)


def mlstm(z, gates, qk_conv_w, qk_conv_b, gate_bias, *, batch, seq, col0):
    L = MLSTM_CHUNK
    nh = MLSTM_HEADS
    w = qk_conv_w.shape[1] // 2
    dh = w // nh
    nc = seq // L
    qk_blk = col0 // (2 * w)
    v_blk = (col0 + 2 * w) // w
    o_blk = v_blk + 1
    kern = functools.partial(_mlstm_kernel, nh=nh, dh=dh)
    return pl.pallas_call(
        kern, grid=(batch, nc),
        in_specs=[
            pl.BlockSpec((L, 2 * w), lambda b, c: (b * nc + c, qk_blk)),
            pl.BlockSpec((QK_HALO, 2 * w),
                         lambda b, c: (jnp.maximum((b * nc + c) * (L // QK_HALO) - 1, 0), qk_blk)),
            pl.BlockSpec((L, w), lambda b, c: (b * nc + c, v_blk)),
            pl.BlockSpec((L, w), lambda b, c: (b * nc + c, o_blk)),
            pl.BlockSpec((L, LANES), lambda b, c: (b * nc + c, 0)),
            pl.BlockSpec((MLSTM_QK_CONV, 2 * w), lambda b, c: (0, 0)),
            pl.BlockSpec((1, 2 * w), lambda b, c: (0, 0)),
            pl.BlockSpec((1, LANES), lambda b, c: (0, 0)),
        ],
        out_specs=pl.BlockSpec((L, w), lambda b, c: (b * nc + c, 0)),
        out_shape=jax.ShapeDtypeStruct((batch * seq, w), BF16),
        scratch_shapes=[
            pltpu.VMEM((QK_HALO + L, 2 * w), F32),
            pltpu.VMEM((nh, dh, dh), F32),
            pltpu.VMEM((nh, SUBLANES, dh), F32),
            pltpu.VMEM((nh, SUBLANES, LANES), F32),
        ],
        compiler_params=_cparams("parallel", "arbitrary"), name="mlstm",
    )(z, z, z, z, gates, qk_conv_w, qk_conv_b, gate_bias)


def _out_proj_kernel(a_ref, b_ref, wa_ref, wb_ref, r_ref, o_ref):
    o_ref[...] = (r_ref[...]
                  + jnp.dot(a_ref[...], wa_ref[...], preferred_element_type=F32)
                  + jnp.dot(b_ref[...], wb_ref[...], preferred_element_type=F32))


def out_proj(a, b, wa, wb, res, *, tm):
    t, d = res.shape
    ka, kb = a.shape[1], b.shape[1]
    return pl.pallas_call(
        _out_proj_kernel, grid=(t // tm,),
        in_specs=[
            pl.BlockSpec((tm, ka), lambda i: (i, 0)),
            pl.BlockSpec((tm, kb), lambda i: (i, 0)),
            pl.BlockSpec((ka, d), lambda i: (0, 0)),
            pl.BlockSpec((kb, d), lambda i: (0, 0)),
            pl.BlockSpec((tm, d), lambda i: (i, 0)),
        ],
        out_specs=pl.BlockSpec((tm, d), lambda i: (i, 0)),
        out_shape=jax.ShapeDtypeStruct((t, d), F32),
        compiler_params=_cparams("parallel"), name="out_proj",
    )(a, b, wa, wb, res)


def _ffn_kernel(x_ref, g_ref, w1_ref, w3_ref, w2_ref, o_ref, xn_ref):
    f = pl.program_id(1)

    @pl.when(f == 0)
    def _():
        x = x_ref[...]
        xn_ref[...] = _rms(x, g_ref[...]).astype(BF16)
        o_ref[...] = x

    xn = xn_ref[...]
    a = jnp.dot(xn, w1_ref[...], preferred_element_type=F32)
    b = jnp.dot(xn, w3_ref[...], preferred_element_type=F32)
    hid = (a * jax.nn.sigmoid(a) * b).astype(BF16)
    o_ref[...] += jnp.dot(hid, w2_ref[...], preferred_element_type=F32)


def ffn(x, g, w1, w3, w2, *, tm, tf):
    t, d = x.shape
    dff = w1.shape[1]
    return pl.pallas_call(
        _ffn_kernel, grid=(t // tm, dff // tf),
        in_specs=[
            pl.BlockSpec((tm, d), lambda i, f: (i, 0)),
            pl.BlockSpec((1, d), lambda i, f: (0, 0)),
            pl.BlockSpec((d, tf), lambda i, f: (0, f)),
            pl.BlockSpec((d, tf), lambda i, f: (0, f)),
            pl.BlockSpec((tf, d), lambda i, f: (f, 0)),
        ],
        out_specs=pl.BlockSpec((tm, d), lambda i, f: (i, 0)),
        out_shape=jax.ShapeDtypeStruct((t, d), F32),
        scratch_shapes=[pltpu.VMEM((tm, d), BF16)],
        compiler_params=_cparams("parallel", "arbitrary"), name="ffn",
    )(x, g, w1, w3, w2)


def _attn_kernel(q_ref, k_ref, v_ref, o_ref, qf_ref, kf_ref, vf_ref, qd_ref, kd_ref, vd_ref,
                 od_ref, ld_ref, on_ref, ln_ref, *, seq, dh):
    T = ATTN_BLOCK
    nblk = seq // T
    qf_ref[...] = q_ref[...].astype(F32) * (dh ** -0.5)
    kf_ref[...] = k_ref[...].astype(F32)
    vf_ref[...] = v_ref[...].astype(F32)
    kd_ref[0:T, :] = jnp.zeros((T, dh), BF16)
    vd_ref[0:T, :] = jnp.zeros((T, dh), BF16)

    qi = lax.broadcasted_iota(jnp.int32, (T, 2 * T), 0)
    ki = lax.broadcasted_iota(jnp.int32, (T, 2 * T), 1)
    dist = T + qi - ki
    band = (dist >= 0) & (dist <= T)

    for g, dil in enumerate(DILATIONS):
        ls = seq // dil
        nb = ls // T
        for r in range(dil):
            rows = pl.ds(r, ls, stride=dil) if dil > 1 else pl.ds(0, ls)
            qd_ref[r * ls:(r + 1) * ls, :] = qf_ref[rows, :].astype(BF16)
            kd_ref[T + r * ls:T + (r + 1) * ls, :] = kf_ref[rows, :].astype(BF16)
            vd_ref[T + r * ls:T + (r + 1) * ls, :] = vf_ref[rows, :].astype(BF16)

        def block(n, carry):
            base = pl.multiple_of(n * T, T)
            qb = qd_ref[pl.ds(base, T), :]
            kb = kd_ref[pl.ds(base, 2 * T), :]
            vb = vd_ref[pl.ds(base, 2 * T), :]
            s = lax.dot_general(qb, kb, (((1,), (1,)), ((), ())), preferred_element_type=F32)
            kmin = jnp.where(n % nb == 0, T, 0)
            s = jnp.where(band & (ki >= kmin), s, -jnp.inf)
            m = jnp.max(s, axis=-1, keepdims=True)
            p = jnp.exp(s - m)
            l = jnp.sum(p, axis=-1, keepdims=True)
            o = jnp.dot(p.astype(BF16), vb, preferred_element_type=F32) / l
            od_ref[pl.ds(base, T), :] = o
            ld_ref[pl.ds(base, T), :] = jnp.broadcast_to(m + jnp.log(l), (T, dh))
            return carry

        lax.fori_loop(0, nblk, block, 0, unroll=32)

        for r in range(dil):
            rows = pl.ds(r, ls, stride=dil) if dil > 1 else pl.ds(0, ls)
            on_ref[g, rows, :] = od_ref[r * ls:(r + 1) * ls, :]
            ln_ref[g, rows, :] = ld_ref[r * ls:(r + 1) * ls, :]

    def merge(n, carry):
        rows = pl.ds(pl.multiple_of(n * T, T), T)
        lses = [ln_ref[g, rows, :] for g in range(len(DILATIONS))]
        mx = functools.reduce(jnp.maximum, lses)
        ws = [jnp.exp(l - mx) for l in lses]
        tot = functools.reduce(lambda a, b: a + b, ws)
        acc = ws[0] * on_ref[0, rows, :]
        for g in range(1, len(DILATIONS)):
            acc = acc + ws[g] * on_ref[g, rows, :]
        o_ref[rows, :] = (acc / tot).astype(o_ref.dtype)
        return carry

    lax.fori_loop(0, nblk, merge, 0, unroll=8)


def dilated_attention(z, *, batch, seq, width):
    nh = ATTN_HEADS
    dh = width // nh
    ng = len(DILATIONS)
    kern = functools.partial(_attn_kernel, seq=seq, dh=dh)
    blk = lambda off: pl.BlockSpec((seq, dh), lambda b, h: (b, off + h))
    return pl.pallas_call(
        kern, grid=(batch, nh),
        in_specs=[blk(0), blk(nh), blk(2 * nh)],
        out_specs=pl.BlockSpec((seq, dh), lambda b, h: (b, h)),
        out_shape=jax.ShapeDtypeStruct((batch * seq, width), BF16),
        scratch_shapes=[
            pltpu.VMEM((seq, dh), F32), pltpu.VMEM((seq, dh), F32), pltpu.VMEM((seq, dh), F32),
            pltpu.VMEM((seq, dh), BF16),
            pltpu.VMEM((ATTN_BLOCK + seq, dh), BF16), pltpu.VMEM((ATTN_BLOCK + seq, dh), BF16),
            pltpu.VMEM((seq, dh), F32), pltpu.VMEM((seq, dh), F32),
            pltpu.VMEM((ng, seq, dh), F32), pltpu.VMEM((ng, seq, dh), F32),
        ],
        compiler_params=_cparams("parallel", "parallel"), name="dilated_attention",
    )(z, z, z)


S5_SEGS = SUBLANES
S5_SEG = 64
S5_PITCH = S5_SEG + 4
S5_TILE = S5_SEGS * S5_SEG
S5_ROWS = S5_SEGS * S5_PITCH
S5_UNROLL = S5_SEG
S5_CHUNK_IN = 256
S5_SLABS = S5_CHUNK_IN // SSM_GROUP * SSM_STATE // LANES


def _s5_kernel(u_ref, bre_ref, bim_ref, cre_ref, cim_ref, lre_ref, lim_ref, dsk_ref, wg_ref, bg_ref,
               o_ref, ug_ref, sre_ref, sim_ref, cyre_ref, cyim_ref, y_ref, *, nchunk):
    first = pl.program_id(1) == 0

    @pl.when(first)
    def _():
        cyre_ref[...] = jnp.zeros_like(cyre_ref)
        cyim_ref[...] = jnp.zeros_like(cyim_ref)
        ug_ref[...] = jnp.zeros_like(ug_ref)

    for i in range(S5_SEGS):
        ug_ref[i * S5_PITCH:i * S5_PITCH + S5_SEG, :] = u_ref[i * S5_SEG:(i + 1) * S5_SEG, :].astype(F32)

    seg_id = lax.broadcasted_iota(jnp.int32, (S5_SEGS, LANES), 0)
    for c in range(nchunk):
        par = (c % 2) * S5_SLABS
        ucols = slice(c * S5_CHUNK_IN, (c + 1) * S5_CHUNK_IN)
        ub = ug_ref[:, ucols].astype(BF16)
        bu_re = jnp.dot(ub, bre_ref[c], preferred_element_type=F32)
        bu_im = jnp.dot(ub, bim_ref[c], preferred_element_type=F32)
        for s in range(S5_SLABS):
            sre_ref[par + s] = bu_re[:, s * LANES:(s + 1) * LANES]
            sim_ref[par + s] = bu_im[:, s * LANES:(s + 1) * LANES]

        lam_re = [jnp.broadcast_to(lre_ref[c, :, s * LANES:(s + 1) * LANES], (S5_SEGS, LANES))
                  for s in range(S5_SLABS)]
        lam_im = [jnp.broadcast_to(lim_ref[c, :, s * LANES:(s + 1) * LANES], (S5_SEGS, LANES))
                  for s in range(S5_SLABS)]

        def step(j, st, store):
            rows = pl.ds(j, S5_SEGS, stride=S5_PITCH)
            new = []
            for s in range(S5_SLABS):
                pr, pi = st[2 * s], st[2 * s + 1]
                nr = lam_re[s] * pr - lam_im[s] * pi + sre_ref[par + s, rows, :]
                ni = lam_re[s] * pi + lam_im[s] * pr + sim_ref[par + s, rows, :]
                if store:
                    sre_ref[par + s, rows, :] = nr
                    sim_ref[par + s, rows, :] = ni
                new += [nr, ni]
            return tuple(new)

        zero = tuple(jnp.zeros((S5_SEGS, LANES), F32) for _ in range(2 * S5_SLABS))
        ends = lax.fori_loop(0, S5_SEG, lambda j, st: step(j, st, False), zero, unroll=S5_UNROLL)

        pw_re, pw_im = [l[0:1] for l in lam_re], [l[0:1] for l in lam_im]
        for _ in range(int(math.log2(S5_SEG))):
            pw_re, pw_im = ([a * a - b * b for a, b in zip(pw_re, pw_im)],
                            [2.0 * a * b for a, b in zip(pw_re, pw_im)])
        init = []
        for s in range(S5_SLABS):
            cols = slice(s * LANES, (s + 1) * LANES)
            cr, ci = cyre_ref[c, 0:1, cols], cyim_ref[c, 0:1, cols]
            in_re = jnp.zeros((S5_SEGS, LANES), F32)
            in_im = jnp.zeros((S5_SEGS, LANES), F32)
            for i in range(S5_SEGS):
                in_re = jnp.where(seg_id == i, cr, in_re)
                in_im = jnp.where(seg_id == i, ci, in_im)
                er, ei = ends[2 * s][i:i + 1], ends[2 * s + 1][i:i + 1]
                cr, ci = (pw_re[s] * cr - pw_im[s] * ci + er, pw_re[s] * ci + pw_im[s] * cr + ei)
            cyre_ref[c, :, cols] = jnp.broadcast_to(cr, (SUBLANES, LANES))
            cyim_ref[c, :, cols] = jnp.broadcast_to(ci, (SUBLANES, LANES))
            init += [in_re, in_im]

        lax.fori_loop(0, S5_SEG, lambda j, st: step(j, st, True), tuple(init), unroll=S5_UNROLL)

        st_re = jnp.concatenate([sre_ref[par + s] for s in range(S5_SLABS)], axis=1).astype(BF16)
        st_im = jnp.concatenate([sim_ref[par + s] for s in range(S5_SLABS)], axis=1).astype(BF16)
        y_ref[:, ucols] = (jnp.dot(st_re, cre_ref[c], preferred_element_type=F32)
                           + jnp.dot(st_im, cim_ref[c], preferred_element_type=F32))

    for i in range(S5_SEGS):
        rows = slice(i * S5_PITCH, i * S5_PITCH + S5_SEG)
        y = y_ref[rows, :] + dsk_ref[...] * ug_ref[rows, :]
        y = jax.nn.gelu(y)
        gate = jnp.dot(y.astype(BF16), wg_ref[...], preferred_element_type=F32) + bg_ref[...]
        o_ref[i * S5_SEG:(i + 1) * S5_SEG, :] = (y * jax.nn.sigmoid(gate)).astype(o_ref.dtype)


def s5(z, b_re, b_im, c_re, c_im, lam_re, lam_im, d_skip, w_glu, b_glu, *, batch, seq, col_blk):
    nchunk, _, nstate = b_re.shape
    ch = nchunk * S5_CHUNK_IN
    nt = seq // S5_TILE
    kern = functools.partial(_s5_kernel, nchunk=nchunk)
    full = lambda shape: pl.BlockSpec(shape, lambda b, i: (0,) * len(shape))
    return pl.pallas_call(
        kern, grid=(batch, nt),
        in_specs=[
            pl.BlockSpec((S5_TILE, ch), lambda b, i: (b * nt + i, col_blk)),
            full(b_re.shape), full(b_im.shape), full(c_re.shape), full(c_im.shape),
            full(lam_re.shape), full(lam_im.shape), full((1, ch)), full((ch, ch)), full((1, ch)),
        ],
        out_specs=pl.BlockSpec((S5_TILE, ch), lambda b, i: (b * nt + i, 0)),
        out_shape=jax.ShapeDtypeStruct((batch * seq, ch), BF16),
        scratch_shapes=[
            pltpu.VMEM((S5_ROWS, ch), F32),
            pltpu.VMEM((2 * S5_SLABS, S5_ROWS, LANES), F32),
            pltpu.VMEM((2 * S5_SLABS, S5_ROWS, LANES), F32),
            pltpu.VMEM((nchunk, SUBLANES, nstate), F32),
            pltpu.VMEM((nchunk, SUBLANES, nstate), F32),
            pltpu.VMEM((S5_ROWS, ch), F32),
        ],
        compiler_params=_cparams("parallel", "arbitrary"), name="s5",
    )(z, b_re, b_im, c_re, c_im, lam_re, lam_im, d_skip, w_glu, b_glu)


def s5_params(lam_re, lam_im, log_dt, b_re, b_im, c_re, c_im):
    ng, ns = lam_re.shape
    gpc = S5_CHUNK_IN // SSM_GROUP
    nchunk = ng // gpc
    dt = jnp.exp(log_dt)[:, None]
    mag = jnp.exp(lam_re * dt)
    lb_re, lb_im = mag * jnp.cos(lam_im * dt), mag * jnp.sin(lam_im * dt)
    den = lam_re * lam_re + lam_im * lam_im
    f_re = ((lb_re - 1.0) * lam_re + lb_im * lam_im) / den
    f_im = (lb_im * lam_re - (lb_re - 1.0) * lam_im) / den
    bb_re = f_re[..., None] * b_re - f_im[..., None] * b_im
    bb_im = f_re[..., None] * b_im + f_im[..., None] * b_re
    eye = jnp.eye(gpc, dtype=F32)

    def pack_b(m):
        m = m.reshape(nchunk, gpc, ns, SSM_GROUP)
        return jnp.einsum('cgpk,gh->cgkhp', m, eye).reshape(nchunk, gpc * SSM_GROUP, gpc * ns).astype(BF16)

    def pack_c(m):
        m = m.reshape(nchunk, gpc, SSM_GROUP, ns)
        return jnp.einsum('cgkp,gh->cgphk', m, eye).reshape(nchunk, gpc * ns, gpc * SSM_GROUP).astype(BF16)

    return (pack_b(bb_re), pack_b(bb_im), pack_c(c_re), pack_c(-c_im),
            lb_re.reshape(nchunk, 1, gpc * ns), lb_im.reshape(nchunk, 1, gpc * ns))


def _pack_bf16_pair(lo, hi):
    lo_bits = lax.bitcast_convert_type(lo.astype(BF16).astype(F32), jnp.uint32) >> 16
    hi_bits = lax.bitcast_convert_type(hi.astype(BF16).astype(F32), jnp.uint32) & jnp.uint32(0xFFFF0000)
    return lo_bits | hi_bits


def _unpack_bf16_pair(w):
    lo = lax.bitcast_convert_type(w << 16, F32)
    hi = lax.bitcast_convert_type(w & jnp.uint32(0xFFFF0000), F32)
    return lo, hi


def _split3(v):
    hi = v.astype(BF16)
    r1 = v - hi.astype(F32)
    mid = r1.astype(BF16)
    lo = (r1 - mid.astype(F32)).astype(BF16)
    return hi, mid, lo


def _out_proj_route_kernel(a_ref, b_ref, wa_ref, wb_ref, r_ref, g_ref, wr_ref, x_ref, xp_ref, eid_ref, gate_ref):
    x = (r_ref[...]
         + jnp.dot(a_ref[...], wa_ref[...], preferred_element_type=F32)
         + jnp.dot(b_ref[...], wb_ref[...], preferred_element_type=F32))
    x_ref[...] = x
    xn = _rms(x, g_ref[...])
    half = xn.shape[1] // 2
    xp_ref[...] = _pack_bf16_pair(xn[:, :half], xn[:, half:])
    xh, xm, xl = _split3(xn)
    ph = jnp.dot(xh, wr_ref[...], preferred_element_type=F32)
    pm = jnp.dot(xm, wr_ref[:, :2 * LANES], preferred_element_type=F32)
    pl_ = jnp.dot(xl, wr_ref[:, :LANES], preferred_element_type=F32)
    logits = (ph[:, :LANES] + (ph[:, LANES:2 * LANES] + pm[:, :LANES])
              + (ph[:, 2 * LANES:] + pm[:, LANES:] + pl_))
    lane = lax.broadcasted_iota(jnp.int32, logits.shape, 1)
    logits = jnp.where(lane < N_EXPERTS, logits, -jnp.inf)
    v1 = jnp.max(logits, axis=-1, keepdims=True)
    i1 = jnp.min(jnp.where(logits == v1, lane, LANES), axis=-1, keepdims=True)
    rest = jnp.where(lane == i1, -jnp.inf, logits)
    v2 = jnp.max(rest, axis=-1, keepdims=True)
    i2 = jnp.min(jnp.where(rest == v2, lane, LANES), axis=-1, keepdims=True)
    e2 = jnp.exp(v2 - v1)
    g1 = 1.0 / (1.0 + e2)
    g2 = e2 / (1.0 + e2)
    eid_ref[...] = jnp.where(lane == 0, i1, jnp.where(lane == 1, i2, 0))
    gate_ref[...] = jnp.where(lane == 0, g1, jnp.where(lane == 1, g2, 0.0))


def out_proj_route(a, b, wa, wb, res, g, w_router, *, tm):
    t, d = res.shape
    ka, kb = a.shape[1], b.shape[1]
    row = lambda n: pl.BlockSpec((tm, n), lambda i: (i, 0))
    full = lambda r, c: pl.BlockSpec((r, c), lambda i: (0, 0))
    return pl.pallas_call(
        _out_proj_route_kernel, grid=(t // tm,),
        in_specs=[row(ka), row(kb), full(ka, d), full(kb, d), row(d), full(1, d), full(d, 3 * LANES)],
        out_specs=[row(d), row(d // 2), row(LANES), row(LANES)],
        out_shape=[jax.ShapeDtypeStruct((t, d), F32), jax.ShapeDtypeStruct((t, d // 2), jnp.uint32),
                   jax.ShapeDtypeStruct((t, LANES), jnp.int32), jax.ShapeDtypeStruct((t, LANES), F32)],
        compiler_params=_cparams("parallel"), name="out_proj_route",
    )(a, b, wa, wb, res, g, w_router)


EXPERT_TM = 1024
EXPERT_TF = 512
COMBINE_TC = 512


def _gather_rows_per_step(tm, nf):
    return -(-tm // ((nf - 1) * SUBLANES)) * SUBLANES


def _expert_kernel(te_ref, tv_ref, src_ref, dst_ref, xp_hbm, w1_ref, w3_ref, w2_ref, ys_hbm,
                   gbuf, xb, acc, obuf, gsem, ssem, *, tm, nf, gr):
    i = pl.program_id(0)
    f = pl.program_id(1)
    ntiles = pl.num_programs(0)
    gb = gr * (nf - 1)
    half = xb.shape[1] // 2

    def gather(tile, r):
        return pltpu.make_async_copy(xp_hbm.at[pl.ds(src_ref[tile * gb + r], 1), :],
                                     gbuf.at[pl.ds(r, 1), :], gsem)

    def scatter(tile, r):
        return pltpu.make_async_copy(obuf.at[pl.ds(r, 1), :],
                                     ys_hbm.at[pl.ds(dst_ref[(tile + 1) * gb + r], 1), :], ssem)

    def for_rows(n, fn):
        def body(r, carry):
            fn(r)
            return carry
        lax.fori_loop(0, n, body, 0, unroll=8)

    @pl.when((i == 0) & (f == 0))
    def _():
        obuf[...] = jnp.zeros_like(obuf)
        for_rows(gb, lambda r: gather(0, r).start())

    @pl.when(f == 0)
    def _():
        for_rows(gb, lambda r: gather(i, r).wait())
        lo, hi = _unpack_bf16_pair(gbuf[0:tm, :])
        xb[:, :half] = lo.astype(BF16)
        xb[:, half:] = hi.astype(BF16)
        acc[...] = jnp.zeros_like(acc)

    def swiglu_rows(rows):
        x = xb[0:rows, :]
        a = jnp.dot(x, w1_ref[0].astype(BF16), preferred_element_type=F32)
        b = jnp.dot(x, w3_ref[0].astype(BF16), preferred_element_type=F32)
        hid = (a * jax.nn.sigmoid(a) * b).astype(BF16)
        acc[0:rows, :] += jnp.dot(hid, w2_ref[0].astype(BF16), preferred_element_type=F32)

    def step(rows, moving):
        if moving:
            for r in range(gr):
                gather(i + 1, f * gr + r).start()
                scatter(i - 1, f * gr + r).start()
        if rows:
            swiglu_rows(rows)

    used = tv_ref[i]
    is_last = f == nf - 1
    quarter = tm // 4
    conds = [(used > q * quarter) & (used <= (q + 1) * quarter) for q in range(4)] + [used == 0]
    for cond, rows in zip(conds, (quarter, 2 * quarter, 3 * quarter, tm, 0)):
        pl.when(cond & jnp.logical_not(is_last))(functools.partial(step, rows, True))
        pl.when(cond & is_last)(functools.partial(step, rows, False))

    @pl.when(is_last)
    def _():
        for_rows(gb, lambda r: scatter(i - 1, r).wait())
        y = acc[...]
        obuf[0:tm, :] = _pack_bf16_pair(y[:, :half], y[:, half:])

    @pl.when((i == ntiles - 1) & is_last)
    def _():
        for_rows(gb, lambda r: gather(i + 1, r).wait())
        for_rows(gb, lambda r: scatter(i, r).start())
        for_rows(gb, lambda r: scatter(i, r).wait())


def expert_ffn(tile_e, tile_v, src, dst, xp, w1, w3, w2, *, n_out_rows):
    tm, tf = EXPERT_TM, EXPERT_TF
    dh = xp.shape[1]
    d = 2 * dh
    dff = w1.shape[2]
    nf = dff // tf
    gr = _gather_rows_per_step(tm, nf)
    ntiles = tile_e.shape[0]
    fsel = lambda i, f, te, tv: jnp.where(tv[i] > 0, f, nf - 1)
    kern = functools.partial(_expert_kernel, tm=tm, nf=nf, gr=gr)
    return pl.pallas_call(
        kern,
        grid_spec=pltpu.PrefetchScalarGridSpec(
            num_scalar_prefetch=4, grid=(ntiles, nf),
            in_specs=[
                pl.BlockSpec(memory_space=pl.ANY),
                pl.BlockSpec((1, d, tf), lambda i, f, te, tv, s, t: (te[i], 0, fsel(i, f, te, tv))),
                pl.BlockSpec((1, d, tf), lambda i, f, te, tv, s, t: (te[i], 0, fsel(i, f, te, tv))),
                pl.BlockSpec((1, tf, d), lambda i, f, te, tv, s, t: (te[i], fsel(i, f, te, tv), 0)),
            ],
            out_specs=pl.BlockSpec(memory_space=pl.ANY),
            scratch_shapes=[
                pltpu.VMEM((gr * (nf - 1), dh), jnp.uint32),
                pltpu.VMEM((tm, d), BF16),
                pltpu.VMEM((tm, d), F32),
                pltpu.VMEM((gr * (nf - 1), dh), jnp.uint32),
                pltpu.SemaphoreType.DMA(()),
                pltpu.SemaphoreType.DMA(()),
            ]),
        out_shape=jax.ShapeDtypeStruct((n_out_rows, dh), jnp.uint32),
        compiler_params=_cparams("arbitrary", "arbitrary"), name="expert_ffn",
    )(tile_e, tile_v, src, dst, xp, w1, w3, w2)


def _combine_kernel(x_ref, gate_ref, g_ref, y1_ref, y2_ref, o_ref):
    gates = gate_ref[...]
    half = x_ref.shape[1] // 2
    lo1, hi1 = _unpack_bf16_pair(y1_ref[...])
    lo2, hi2 = _unpack_bf16_pair(y2_ref[...])
    g1, g2 = gates[:, 0:1], gates[:, 1:2]
    x = x_ref[...]
    x = jnp.concatenate([x[:, :half] + g1 * lo1 + g2 * lo2, x[:, half:] + g1 * hi1 + g2 * hi2], axis=1)
    o_ref[...] = _rms(x, g_ref[...])


def combine(x, gates, g, ys):
    t, d = x.shape
    tc = COMBINE_TC
    return pl.pallas_call(
        _combine_kernel, grid=(t // tc,),
        in_specs=[
            pl.BlockSpec((tc, d), lambda i: (i, 0)),
            pl.BlockSpec((tc, LANES), lambda i: (i, 0)),
            pl.BlockSpec((1, d), lambda i: (0, 0)),
            pl.BlockSpec((tc, d // 2), lambda i: (i, 0)),
            pl.BlockSpec((tc, d // 2), lambda i: (t // tc + i, 0)),
        ],
        out_specs=pl.BlockSpec((tc, d), lambda i: (i, 0)),
        out_shape=jax.ShapeDtypeStruct((t, d), F32),
        compiler_params=_cparams("parallel"), name="moe_combine",
    )(x, gates, g, ys, ys)


def route_slots(eid, *, dff):
    tm = EXPERT_TM
    nf = dff // EXPERT_TF
    gb = _gather_rows_per_step(tm, nf) * (nf - 1)
    t = eid.shape[0]
    na = 2 * t
    e_flat = eid.reshape(na)
    onehot = (e_flat[:, None] == jnp.arange(N_EXPERTS, dtype=jnp.int32)[None, :]).astype(jnp.int32)
    csum = jnp.cumsum(onehot, axis=0)
    rank = jnp.sum((csum - onehot) * onehot, axis=1)
    counts = csum[-1]
    padded = (counts + tm - 1) // tm * tm
    ends = jnp.cumsum(padded)
    slot = jnp.sum(onehot * (ends - padded)[None, :], axis=1) + rank
    ntiles = -(-(na + N_EXPERTS * tm) // tm)
    ns = ntiles * tm
    asg = jnp.full((ns,), -1, jnp.int32).at[slot].set(jnp.arange(na, dtype=jnp.int32))
    is_pad = asg < 0
    tok = jnp.where(is_pad, 0, asg // 2)
    pad_rank = jnp.cumsum(is_pad.astype(jnp.int32)) - 1
    dst = jnp.where(is_pad, na + gb + pad_rank, (asg % 2) * t + asg // 2).reshape(ntiles, tm)
    over = (na + gb + (ns - na) + jnp.arange(ntiles * (gb - tm), dtype=jnp.int32)).reshape(ntiles, gb - tm)
    dst = jnp.concatenate([na + jnp.arange(gb, dtype=jnp.int32),
                           jnp.concatenate([dst, over], axis=1).reshape(-1)])
    src = jnp.pad(tok.reshape(ntiles, tm), ((0, 1), (0, gb - tm))).reshape(-1)
    starts = jnp.arange(ntiles, dtype=jnp.int32) * tm
    tile_e = jnp.sum((starts[:, None] >= ends[None, :]).astype(jnp.int32), axis=1)
    used_end = jnp.concatenate([ends - padded + counts, jnp.zeros((1,), jnp.int32)])
    tile_v = jnp.clip(used_end[tile_e] - starts, 0, tm)
    last_e = jnp.max(jnp.where(tile_v > 0, tile_e, 0))
    tile_e = jnp.where(tile_v > 0, tile_e, last_e)
    n_out_rows = na + gb + (ns - na) + ntiles * (gb - tm)
    n_out_rows = -(-n_out_rows // COMBINE_TC) * COMBINE_TC
    return tile_e, tile_v, src, dst, n_out_rows


def kernel(x, e_norm_mix, e_w_in, e_conv_w, e_conv_b, e_ln_g, e_ln_b, e_qk_conv_w, e_qk_conv_b, e_b_i, e_b_f, e_w_out, e_norm_ffn, e_ffn_w1, e_ffn_w3, e_ffn_w2, o_norm_mix, o_w_in, o_lam_re, o_lam_im, o_log_dt, o_b_re, o_b_im, o_c_re, o_c_im, o_d_skip, o_w_glu, o_b_glu, o_w_out, o_norm_ffn, o_router, o_exp_w1, o_exp_w3, o_exp_w2, final_norm):
    batch, seq, d = x.shape
    t = batch * seq
    xt = x.reshape(t, d)
    row = lambda v: v.reshape(1, -1)
    assert e_norm_mix.shape[0] == 1 and o_norm_mix.shape[0] == 1, "one even and one odd layer"

    ch = e_conv_w.shape[2]
    mw = e_qk_conv_w.shape[2] // 2
    nh = MLSTM_HEADS
    main_cols = 2 * ch + 4 * mw
    w_in = e_w_in[0].astype(BF16)
    w_gates = jnp.pad(w_in[:, main_cols:], ((0, 0), (0, LANES - 2 * nh)))
    z0, gates0 = norm_matmul(xt, row(e_norm_mix[0]), w_in, w_gates, tm=512, tn=1024, n_out=main_cols)
    out_a = conformer_conv(z0, e_conv_w[0], row(e_conv_b[0]), row(e_ln_g[0]), row(e_ln_b[0]),
                           batch=batch, seq=seq, ts=512)
    gate_bias = jnp.pad(jnp.concatenate([e_b_i[0], e_b_f[0]]), (0, LANES - 2 * nh)).reshape(1, LANES)
    out_b = mlstm(z0, gates0, e_qk_conv_w[0], row(e_qk_conv_b[0]), gate_bias, batch=batch, seq=seq, col0=2 * ch)
    w_out = e_w_out[0].astype(BF16)
    x1 = out_proj(out_a, out_b, w_out[:ch], w_out[ch:], xt, tm=512)
    x2 = ffn(x1, row(e_norm_ffn[0]), e_ffn_w1[0].astype(BF16), e_ffn_w3[0].astype(BF16),
             e_ffn_w2[0].astype(BF16), tm=1024, tf=512)

    aw = o_w_in.shape[2] - o_d_skip.shape[1]
    aw //= 3
    sch = o_d_skip.shape[1]
    z1 = norm_matmul(x2, row(o_norm_mix[0]), o_w_in[0].astype(BF16), tm=512, tn=1024)
    out_c = dilated_attention(z1, batch=batch, seq=seq, width=aw)
    sp = s5_params(o_lam_re[0], o_lam_im[0], o_log_dt[0], o_b_re[0], o_b_im[0], o_c_re[0], o_c_im[0])
    out_d = s5(z1, *sp, row(o_d_skip[0]), o_w_glu[0].astype(BF16), row(o_b_glu[0]),
               batch=batch, seq=seq, col_blk=3 * aw // sch)
    w_out1 = o_w_out[0].astype(BF16)
    w_router = jnp.concatenate(
        [jnp.pad(p, ((0, 0), (0, LANES - N_EXPERTS))) for p in _split3(o_router[0])], axis=1)
    x3, xp3, eid, gates = out_proj_route(out_c, out_d, w_out1[:aw], w_out1[aw:], x2, row(o_norm_ffn[0]),
                                         w_router, tm=512)
    tile_e, tile_v, src, dst, n_out_rows = route_slots(eid[:, :2], dff=o_exp_w1.shape[3])
    ys = expert_ffn(tile_e, tile_v, src, dst, xp3, o_exp_w1[0], o_exp_w3[0], o_exp_w2[0], n_out_rows=n_out_rows)
    out = combine(x3, gates, row(final_norm), ys)
    return out.reshape(batch, seq, d)
```

```python
import functools
import math

import jax
import jax.numpy as jnp
from jax import lax
from jax.experimental import pallas as pl
from jax.experimental.pallas import tpu as pltpu

F32 = jnp.float32
BF16 = jnp.bfloat16

RMS_EPS = 1e-6
LN_EPS = 1e-5
CONV_WIDTH = 31
MLSTM_HEADS = 8
MLSTM_QK_CONV = 4
MLSTM_CHUNK = 128
ATTN_HEADS = 8
ATTN_BLOCK = 128
DILATIONS = (1, 4, 16)
SSM_GROUP = 16
SSM_STATE = 64
N_EXPERTS = 8
LANES = 128
SUBLANES = 8
VMEM_LIMIT = 56 * 1024 * 1024


def _cparams(*sem):
    return pltpu.CompilerParams(dimension_semantics=sem, vmem_limit_bytes=VMEM_LIMIT)


def _rms(x, g):
    return x * lax.rsqrt(jnp.mean(x * x, axis=-1, keepdims=True) + RMS_EPS) * g


def _norm_matmul_kernel(x_ref, g_ref, w_ref, o_ref, *, tn):
    xn = _rms(x_ref[...], g_ref[...]).astype(BF16)
    for j in range(o_ref.shape[1] // tn):
        cols = slice(j * tn, (j + 1) * tn)
        o_ref[:, cols] = jnp.dot(xn, w_ref[:, cols], preferred_element_type=F32).astype(o_ref.dtype)


def _norm_matmul_aux_kernel(x_ref, g_ref, w_ref, wa_ref, o_ref, oa_ref, *, tn):
    xn = _rms(x_ref[...], g_ref[...]).astype(BF16)
    oa_ref[...] = jnp.dot(xn, wa_ref[...], preferred_element_type=F32)
    for j in range(o_ref.shape[1] // tn):
        cols = slice(j * tn, (j + 1) * tn)
        o_ref[:, cols] = jnp.dot(xn, w_ref[:, cols], preferred_element_type=F32).astype(o_ref.dtype)


def norm_matmul(x, g, w, w_aux=None, *, tm, tn, n_out=None):
    t, d = x.shape
    n = w.shape[1] if n_out is None else n_out
    resident = lambda shape: pl.BlockSpec(shape, lambda i: (0, 0), pipeline_mode=pl.Buffered(1))
    x_spec = pl.BlockSpec((tm, d), lambda i: (i, 0))
    g_spec = pl.BlockSpec((1, d), lambda i: (0, 0))
    o_spec = pl.BlockSpec((tm, n), lambda i: (i, 0))
    if w_aux is None:
        return pl.pallas_call(
            functools.partial(_norm_matmul_kernel, tn=tn), grid=(t // tm,),
            in_specs=[x_spec, g_spec, resident(w.shape)], out_specs=o_spec,
            out_shape=jax.ShapeDtypeStruct((t, n), BF16),
            compiler_params=_cparams("parallel"), name="norm_matmul")(x, g, w)
    na = w_aux.shape[1]
    return pl.pallas_call(
        functools.partial(_norm_matmul_aux_kernel, tn=tn), grid=(t // tm,),
        in_specs=[x_spec, g_spec, resident(w.shape), resident(w_aux.shape)],
        out_specs=[o_spec, pl.BlockSpec((tm, na), lambda i: (i, 0))],
        out_shape=[jax.ShapeDtypeStruct((t, n), BF16), jax.ShapeDtypeStruct((t, na), F32)],
        compiler_params=_cparams("parallel"), name="norm_matmul_aux")(x, g, w, w_aux)


CONV_HALO = 32
CONV_ROWS = 64
NORM_ROWS = 32


def _conv_kernel(u_ref, halo_ref, w_ref, cb_ref, lg_ref, lb_ref, o_ref, buf_ref, xs_ref, acc_ref, *, ts, ch):
    first = pl.program_id(1) == 0

    def glu(u):
        u = u.astype(F32)
        return u[:, :ch] * jax.nn.sigmoid(u[:, ch:])

    buf_ref[0:CONV_HALO, :] = jnp.where(first, 0.0, glu(halo_ref[...]))
    buf_ref[CONV_HALO:CONV_HALO + ts, :] = glu(u_ref[...])
    n_shift = ts + CONV_HALO - SUBLANES
    for b in range(1, SUBLANES):
        xs_ref[b, 0:n_shift, :] = buf_ref[b:b + n_shift, :]

    lead = CONV_HALO - (CONV_WIDTH - 1)

    nsub = CONV_ROWS // SUBLANES
    for lc in range(ch // LANES):
        cols = slice(lc * LANES, (lc + 1) * LANES)
        taps = [jnp.broadcast_to(w_ref[k:k + 1, cols], (SUBLANES, LANES)) for k in range(CONV_WIDTH)]
        bias = jnp.broadcast_to(cb_ref[:, cols], (SUBLANES, LANES))

        def conv_step(r, carry, cols=cols, taps=taps, bias=bias):
            base = pl.multiple_of(r * CONV_ROWS, CONV_ROWS)
            acc = [bias] * nsub
            for k in range(CONV_WIDTH):
                a, b = divmod(lead + k, SUBLANES)
                for j in range(nsub):
                    rows = pl.ds(base + SUBLANES * (a + j), SUBLANES)
                    xk = buf_ref[rows, cols] if b == 0 else xs_ref[b, rows, cols]
                    acc[j] = acc[j] + taps[k] * xk
            for j in range(nsub):
                acc_ref[pl.ds(base + SUBLANES * j, SUBLANES), cols] = acc[j]
            return carry

        lax.fori_loop(0, ts // CONV_ROWS, conv_step, 0)

    def norm_step(r, carry):
        base = pl.multiple_of(r * NORM_ROWS, NORM_ROWS)
        a = acc_ref[pl.ds(base, NORM_ROWS), :]
        mu = jnp.mean(a, axis=-1, keepdims=True)
        d = a - mu
        var = jnp.mean(d * d, axis=-1, keepdims=True)
        an = d * lax.rsqrt(var + LN_EPS) * lg_ref[...] + lb_ref[...]
        o_ref[pl.ds(base, NORM_ROWS), :] = (an * jax.nn.sigmoid(an)).astype(o_ref.dtype)
        return carry

    lax.fori_loop(0, ts // NORM_ROWS, norm_step, 0, unroll=4)


def conformer_conv(z, conv_w, conv_b, ln_g, ln_b, *, batch, seq, ts):
    ch = conv_w.shape[1]
    nts = seq // ts
    halo_blocks = ts // CONV_HALO
    kern = functools.partial(_conv_kernel, ts=ts, ch=ch)
    vec = lambda: pl.BlockSpec((1, ch), lambda b, i: (0, 0))
    return pl.pallas_call(
        kern, grid=(batch, nts),
        in_specs=[
            pl.BlockSpec((ts, 2 * ch), lambda b, i: (b * nts + i, 0)),
            pl.BlockSpec((CONV_HALO, 2 * ch),
                         lambda b, i: (jnp.maximum((b * nts + i) * halo_blocks - 1, 0), 0)),
            pl.BlockSpec((CONV_WIDTH, ch), lambda b, i: (0, 0)),
            vec(), vec(), vec(),
        ],
        out_specs=pl.BlockSpec((ts, ch), lambda b, i: (b * nts + i, 0)),
        out_shape=jax.ShapeDtypeStruct((batch * seq, ch), BF16),
        scratch_shapes=[
            pltpu.VMEM((CONV_HALO + ts, ch), F32),
            pltpu.VMEM((SUBLANES, CONV_HALO + ts, ch), F32),
            pltpu.VMEM((ts, ch), F32),
        ],
        compiler_params=_cparams("parallel", "arbitrary"), name="conformer_conv",
    )(z, z, conv_w, conv_b, ln_g, ln_b)


QK_HALO = 16


def _mlstm_kernel(zqk_ref, halo_ref, zv_ref, zo_ref, g_ref, cw_ref, cb_ref, gb_ref, o_ref,
                  qb_ref, c_ref, n_ref, m_ref, *, nh, dh):
    L = MLSTM_CHUNK
    first = pl.program_id(1) == 0

    @pl.when(first)
    def _():
        c_ref[...] = jnp.zeros_like(c_ref)
        n_ref[...] = jnp.zeros_like(n_ref)
        m_ref[...] = jnp.zeros_like(m_ref)

    qb_ref[0:QK_HALO, :] = jnp.where(first, 0.0, halo_ref[...].astype(F32))
    qb_ref[QK_HALO:QK_HALO + L, :] = zqk_ref[...].astype(F32)
    lead = QK_HALO - (MLSTM_QK_CONV - 1)
    acc = jnp.broadcast_to(cb_ref[...], (L, 2 * nh * dh))
    for k in range(MLSTM_QK_CONV):
        acc = acc + cw_ref[k:k + 1, :] * qb_ref[lead + k:lead + k + L, :]
    qk = acc * jax.nn.sigmoid(acc)
    row = lax.broadcasted_iota(jnp.int32, (L, L), 0)
    col = lax.broadcasted_iota(jnp.int32, (L, L), 1)

    g = g_ref[...] + gb_ref[...]
    logf = jax.nn.log_sigmoid(g)
    causal = col <= row
    tri = causal.astype(F32)
    bcum = jnp.dot(tri, logf, preferred_element_type=F32, precision=lax.Precision.HIGHEST)
    g_t = g.T
    b_t = bcum.T
    scale = dh ** -0.5

    for h in range(nh):
        q = qk[:, h * dh:(h + 1) * dh].astype(BF16)
        kf = qk[:, (nh + h) * dh:(nh + h + 1) * dh] * scale
        k = kf.astype(BF16)
        v = zv_ref[:, h * dh:(h + 1) * dh]
        b_col = bcum[:, nh + h:nh + h + 1]
        b_row = b_t[nh + h:nh + h + 1, :]
        i_col = g[:, h:h + 1]
        i_row = g_t[h:h + 1, :]
        m_prev = m_ref[h, 0:1, 0:1]
        c_prev = c_ref[h]
        n_prev = n_ref[h, 0:1, :]

        log_d = jnp.where(causal, b_col - b_row + i_row, -jnp.inf)
        inter = b_col + m_prev
        m_t = jnp.maximum(inter, jnp.max(log_d, axis=-1, keepdims=True))
        s = lax.dot_general(q, k, (((1,), (1,)), ((), ())), preferred_element_type=F32)
        s = s * jnp.exp(log_d - m_t)
        w_int = jnp.exp(inter - m_t)
        num = (jnp.dot(s.astype(BF16), v, preferred_element_type=F32)
               + w_int * jnp.dot(q, c_prev.astype(BF16), preferred_element_type=F32))
        qn = jnp.sum(q.astype(F32) * n_prev, axis=-1, keepdims=True)
        den = jnp.sum(s, axis=-1, keepdims=True) + w_int * qn
        hval = num / jnp.maximum(jnp.abs(den), jnp.exp(-m_t))

        b_last = b_col[L - 1:L, :]
        gk = b_last - b_col + i_col
        m_new = jnp.maximum(b_last + m_prev, jnp.max(gk, axis=0, keepdims=True))
        w_k = jnp.exp(gk - m_new)
        decay = jnp.exp(b_last + m_prev - m_new)
        kw = kf * w_k
        c_ref[h] = decay * c_prev + lax.dot_general(
            kw.astype(BF16), v, (((0,), (0,)), ((), ())), preferred_element_type=F32)
        n_ref[h] = jnp.broadcast_to(decay * n_prev + jnp.sum(kw, axis=0, keepdims=True), (SUBLANES, dh))
        m_ref[h] = jnp.broadcast_to(m_new, (SUBLANES, LANES))

        gate_o = jax.nn.sigmoid(zo_ref[:, h * dh:(h + 1) * dh].astype(F32))
        o_ref[:, h * dh:(h + 1) * dh] = (gate_o * hval).astype(o_ref.dtype)


def mlstm(z, gates, qk_conv_w, qk_conv_b, gate_bias, *, batch, seq, col0):
    L = MLSTM_CHUNK
    nh = MLSTM_HEADS
    w = qk_conv_w.shape[1] // 2
    dh = w // nh
    nc = seq // L
    qk_blk = col0 // (2 * w)
    v_blk = (col0 + 2 * w) // w
    o_blk = v_blk + 1
    kern = functools.partial(_mlstm_kernel, nh=nh, dh=dh)
    return pl.pallas_call(
        kern, grid=(batch, nc),
        in_specs=[
            pl.BlockSpec((L, 2 * w), lambda b, c: (b * nc + c, qk_blk)),
            pl.BlockSpec((QK_HALO, 2 * w),
                         lambda b, c: (jnp.maximum((b * nc + c) * (L // QK_HALO) - 1, 0), qk_blk)),
            pl.BlockSpec((L, w), lambda b, c: (b * nc + c, v_blk)),
            pl.BlockSpec((L, w), lambda b, c: (b * nc + c, o_blk)),
            pl.BlockSpec((L, LANES), lambda b, c: (b * nc + c, 0)),
            pl.BlockSpec((MLSTM_QK_CONV, 2 * w), lambda b, c: (0, 0)),
            pl.BlockSpec((1, 2 * w), lambda b, c: (0, 0)),
            pl.BlockSpec((1, LANES), lambda b, c: (0, 0)),
        ],
        out_specs=pl.BlockSpec((L, w), lambda b, c: (b * nc + c, 0)),
        out_shape=jax.ShapeDtypeStruct((batch * seq, w), BF16),
        scratch_shapes=[
            pltpu.VMEM((QK_HALO + L, 2 * w), F32),
            pltpu.VMEM((nh, dh, dh), F32),
            pltpu.VMEM((nh, SUBLANES, dh), F32),
            pltpu.VMEM((nh, SUBLANES, LANES), F32),
        ],
        compiler_params=_cparams("parallel", "arbitrary"), name="mlstm",
    )(z, z, z, z, gates, qk_conv_w, qk_conv_b, gate_bias)


def _out_proj_kernel(a_ref, b_ref, wa_ref, wb_ref, r_ref, o_ref):
    o_ref[...] = (r_ref[...]
                  + jnp.dot(a_ref[...], wa_ref[...], preferred_element_type=F32)
                  + jnp.dot(b_ref[...], wb_ref[...], preferred_element_type=F32))


def out_proj(a, b, wa, wb, res, *, tm):
    t, d = res.shape
    ka, kb = a.shape[1], b.shape[1]
    return pl.pallas_call(
        _out_proj_kernel, grid=(t // tm,),
        in_specs=[
            pl.BlockSpec((tm, ka), lambda i: (i, 0)),
            pl.BlockSpec((tm, kb), lambda i: (i, 0)),
            pl.BlockSpec((ka, d), lambda i: (0, 0)),
            pl.BlockSpec((kb, d), lambda i: (0, 0)),
            pl.BlockSpec((tm, d), lambda i: (i, 0)),
        ],
        out_specs=pl.BlockSpec((tm, d), lambda i: (i, 0)),
        out_shape=jax.ShapeDtypeStruct((t, d), F32),
        compiler_params=_cparams("parallel"), name="out_proj",
    )(a, b, wa, wb, res)


def _ffn_kernel(x_ref, g_ref, w1_ref, w3_ref, w2_ref, o_ref, xn_ref):
    f = pl.program_id(1)

    @pl.when(f == 0)
    def _():
        x = x_ref[...]
        xn_ref[...] = _rms(x, g_ref[...]).astype(BF16)
        o_ref[...] = x

    xn = xn_ref[...]
    a = jnp.dot(xn, w1_ref[...], preferred_element_type=F32)
    b = jnp.dot(xn, w3_ref[...], preferred_element_type=F32)
    hid = (a * jax.nn.sigmoid(a) * b).astype(BF16)
    o_ref[...] += jnp.dot(hid, w2_ref[...], preferred_element_type=F32)


def ffn(x, g, w1, w3, w2, *, tm, tf):
    t, d = x.shape
    dff = w1.shape[1]
    return pl.pallas_call(
        _ffn_kernel, grid=(t // tm, dff // tf),
        in_specs=[
            pl.BlockSpec((tm, d), lambda i, f: (i, 0)),
            pl.BlockSpec((1, d), lambda i, f: (0, 0)),
            pl.BlockSpec((d, tf), lambda i, f: (0, f)),
            pl.BlockSpec((d, tf), lambda i, f: (0, f)),
            pl.BlockSpec((tf, d), lambda i, f: (f, 0)),
        ],
        out_specs=pl.BlockSpec((tm, d), lambda i, f: (i, 0)),
        out_shape=jax.ShapeDtypeStruct((t, d), F32),
        scratch_shapes=[pltpu.VMEM((tm, d), BF16)],
        compiler_params=_cparams("parallel", "arbitrary"), name="ffn",
    )(x, g, w1, w3, w2)


def _attn_kernel(q_ref, k_ref, v_ref, o_ref, qf_ref, kf_ref, vf_ref, qd_ref, kd_ref, vd_ref,
                 od_ref, ld_ref, on_ref, ln_ref, *, seq, dh):
    T = ATTN_BLOCK
    nblk = seq // T
    qf_ref[...] = q_ref[...].astype(F32) * (dh ** -0.5)
    kf_ref[...] = k_ref[...].astype(F32)
    vf_ref[...] = v_ref[...].astype(F32)
    kd_ref[0:T, :] = jnp.zeros((T, dh), BF16)
    vd_ref[0:T, :] = jnp.zeros((T, dh), BF16)

    qi = lax.broadcasted_iota(jnp.int32, (T, 2 * T), 0)
    ki = lax.broadcasted_iota(jnp.int32, (T, 2 * T), 1)
    dist = T + qi - ki
    band = (dist >= 0) & (dist <= T)

    for g, dil in enumerate(DILATIONS):
        ls = seq // dil
        nb = ls // T
        for r in range(dil):
            rows = pl.ds(r, ls, stride=dil) if dil > 1 else pl.ds(0, ls)
            qd_ref[r * ls:(r + 1) * ls, :] = qf_ref[rows, :].astype(BF16)
            kd_ref[T + r * ls:T + (r + 1) * ls, :] = kf_ref[rows, :].astype(BF16)
            vd_ref[T + r * ls:T + (r + 1) * ls, :] = vf_ref[rows, :].astype(BF16)

        def block(n, carry):
            base = pl.multiple_of(n * T, T)
            qb = qd_ref[pl.ds(base, T), :]
            kb = kd_ref[pl.ds(base, 2 * T), :]
            vb = vd_ref[pl.ds(base, 2 * T), :]
            s = lax.dot_general(qb, kb, (((1,), (1,)), ((), ())), preferred_element_type=F32)
            kmin = jnp.where(n % nb == 0, T, 0)
            s = jnp.where(band & (ki >= kmin), s, -jnp.inf)
            m = jnp.max(s, axis=-1, keepdims=True)
            p = jnp.exp(s - m)
            l = jnp.sum(p, axis=-1, keepdims=True)
            o = jnp.dot(p.astype(BF16), vb, preferred_element_type=F32) / l
            od_ref[pl.ds(base, T), :] = o
            ld_ref[pl.ds(base, T), :] = jnp.broadcast_to(m + jnp.log(l), (T, dh))
            return carry

        lax.fori_loop(0, nblk, block, 0, unroll=32)

        for r in range(dil):
            rows = pl.ds(r, ls, stride=dil) if dil > 1 else pl.ds(0, ls)
            on_ref[g, rows, :] = od_ref[r * ls:(r + 1) * ls, :]
            ln_ref[g, rows, :] = ld_ref[r * ls:(r + 1) * ls, :]

    def merge(n, carry):
        rows = pl.ds(pl.multiple_of(n * T, T), T)
        lses = [ln_ref[g, rows, :] for g in range(len(DILATIONS))]
        mx = functools.reduce(jnp.maximum, lses)
        ws = [jnp.exp(l - mx) for l in lses]
        tot = functools.reduce(lambda a, b: a + b, ws)
        acc = ws[0] * on_ref[0, rows, :]
        for g in range(1, len(DILATIONS)):
            acc = acc + ws[g] * on_ref[g, rows, :]
        o_ref[rows, :] = (acc / tot).astype(o_ref.dtype)
        return carry

    lax.fori_loop(0, nblk, merge, 0, unroll=8)


def dilated_attention(z, *, batch, seq, width):
    nh = ATTN_HEADS
    dh = width // nh
    ng = len(DILATIONS)
    kern = functools.partial(_attn_kernel, seq=seq, dh=dh)
    blk = lambda off: pl.BlockSpec((seq, dh), lambda b, h: (b, off + h))
    return pl.pallas_call(
        kern, grid=(batch, nh),
        in_specs=[blk(0), blk(nh), blk(2 * nh)],
        out_specs=pl.BlockSpec((seq, dh), lambda b, h: (b, h)),
        out_shape=jax.ShapeDtypeStruct((batch * seq, width), BF16),
        scratch_shapes=[
            pltpu.VMEM((seq, dh), F32), pltpu.VMEM((seq, dh), F32), pltpu.VMEM((seq, dh), F32),
            pltpu.VMEM((seq, dh), BF16),
            pltpu.VMEM((ATTN_BLOCK + seq, dh), BF16), pltpu.VMEM((ATTN_BLOCK + seq, dh), BF16),
            pltpu.VMEM((seq, dh), F32), pltpu.VMEM((seq, dh), F32),
            pltpu.VMEM((ng, seq, dh), F32), pltpu.VMEM((ng, seq, dh), F32),
        ],
        compiler_params=_cparams("parallel", "parallel"), name="dilated_attention",
    )(z, z, z)


S5_SEGS = SUBLANES
S5_SEG = 64
S5_PITCH = S5_SEG + 4
S5_TILE = S5_SEGS * S5_SEG
S5_ROWS = S5_SEGS * S5_PITCH
S5_UNROLL = S5_SEG
S5_CHUNK_IN = 256
S5_SLABS = S5_CHUNK_IN // SSM_GROUP * SSM_STATE // LANES


def _s5_kernel(u_ref, bre_ref, bim_ref, cre_ref, cim_ref, lre_ref, lim_ref, dsk_ref, wg_ref, bg_ref,
               o_ref, ug_ref, sre_ref, sim_ref, cyre_ref, cyim_ref, y_ref, *, nchunk):
    first = pl.program_id(1) == 0

    @pl.when(first)
    def _():
        cyre_ref[...] = jnp.zeros_like(cyre_ref)
        cyim_ref[...] = jnp.zeros_like(cyim_ref)
        ug_ref[...] = jnp.zeros_like(ug_ref)

    for i in range(S5_SEGS):
        ug_ref[i * S5_PITCH:i * S5_PITCH + S5_SEG, :] = u_ref[i * S5_SEG:(i + 1) * S5_SEG, :].astype(F32)

    seg_id = lax.broadcasted_iota(jnp.int32, (S5_SEGS, LANES), 0)
    for c in range(nchunk):
        par = (c % 2) * S5_SLABS
        ucols = slice(c * S5_CHUNK_IN, (c + 1) * S5_CHUNK_IN)
        ub = ug_ref[:, ucols].astype(BF16)
        bu_re = jnp.dot(ub, bre_ref[c], preferred_element_type=F32)
        bu_im = jnp.dot(ub, bim_ref[c], preferred_element_type=F32)
        for s in range(S5_SLABS):
            sre_ref[par + s] = bu_re[:, s * LANES:(s + 1) * LANES]
            sim_ref[par + s] = bu_im[:, s * LANES:(s + 1) * LANES]

        lam_re = [jnp.broadcast_to(lre_ref[c, :, s * LANES:(s + 1) * LANES], (S5_SEGS, LANES))
                  for s in range(S5_SLABS)]
        lam_im = [jnp.broadcast_to(lim_ref[c, :, s * LANES:(s + 1) * LANES], (S5_SEGS, LANES))
                  for s in range(S5_SLABS)]

        def step(j, st, store):
            rows = pl.ds(j, S5_SEGS, stride=S5_PITCH)
            new = []
            for s in range(S5_SLABS):
                pr, pi = st[2 * s], st[2 * s + 1]
                nr = lam_re[s] * pr - lam_im[s] * pi + sre_ref[par + s, rows, :]
                ni = lam_re[s] * pi + lam_im[s] * pr + sim_ref[par + s, rows, :]
                if store:
                    sre_ref[par + s, rows, :] = nr
                    sim_ref[par + s, rows, :] = ni
                new += [nr, ni]
            return tuple(new)

        zero = tuple(jnp.zeros((S5_SEGS, LANES), F32) for _ in range(2 * S5_SLABS))
        ends = lax.fori_loop(0, S5_SEG, lambda j, st: step(j, st, False), zero, unroll=S5_UNROLL)

        pw_re, pw_im = [l[0:1] for l in lam_re], [l[0:1] for l in lam_im]
        for _ in range(int(math.log2(S5_SEG))):
            pw_re, pw_im = ([a * a - b * b for a, b in zip(pw_re, pw_im)],
                            [2.0 * a * b for a, b in zip(pw_re, pw_im)])
        init = []
        for s in range(S5_SLABS):
            cols = slice(s * LANES, (s + 1) * LANES)
            cr, ci = cyre_ref[c, 0:1, cols], cyim_ref[c, 0:1, cols]
            in_re = jnp.zeros((S5_SEGS, LANES), F32)
            in_im = jnp.zeros((S5_SEGS, LANES), F32)
            for i in range(S5_SEGS):
                in_re = jnp.where(seg_id == i, cr, in_re)
                in_im = jnp.where(seg_id == i, ci, in_im)
                er, ei = ends[2 * s][i:i + 1], ends[2 * s + 1][i:i + 1]
                cr, ci = (pw_re[s] * cr - pw_im[s] * ci + er, pw_re[s] * ci + pw_im[s] * cr + ei)
            cyre_ref[c, :, cols] = jnp.broadcast_to(cr, (SUBLANES, LANES))
            cyim_ref[c, :, cols] = jnp.broadcast_to(ci, (SUBLANES, LANES))
            init += [in_re, in_im]

        lax.fori_loop(0, S5_SEG, lambda j, st: step(j, st, True), tuple(init), unroll=S5_UNROLL)

        st_re = jnp.concatenate([sre_ref[par + s] for s in range(S5_SLABS)], axis=1).astype(BF16)
        st_im = jnp.concatenate([sim_ref[par + s] for s in range(S5_SLABS)], axis=1).astype(BF16)
        y_ref[:, ucols] = (jnp.dot(st_re, cre_ref[c], preferred_element_type=F32)
                           + jnp.dot(st_im, cim_ref[c], preferred_element_type=F32))

    for i in range(S5_SEGS):
        rows = slice(i * S5_PITCH, i * S5_PITCH + S5_SEG)
        y = y_ref[rows, :] + dsk_ref[...] * ug_ref[rows, :]
        y = jax.nn.gelu(y)
        gate = jnp.dot(y.astype(BF16), wg_ref[...], preferred_element_type=F32) + bg_ref[...]
        o_ref[i * S5_SEG:(i + 1) * S5_SEG, :] = (y * jax.nn.sigmoid(gate)).astype(o_ref.dtype)


def s5(z, b_re, b_im, c_re, c_im, lam_re, lam_im, d_skip, w_glu, b_glu, *, batch, seq, col_blk):
    nchunk, _, nstate = b_re.shape
    ch = nchunk * S5_CHUNK_IN
    nt = seq // S5_TILE
    kern = functools.partial(_s5_kernel, nchunk=nchunk)
    full = lambda shape: pl.BlockSpec(shape, lambda b, i: (0,) * len(shape))
    return pl.pallas_call(
        kern, grid=(batch, nt),
        in_specs=[
            pl.BlockSpec((S5_TILE, ch), lambda b, i: (b * nt + i, col_blk)),
            full(b_re.shape), full(b_im.shape), full(c_re.shape), full(c_im.shape),
            full(lam_re.shape), full(lam_im.shape), full((1, ch)), full((ch, ch)), full((1, ch)),
        ],
        out_specs=pl.BlockSpec((S5_TILE, ch), lambda b, i: (b * nt + i, 0)),
        out_shape=jax.ShapeDtypeStruct((batch * seq, ch), BF16),
        scratch_shapes=[
            pltpu.VMEM((S5_ROWS, ch), F32),
            pltpu.VMEM((2 * S5_SLABS, S5_ROWS, LANES), F32),
            pltpu.VMEM((2 * S5_SLABS, S5_ROWS, LANES), F32),
            pltpu.VMEM((nchunk, SUBLANES, nstate), F32),
            pltpu.VMEM((nchunk, SUBLANES, nstate), F32),
            pltpu.VMEM((S5_ROWS, ch), F32),
        ],
        compiler_params=_cparams("parallel", "arbitrary"), name="s5",
    )(z, b_re, b_im, c_re, c_im, lam_re, lam_im, d_skip, w_glu, b_glu)


def s5_params(lam_re, lam_im, log_dt, b_re, b_im, c_re, c_im):
    ng, ns = lam_re.shape
    gpc = S5_CHUNK_IN // SSM_GROUP
    nchunk = ng // gpc
    dt = jnp.exp(log_dt)[:, None]
    mag = jnp.exp(lam_re * dt)
    lb_re, lb_im = mag * jnp.cos(lam_im * dt), mag * jnp.sin(lam_im * dt)
    den = lam_re * lam_re + lam_im * lam_im
    f_re = ((lb_re - 1.0) * lam_re + lb_im * lam_im) / den
    f_im = (lb_im * lam_re - (lb_re - 1.0) * lam_im) / den
    bb_re = f_re[..., None] * b_re - f_im[..., None] * b_im
    bb_im = f_re[..., None] * b_im + f_im[..., None] * b_re
    eye = jnp.eye(gpc, dtype=F32)

    def pack_b(m):
        m = m.reshape(nchunk, gpc, ns, SSM_GROUP)
        return jnp.einsum('cgpk,gh->cgkhp', m, eye).reshape(nchunk, gpc * SSM_GROUP, gpc * ns).astype(BF16)

    def pack_c(m):
        m = m.reshape(nchunk, gpc, SSM_GROUP, ns)
        return jnp.einsum('cgkp,gh->cgphk', m, eye).reshape(nchunk, gpc * ns, gpc * SSM_GROUP).astype(BF16)

    return (pack_b(bb_re), pack_b(bb_im), pack_c(c_re), pack_c(-c_im),
            lb_re.reshape(nchunk, 1, gpc * ns), lb_im.reshape(nchunk, 1, gpc * ns))


def _pack_bf16_pair(lo, hi):
    lo_bits = lax.bitcast_convert_type(lo.astype(BF16).astype(F32), jnp.uint32) >> 16
    hi_bits = lax.bitcast_convert_type(hi.astype(BF16).astype(F32), jnp.uint32) & jnp.uint32(0xFFFF0000)
    return lo_bits | hi_bits


def _unpack_bf16_pair(w):
    lo = lax.bitcast_convert_type(w << 16, F32)
    hi = lax.bitcast_convert_type(w & jnp.uint32(0xFFFF0000), F32)
    return lo, hi


def _split3(v):
    hi = v.astype(BF16)
    r1 = v - hi.astype(F32)
    mid = r1.astype(BF16)
    lo = (r1 - mid.astype(F32)).astype(BF16)
    return hi, mid, lo


def _out_proj_route_kernel(a_ref, b_ref, wa_ref, wb_ref, r_ref, g_ref, wr_ref, x_ref, xp_ref, eid_ref, gate_ref):
    x = (r_ref[...]
         + jnp.dot(a_ref[...], wa_ref[...], preferred_element_type=F32)
         + jnp.dot(b_ref[...], wb_ref[...], preferred_element_type=F32))
    x_ref[...] = x
    xn = _rms(x, g_ref[...])
    half = xn.shape[1] // 2
    xp_ref[...] = _pack_bf16_pair(xn[:, :half], xn[:, half:])
    xh, xm, xl = _split3(xn)
    ph = jnp.dot(xh, wr_ref[...], preferred_element_type=F32)
    pm = jnp.dot(xm, wr_ref[:, :2 * LANES], preferred_element_type=F32)
    pl_ = jnp.dot(xl, wr_ref[:, :LANES], preferred_element_type=F32)
    logits = (ph[:, :LANES] + (ph[:, LANES:2 * LANES] + pm[:, :LANES])
              + (ph[:, 2 * LANES:] + pm[:, LANES:] + pl_))
    lane = lax.broadcasted_iota(jnp.int32, logits.shape, 1)
    logits = jnp.where(lane < N_EXPERTS, logits, -jnp.inf)
    v1 = jnp.max(logits, axis=-1, keepdims=True)
    i1 = jnp.min(jnp.where(logits == v1, lane, LANES), axis=-1, keepdims=True)
    rest = jnp.where(lane == i1, -jnp.inf, logits)
    v2 = jnp.max(rest, axis=-1, keepdims=True)
    i2 = jnp.min(jnp.where(rest == v2, lane, LANES), axis=-1, keepdims=True)
    e2 = jnp.exp(v2 - v1)
    g1 = 1.0 / (1.0 + e2)
    g2 = e2 / (1.0 + e2)
    eid_ref[...] = jnp.where(lane == 0, i1, jnp.where(lane == 1, i2, 0))
    gate_ref[...] = jnp.where(lane == 0, g1, jnp.where(lane == 1, g2, 0.0))


def out_proj_route(a, b, wa, wb, res, g, w_router, *, tm):
    t, d = res.shape
    ka, kb = a.shape[1], b.shape[1]
    row = lambda n: pl.BlockSpec((tm, n), lambda i: (i, 0))
    full = lambda r, c: pl.BlockSpec((r, c), lambda i: (0, 0))
    return pl.pallas_call(
        _out_proj_route_kernel, grid=(t // tm,),
        in_specs=[row(ka), row(kb), full(ka, d), full(kb, d), row(d), full(1, d), full(d, 3 * LANES)],
        out_specs=[row(d), row(d // 2), row(LANES), row(LANES)],
        out_shape=[jax.ShapeDtypeStruct((t, d), F32), jax.ShapeDtypeStruct((t, d // 2), jnp.uint32),
                   jax.ShapeDtypeStruct((t, LANES), jnp.int32), jax.ShapeDtypeStruct((t, LANES), F32)],
        compiler_params=_cparams("parallel"), name="out_proj_route",
    )(a, b, wa, wb, res, g, w_router)


EXPERT_TM = 1024
EXPERT_TF = 512
COMBINE_TC = 512


def _gather_rows_per_step(tm, nf):
    return -(-tm // ((nf - 1) * SUBLANES)) * SUBLANES


def _expert_kernel(te_ref, tv_ref, src_ref, dst_ref, xp_hbm, w1_ref, w3_ref, w2_ref, ys_hbm,
                   gbuf, xb, acc, obuf, gsem, ssem, *, tm, nf, gr):
    i = pl.program_id(0)
    f = pl.program_id(1)
    ntiles = pl.num_programs(0)
    gb = gr * (nf - 1)
    half = xb.shape[1] // 2

    def gather(tile, r):
        return pltpu.make_async_copy(xp_hbm.at[pl.ds(src_ref[tile * gb + r], 1), :],
                                     gbuf.at[pl.ds(r, 1), :], gsem)

    def scatter(tile, r):
        return pltpu.make_async_copy(obuf.at[pl.ds(r, 1), :],
                                     ys_hbm.at[pl.ds(dst_ref[(tile + 1) * gb + r], 1), :], ssem)

    def for_rows(n, fn):
        def body(r, carry):
            fn(r)
            return carry
        lax.fori_loop(0, n, body, 0, unroll=8)

    @pl.when((i == 0) & (f == 0))
    def _():
        obuf[...] = jnp.zeros_like(obuf)
        for_rows(gb, lambda r: gather(0, r).start())

    @pl.when(f == 0)
    def _():
        for_rows(gb, lambda r: gather(i, r).wait())
        lo, hi = _unpack_bf16_pair(gbuf[0:tm, :])
        xb[:, :half] = lo.astype(BF16)
        xb[:, half:] = hi.astype(BF16)
        acc[...] = jnp.zeros_like(acc)

    def swiglu_rows(rows):
        x = xb[0:rows, :]
        a = jnp.dot(x, w1_ref[0].astype(BF16), preferred_element_type=F32)
        b = jnp.dot(x, w3_ref[0].astype(BF16), preferred_element_type=F32)
        hid = (a * jax.nn.sigmoid(a) * b).astype(BF16)
        acc[0:rows, :] += jnp.dot(hid, w2_ref[0].astype(BF16), preferred_element_type=F32)

    def step(rows, moving):
        if moving:
            for r in range(gr):
                gather(i + 1, f * gr + r).start()
                scatter(i - 1, f * gr + r).start(priority=r % 2)
        if rows:
            swiglu_rows(rows)

    used = tv_ref[i]
    is_last = f == nf - 1
    quarter = tm // 4
    conds = [(used > q * quarter) & (used <= (q + 1) * quarter) for q in range(4)] + [used == 0]
    for cond, rows in zip(conds, (quarter, 2 * quarter, 3 * quarter, tm, 0)):
        pl.when(cond & jnp.logical_not(is_last))(functools.partial(step, rows, True))
        pl.when(cond & is_last)(functools.partial(step, rows, False))

    @pl.when(is_last)
    def _():
        for_rows(gb, lambda r: scatter(i - 1, r).wait())
        y = acc[...]
        obuf[0:tm, :] = _pack_bf16_pair(y[:, :half], y[:, half:])

    @pl.when((i == ntiles - 1) & is_last)
    def _():
        for_rows(gb, lambda r: gather(i + 1, r).wait())
        for_rows(gb, lambda r: scatter(i, r).start())
        for_rows(gb, lambda r: scatter(i, r).wait())


def expert_ffn(tile_e, tile_v, src, dst, xp, w1, w3, w2, *, n_out_rows):
    tm, tf = EXPERT_TM, EXPERT_TF
    dh = xp.shape[1]
    d = 2 * dh
    dff = w1.shape[2]
    nf = dff // tf
    gr = _gather_rows_per_step(tm, nf)
    ntiles = tile_e.shape[0]
    fsel = lambda i, f, te, tv: jnp.where(tv[i] > 0, f, nf - 1)
    kern = functools.partial(_expert_kernel, tm=tm, nf=nf, gr=gr)
    return pl.pallas_call(
        kern,
        grid_spec=pltpu.PrefetchScalarGridSpec(
            num_scalar_prefetch=4, grid=(ntiles, nf),
            in_specs=[
                pl.BlockSpec(memory_space=pl.ANY),
                pl.BlockSpec((1, d, tf), lambda i, f, te, tv, s, t: (te[i], 0, fsel(i, f, te, tv))),
                pl.BlockSpec((1, d, tf), lambda i, f, te, tv, s, t: (te[i], 0, fsel(i, f, te, tv))),
                pl.BlockSpec((1, tf, d), lambda i, f, te, tv, s, t: (te[i], fsel(i, f, te, tv), 0)),
            ],
            out_specs=pl.BlockSpec(memory_space=pl.ANY),
            scratch_shapes=[
                pltpu.VMEM((gr * (nf - 1), dh), jnp.uint32),
                pltpu.VMEM((tm, d), BF16),
                pltpu.VMEM((tm, d), F32),
                pltpu.VMEM((gr * (nf - 1), dh), jnp.uint32),
                pltpu.SemaphoreType.DMA(()),
                pltpu.SemaphoreType.DMA(()),
            ]),
        out_shape=jax.ShapeDtypeStruct((n_out_rows, dh), jnp.uint32),
        compiler_params=_cparams("arbitrary", "arbitrary"), name="expert_ffn",
    )(tile_e, tile_v, src, dst, xp, w1, w3, w2)


def _combine_kernel(x_ref, gate_ref, g_ref, y1_ref, y2_ref, o_ref):
    gates = gate_ref[...]
    half = x_ref.shape[1] // 2
    lo1, hi1 = _unpack_bf16_pair(y1_ref[...])
    lo2, hi2 = _unpack_bf16_pair(y2_ref[...])
    g1, g2 = gates[:, 0:1], gates[:, 1:2]
    x = x_ref[...]
    x = jnp.concatenate([x[:, :half] + g1 * lo1 + g2 * lo2, x[:, half:] + g1 * hi1 + g2 * hi2], axis=1)
    o_ref[...] = _rms(x, g_ref[...])


def combine(x, gates, g, ys):
    t, d = x.shape
    tc = COMBINE_TC
    return pl.pallas_call(
        _combine_kernel, grid=(t // tc,),
        in_specs=[
            pl.BlockSpec((tc, d), lambda i: (i, 0)),
            pl.BlockSpec((tc, LANES), lambda i: (i, 0)),
            pl.BlockSpec((1, d), lambda i: (0, 0)),
            pl.BlockSpec((tc, d // 2), lambda i: (i, 0)),
            pl.BlockSpec((tc, d // 2), lambda i: (t // tc + i, 0)),
        ],
        out_specs=pl.BlockSpec((tc, d), lambda i: (i, 0)),
        out_shape=jax.ShapeDtypeStruct((t, d), F32),
        compiler_params=_cparams("parallel"), name="moe_combine",
    )(x, gates, g, ys, ys)


def route_slots(eid, *, dff):
    tm = EXPERT_TM
    nf = dff // EXPERT_TF
    gb = _gather_rows_per_step(tm, nf) * (nf - 1)
    t = eid.shape[0]
    na = 2 * t
    e_flat = eid.reshape(na)
    onehot = (e_flat[:, None] == jnp.arange(N_EXPERTS, dtype=jnp.int32)[None, :]).astype(jnp.int32)
    csum = jnp.cumsum(onehot, axis=0)
    rank = jnp.sum((csum - onehot) * onehot, axis=1)
    counts = csum[-1]
    padded = (counts + tm - 1) // tm * tm
    ends = jnp.cumsum(padded)
    slot = jnp.sum(onehot * (ends - padded)[None, :], axis=1) + rank
    ntiles = -(-(na + N_EXPERTS * tm) // tm)
    ns = ntiles * tm
    asg = jnp.full((ns,), -1, jnp.int32).at[slot].set(jnp.arange(na, dtype=jnp.int32))
    is_pad = asg < 0
    tok = jnp.where(is_pad, 0, asg // 2)
    pad_rank = jnp.cumsum(is_pad.astype(jnp.int32)) - 1
    dst = jnp.where(is_pad, na + gb + pad_rank, (asg % 2) * t + asg // 2).reshape(ntiles, tm)
    over = (na + gb + (ns - na) + jnp.arange(ntiles * (gb - tm), dtype=jnp.int32)).reshape(ntiles, gb - tm)
    dst = jnp.concatenate([na + jnp.arange(gb, dtype=jnp.int32),
                           jnp.concatenate([dst, over], axis=1).reshape(-1)])
    src = jnp.pad(tok.reshape(ntiles, tm), ((0, 1), (0, gb - tm))).reshape(-1)
    starts = jnp.arange(ntiles, dtype=jnp.int32) * tm
    tile_e = jnp.sum((starts[:, None] >= ends[None, :]).astype(jnp.int32), axis=1)
    used_end = jnp.concatenate([ends - padded + counts, jnp.zeros((1,), jnp.int32)])
    tile_v = jnp.clip(used_end[tile_e] - starts, 0, tm)
    last_e = jnp.max(jnp.where(tile_v > 0, tile_e, 0))
    tile_e = jnp.where(tile_v > 0, tile_e, last_e)
    n_out_rows = na + gb + (ns - na) + ntiles * (gb - tm)
    n_out_rows = -(-n_out_rows // COMBINE_TC) * COMBINE_TC
    return tile_e, tile_v, src, dst, n_out_rows


def kernel(x, e_norm_mix, e_w_in, e_conv_w, e_conv_b, e_ln_g, e_ln_b, e_qk_conv_w, e_qk_conv_b, e_b_i, e_b_f, e_w_out, e_norm_ffn, e_ffn_w1, e_ffn_w3, e_ffn_w2, o_norm_mix, o_w_in, o_lam_re, o_lam_im, o_log_dt, o_b_re, o_b_im, o_c_re, o_c_im, o_d_skip, o_w_glu, o_b_glu, o_w_out, o_norm_ffn, o_router, o_exp_w1, o_exp_w3, o_exp_w2, final_norm):
    batch, seq, d = x.shape
    t = batch * seq
    xt = x.reshape(t, d)
    row = lambda v: v.reshape(1, -1)
    assert e_norm_mix.shape[0] == 1 and o_norm_mix.shape[0] == 1, "one even and one odd layer"

    ch = e_conv_w.shape[2]
    mw = e_qk_conv_w.shape[2] // 2
    nh = MLSTM_HEADS
    main_cols = 2 * ch + 4 * mw
    w_in = e_w_in[0].astype(BF16)
    w_gates = jnp.pad(w_in[:, main_cols:], ((0, 0), (0, LANES - 2 * nh)))
    z0, gates0 = norm_matmul(xt, row(e_norm_mix[0]), w_in, w_gates, tm=512, tn=1024, n_out=main_cols)
    out_a = conformer_conv(z0, e_conv_w[0], row(e_conv_b[0]), row(e_ln_g[0]), row(e_ln_b[0]),
                           batch=batch, seq=seq, ts=512)
    gate_bias = jnp.pad(jnp.concatenate([e_b_i[0], e_b_f[0]]), (0, LANES - 2 * nh)).reshape(1, LANES)
    out_b = mlstm(z0, gates0, e_qk_conv_w[0], row(e_qk_conv_b[0]), gate_bias, batch=batch, seq=seq, col0=2 * ch)
    w_out = e_w_out[0].astype(BF16)
    x1 = out_proj(out_a, out_b, w_out[:ch], w_out[ch:], xt, tm=512)
    x2 = ffn(x1, row(e_norm_ffn[0]), e_ffn_w1[0].astype(BF16), e_ffn_w3[0].astype(BF16),
             e_ffn_w2[0].astype(BF16), tm=1024, tf=512)

    aw = o_w_in.shape[2] - o_d_skip.shape[1]
    aw //= 3
    sch = o_d_skip.shape[1]
    z1 = norm_matmul(x2, row(o_norm_mix[0]), o_w_in[0].astype(BF16), tm=512, tn=1024)
    out_c = dilated_attention(z1, batch=batch, seq=seq, width=aw)
    sp = s5_params(o_lam_re[0], o_lam_im[0], o_log_dt[0], o_b_re[0], o_b_im[0], o_c_re[0], o_c_im[0])
    out_d = s5(z1, *sp, row(o_d_skip[0]), o_w_glu[0].astype(BF16), row(o_b_glu[0]),
               batch=batch, seq=seq, col_blk=3 * aw // sch)
    w_out1 = o_w_out[0].astype(BF16)
    w_router = jnp.concatenate(
        [jnp.pad(p, ((0, 0), (0, LANES - N_EXPERTS))) for p in _split3(o_router[0])], axis=1)
    x3, xp3, eid, gates = out_proj_route(out_c, out_d, w_out1[:aw], w_out1[aw:], x2, row(o_norm_ffn[0]),
                                         w_router, tm=512)
    tile_e, tile_v, src, dst, n_out_rows = route_slots(eid[:, :2], dff=o_exp_w1.shape[3])
    ys = expert_ffn(tile_e, tile_v, src, dst, xp3, o_exp_w1[0], o_exp_w3[0], o_exp_w2[0], n_out_rows=n_out_rows)
    out = combine(x3, gates, row(final_norm), ys)
    return out.reshape(batch, seq, d)
```
